```python
import jax, jax.numpy as jnp
from jax import lax
import numpy as np

D_MODEL = 2048
BATCH = 2
SEQ = 4096
DEPTH = 2
DEC_BATCH = 16
DEC_SEQ = 64
PAST_LEN = 2048

CHUNK = 64
QBLOCK = 128
EPS = 1e-6
D_FF = 5632
MLA_HEADS = 8
Q_LORA = 512
KV_LORA = 512
NOPE_DIM = 128
ROPE_DIM = 64
V_DIM = 128
ROPE_THETA = 10000.0
MLA_WIDTH = MLA_HEADS * V_DIM
CONV_DIM = 512
CONV_W = 3
SB_HEADS = 4
SB_DIM = 128
SB_WIDTH = SB_HEADS * SB_DIM
D_MIX = MLA_WIDTH + CONV_DIM + SB_WIDTH
D_IN = Q_LORA + KV_LORA + ROPE_DIM + 3 * CONV_DIM + 3 * SB_WIDTH
MLA_SCALE = (NOPE_DIM + ROPE_DIM) ** -0.5
SB_SCALE = SB_DIM ** -0.5
NEG_INF = -1e30

kernel_name = 'hybrid_mla_conv_stickbreak_stream_step'


def rms_norm(x, g):
    xf = x.astype(jnp.float32)
    y = xf * lax.rsqrt(jnp.mean(xf * xf, axis=-1, keepdims=True) + EPS)
    return (y * g.astype(jnp.float32)).astype(x.dtype)


def swiglu(x, w_gate, w_up, w_down):
    return (jax.nn.silu(x @ w_gate) * (x @ w_up)) @ w_down


def rope_tables(pos):
    half = ROPE_DIM // 2
    inv_freq = ROPE_THETA ** (-jnp.arange(half, dtype=jnp.float32) / half)
    ang = pos.astype(jnp.float32)[:, None] * inv_freq[None, :]
    return jnp.cos(ang), jnp.sin(ang)


def apply_rope(x, cos, sin):
    half = ROPE_DIM // 2
    xf = x.astype(jnp.float32)
    x1, x2 = xf[..., :half], xf[..., half:]
    return jnp.concatenate([x1 * cos - x2 * sin, x2 * cos + x1 * sin], axis=-1).astype(x.dtype)


def with_past(past, new):
    return new if past is None else jnp.concatenate([past.astype(new.dtype), new], axis=1)


def sweep_query_blocks(block_fn, q_inputs, q_pos):
    sq = q_pos.shape[0]
    if sq <= QBLOCK:
        return block_fn(q_inputs, q_pos)
    nb = sq // QBLOCK

    def split(a):
        return jnp.swapaxes(a.reshape(a.shape[0], nb, QBLOCK, *a.shape[2:]), 0, 1)

    blocks = tuple(split(a) for a in q_inputs)
    out = lax.map(lambda xs: block_fn(xs[0], xs[1]), (blocks, q_pos.reshape(nb, QBLOCK)))
    out = jnp.swapaxes(out, 0, 1)
    return out.reshape(out.shape[0], sq, *out.shape[3:])


def token_mixing(u, pos, past, lw):
    bsz, s, _ = u.shape
    offs = np.cumsum([Q_LORA, KV_LORA, ROPE_DIM, CONV_DIM, CONV_DIM, CONV_DIM, SB_WIDTH, SB_WIDTH]).tolist()
    z = u @ lw['w_in']
    c_q, c_kv, k_rope, gate_b, gate_c, x_conv, q_sb, k_sb, v_sb = jnp.split(z, offs, axis=-1)

    if past is None:
        past_ckv = past_kr = past_k = past_v = None
        conv_hist = jnp.zeros((bsz, CONV_W - 1, CONV_DIM), u.dtype)
        kpos = pos
    else:
        past_ckv, past_kr, past_k, past_v, conv_hist = past
        kpos = jnp.concatenate([jnp.arange(past_ckv.shape[1], dtype=jnp.int32), pos])

    c_q = rms_norm(c_q, lw['mla_q_norm_g'])
    c_kv = rms_norm(c_kv, lw['mla_kv_norm_g'])
    q = (c_q @ lw['mla_w_uq']).reshape(bsz, s, MLA_HEADS, NOPE_DIM + ROPE_DIM)
    cos, sin = rope_tables(pos)
    q_rope = apply_rope(q[..., NOPE_DIM:], cos[:, None, :], sin[:, None, :])
    k_rope = apply_rope(k_rope, cos, sin)
    q_lat = jnp.einsum('bshn,lhn->bshl', q[..., :NOPE_DIM], lw['mla_w_uk'])
    keys_ckv = with_past(past_ckv, c_kv)
    keys_kr = with_past(past_kr, k_rope)

    def mla_block(qb, qpos):
        ql, qr = qb
        scores = (jnp.einsum('bqhl,bkl->bhqk', ql, keys_ckv)
                  + jnp.einsum('bqhr,bkr->bhqk', qr, keys_kr)).astype(jnp.float32) * MLA_SCALE
        visible = (kpos[None, :] // CHUNK) <= (qpos[:, None] // CHUNK)
        p = jax.nn.softmax(jnp.where(visible, scores, NEG_INF), axis=-1).astype(keys_ckv.dtype)
        return jnp.einsum('bhqk,bkl->bqhl', p, keys_ckv)

    o_lat = sweep_query_blocks(mla_block, (q_lat, q_rope), pos)
    y_mla = jnp.einsum('bshl,lhd->bshd', o_lat, lw['mla_w_uv']).reshape(bsz, s, MLA_WIDTH)

    u_conv = gate_c * x_conv
    ext = jnp.concatenate([conv_hist.astype(u_conv.dtype), u_conv], axis=1)
    w = lw['conv_w']
    conv = w[0] * ext[:, 0:s]
    for k in range(1, CONV_W):
        conv = conv + w[k] * ext[:, k:k + s]
    y_conv = gate_b * conv
    new_conv = ext[:, -(CONV_W - 1):]

    q_sb = q_sb.reshape(bsz, s, SB_HEADS, SB_DIM)
    k_sb = k_sb.reshape(bsz, s, SB_HEADS, SB_DIM)
    v_sb = v_sb.reshape(bsz, s, SB_HEADS, SB_DIM)
    keys_k = with_past(past_k, k_sb)
    keys_v = with_past(past_v, v_sb)

    def sb_block(qb, qpos):
        (qq,) = qb
        logits = jnp.einsum('bqhd,bkhd->bhqk', qq, keys_k).astype(jnp.float32) * SB_SCALE
        earlier = kpos[None, :] < qpos[:, None]
        log_keep = jnp.where(earlier, jax.nn.log_sigmoid(-logits), 0.0)
        log_between = lax.cumsum(log_keep, axis=3, reverse=True) - log_keep
        weights = jnp.where(earlier, jnp.exp(jax.nn.log_sigmoid(logits) + log_between), 0.0)
        return jnp.einsum('bhqk,bkhd->bqhd', weights.astype(keys_v.dtype), keys_v)

    y_sb = sweep_query_blocks(sb_block, (q_sb,), pos).reshape(bsz, s, SB_WIDTH)

    merged = jnp.concatenate([rms_norm(y_mla, lw['out_norm_mla_g']),
                              rms_norm(y_conv, lw['out_norm_conv_g']),
                              rms_norm(y_sb, lw['out_norm_sb_g'])], axis=-1)
    y = merged @ lw['w_o']
    return y, (c_kv, k_rope, k_sb, v_sb, new_conv)


def trunk_layer(x, pos, past, lw):
    f1 = swiglu(rms_norm(x, lw['ffn1_pre_g']), lw['ffn1_w_gate'], lw['ffn1_w_up'], lw['ffn1_w_down'])
    h = x + 0.5 * rms_norm(f1, lw['ffn1_post_g'])
    m, new_state = token_mixing(rms_norm(h, lw['mix_pre_g']), pos, past, lw)
    h = h + rms_norm(m, lw['mix_post_g'])
    f2 = swiglu(rms_norm(h, lw['ffn2_pre_g']), lw['ffn2_w_gate'], lw['ffn2_w_up'], lw['ffn2_w_down'])
    h = h + 0.5 * rms_norm(f2, lw['ffn2_post_g'])
    return h, new_state


def setup_inputs(seed: int = 0) -> dict:
    key = jax.random.key(seed)
    ks = iter(jax.random.split(key, 40))

    def nrm(shape, scale):
        return scale * jax.random.normal(next(ks), shape, jnp.float32)

    def gain(shape):
        return 1.0 + 0.02 * jax.random.normal(next(ks), shape, jnp.float32)

    L = DEPTH
    return {
        'x_prompt': nrm((BATCH, SEQ, D_MODEL), 1.0),
        'x_sample': nrm((DEC_BATCH, DEC_SEQ, D_MODEL), 1.0),
        'cache_mla_ckv': nrm((L, DEC_BATCH, PAST_LEN, KV_LORA), 1.0),
        'cache_mla_krope': nrm((L, DEC_BATCH, PAST_LEN, ROPE_DIM), 1.0),
        'cache_sb_k': nrm((L, DEC_BATCH, PAST_LEN, SB_HEADS, SB_DIM), 1.0),
        'cache_sb_v': nrm((L, DEC_BATCH, PAST_LEN, SB_HEADS, SB_DIM), 1.0),
        'state_conv': nrm((L, DEC_BATCH, CONV_W - 1, CONV_DIM), 1.0),
        'ffn1_pre_g': gain((L, D_MODEL)),
        'ffn1_w_gate': nrm((L, D_MODEL, D_FF), D_MODEL ** -0.5),
        'ffn1_w_up': nrm((L, D_MODEL, D_FF), D_MODEL ** -0.5),
        'ffn1_w_down': nrm((L, D_FF, D_MODEL), D_FF ** -0.5),
        'ffn1_post_g': gain((L, D_MODEL)),
        'mix_pre_g': gain((L, D_MODEL)),
        'w_in': nrm((L, D_MODEL, D_IN), D_MODEL ** -0.5),
        'mla_q_norm_g': gain((L, Q_LORA)),
        'mla_kv_norm_g': gain((L, KV_LORA)),
        'mla_w_uq': nrm((L, Q_LORA, MLA_HEADS * (NOPE_DIM + ROPE_DIM)), Q_LORA ** -0.5),
        'mla_w_uk': nrm((L, KV_LORA, MLA_HEADS, NOPE_DIM), KV_LORA ** -0.5),
        'mla_w_uv': nrm((L, KV_LORA, MLA_HEADS, V_DIM), KV_LORA ** -0.5),
        'conv_w': nrm((L, CONV_W, CONV_DIM), CONV_W ** -0.5),
        'out_norm_mla_g': gain((L, MLA_WIDTH)),
        'out_norm_conv_g': gain((L, CONV_DIM)),
        'out_norm_sb_g': gain((L, SB_WIDTH)),
        'w_o': nrm((L, D_MIX, D_MODEL), D_MIX ** -0.5),
        'mix_post_g': gain((L, D_MODEL)),
        'ffn2_pre_g': gain((L, D_MODEL)),
        'ffn2_w_gate': nrm((L, D_MODEL, D_FF), D_MODEL ** -0.5),
        'ffn2_w_up': nrm((L, D_MODEL, D_FF), D_MODEL ** -0.5),
        'ffn2_w_down': nrm((L, D_FF, D_MODEL), D_FF ** -0.5),
        'ffn2_post_g': gain((L, D_MODEL)),
    }


def reference(x_prompt, x_sample, cache_mla_ckv, cache_mla_krope, cache_sb_k, cache_sb_v, state_conv,
              ffn1_pre_g, ffn1_w_gate, ffn1_w_up, ffn1_w_down, ffn1_post_g,
              mix_pre_g, w_in, mla_q_norm_g, mla_kv_norm_g, mla_w_uq, mla_w_uk, mla_w_uv, conv_w,
              out_norm_mla_g, out_norm_conv_g, out_norm_sb_g, w_o, mix_post_g,
              ffn2_pre_g, ffn2_w_gate, ffn2_w_up, ffn2_w_down, ffn2_post_g):
    past_len = cache_mla_ckv.shape[2]
    pos_p = jnp.arange(x_prompt.shape[1], dtype=jnp.int32)
    pos_s = past_len + jnp.arange(x_sample.shape[1], dtype=jnp.int32)

    h_p, h_s = x_prompt, x_sample
    new_p, new_s = [], []
    for l in range(DEPTH):
        lw = {
            'ffn1_pre_g': ffn1_pre_g[l], 'ffn1_w_gate': ffn1_w_gate[l], 'ffn1_w_up': ffn1_w_up[l],
            'ffn1_w_down': ffn1_w_down[l], 'ffn1_post_g': ffn1_post_g[l],
            'mix_pre_g': mix_pre_g[l], 'w_in': w_in[l],
            'mla_q_norm_g': mla_q_norm_g[l], 'mla_kv_norm_g': mla_kv_norm_g[l],
            'mla_w_uq': mla_w_uq[l], 'mla_w_uk': mla_w_uk[l], 'mla_w_uv': mla_w_uv[l],
            'conv_w': conv_w[l],
            'out_norm_mla_g': out_norm_mla_g[l], 'out_norm_conv_g': out_norm_conv_g[l],
            'out_norm_sb_g': out_norm_sb_g[l], 'w_o': w_o[l], 'mix_post_g': mix_post_g[l],
            'ffn2_pre_g': ffn2_pre_g[l], 'ffn2_w_gate': ffn2_w_gate[l], 'ffn2_w_up': ffn2_w_up[l],
            'ffn2_w_down': ffn2_w_down[l], 'ffn2_post_g': ffn2_post_g[l],
        }
        h_p, st_p = trunk_layer(h_p, pos_p, None, lw)
        past = (cache_mla_ckv[l], cache_mla_krope[l], cache_sb_k[l], cache_sb_v[l], state_conv[l])
        h_s, st_s = trunk_layer(h_s, pos_s, past, lw)
        new_p.append(st_p)
        new_s.append(st_s)

    def stacked(states, i):
        return jnp.stack([st[i] for st in states], axis=0)

    return (h_p, h_s,
            stacked(new_p, 0), stacked(new_p, 1), stacked(new_p, 2), stacked(new_p, 3), stacked(new_p, 4),
            stacked(new_s, 0), stacked(new_s, 1), stacked(new_s, 2), stacked(new_s, 3), stacked(new_s, 4))
```

```python
import functools

import jax
import jax.numpy as jnp
import numpy as np
from jax import lax
from jax.experimental import pallas as pl
from jax.experimental.pallas import tpu as pltpu

F32 = jnp.float32
BF16 = jnp.bfloat16

D_MODEL = 2048
CHUNK = 64
EPS = 1e-6
MLA_HEADS = 8
Q_LORA = 512
KV_LORA = 512
NOPE_DIM = 128
ROPE_DIM = 64
V_DIM = 128
ROPE_THETA = 10000.0
MLA_WIDTH = MLA_HEADS * V_DIM
CONV_DIM = 512
CONV_W = 3
SB_HEADS = 4
SB_DIM = 128
SB_WIDTH = SB_HEADS * SB_DIM
MLA_SCALE = (NOPE_DIM + ROPE_DIM) ** -0.5
SB_SCALE = SB_DIM ** -0.5
NEG_INF = -1e30
_CHUNK_SHIFT = CHUNK.bit_length() - 1
assert 1 << _CHUNK_SHIFT == CHUNK

LANES = 128
ROPE_SLOT = LANES
KV_W = KV_LORA + ROPE_SLOT
SEG = 64
VMEM_LIMIT = 56 * 1024 * 1024

_OFF_CQ, _OFF_CKV, _OFF_GB, _OFF_GC, _OFF_XC, _OFF_QSB, _OFF_KSB, _OFF_VSB = (512 * i for i in range(8))
_OFF_KR = 4096
_OFF_KROT = 4096 + ROPE_SLOT
W_IN_PACKED = 4096 + 2 * ROPE_SLOT
_UQ_ROPE = MLA_HEADS * NOPE_DIM
_UQ_ROT = _UQ_ROPE + MLA_HEADS * ROPE_SLOT
W_UQ_PACKED = _UQ_ROT + MLA_HEADS * ROPE_SLOT


def _params(*sem):
    return pltpu.CompilerParams(dimension_semantics=sem, vmem_limit_bytes=VMEM_LIMIT)


def _rms(x, g):
    return x * lax.rsqrt(jnp.mean(x * x, axis=-1, keepdims=True) + EPS) * g


def _dot(a, b):
    return jnp.dot(a, b, preferred_element_type=F32)


def _dot_nt(a, b):
    return lax.dot_general(a, b, (((1,), (1,)), ((), ())), preferred_element_type=F32)


def _pick_tile(cap, *sizes):
    t = cap
    while any(s % t for s in sizes):
        t //= 2
    return t


def _ffn_body(x_ref, pre_ref, wg_ref, wu_ref, wd_ref, post_ref, o_ref, xn_ref, acc_ref):
    j = pl.program_id(1)

    @pl.when(j == 0)
    def _():
        xn_ref[...] = _rms(x_ref[...], pre_ref[...]).astype(BF16)
        acc_ref[...] = jnp.zeros_like(acc_ref)

    xn = xn_ref[...]
    a = jax.nn.silu(_dot(xn, wg_ref[...])) * _dot(xn, wu_ref[...])
    acc_ref[...] += _dot(a.astype(BF16), wd_ref[...])

    @pl.when(j == pl.num_programs(1) - 1)
    def _():
        o_ref[...] = x_ref[...] + 0.5 * _rms(acc_ref[...], post_ref[...])


def _ffn(x, pre_g, wg, wu, wd, post_g, tm):
    r, d = x.shape
    dff = wg.shape[1]
    tf = _pick_tile(512, dff)
    return pl.pallas_call(
        _ffn_body,
        grid=(r // tm, dff // tf),
        in_specs=[
            pl.BlockSpec((tm, d), lambda i, j: (i, 0)),
            pl.BlockSpec((1, d), lambda i, j: (0, 0)),
            pl.BlockSpec((d, tf), lambda i, j: (0, j)),
            pl.BlockSpec((d, tf), lambda i, j: (0, j)),
            pl.BlockSpec((tf, d), lambda i, j: (j, 0)),
            pl.BlockSpec((1, d), lambda i, j: (0, 0)),
        ],
        out_specs=pl.BlockSpec((tm, d), lambda i, j: (i, 0)),
        out_shape=jax.ShapeDtypeStruct((r, d), F32),
        scratch_shapes=[pltpu.VMEM((tm, d), BF16), pltpu.VMEM((tm, d), F32)],
        compiler_params=_params("parallel", "arbitrary"),
        name="ffn",
    )(x, pre_g, wg, wu, wd, post_g)


def _proj_body(h_ref, pre_ref, win_ref, qg_ref, kvg_ref, wuq_ref, wuk_ref, cos_ref, sin_ref,
               q_ref, kv_ref, ckv_ref, kr_ref, gb_ref, uc_ref, qsb_ref, ksb_ref, vsb_ref,
               ksbh_ref, vsbh_ref):
    xn = _rms(h_ref[...], pre_ref[...]).astype(BF16)
    cos = cos_ref[...]
    sin = sin_ref[...]

    def col(off, width):
        return _dot(xn, win_ref[:, off:off + width])

    c_kv = _rms(col(_OFF_CKV, KV_LORA), kvg_ref[...])
    ckv_ref[...] = c_kv
    kv_ref[:, :KV_LORA] = c_kv.astype(BF16)
    k_rope = col(_OFF_KR, ROPE_SLOT) * cos + col(_OFF_KROT, ROPE_SLOT) * sin
    kv_ref[:, KV_LORA:] = k_rope.astype(BF16)
    kr_ref[...] = k_rope[:, :ROPE_DIM]

    gb_ref[...] = col(_OFF_GB, CONV_DIM)
    uc_ref[...] = col(_OFF_GC, CONV_DIM) * col(_OFF_XC, CONV_DIM)

    qsb_ref[...] = col(_OFF_QSB, SB_WIDTH).astype(BF16)
    k_sb = col(_OFF_KSB, SB_WIDTH)
    v_sb = col(_OFF_VSB, SB_WIDTH)
    ksb_ref[...] = k_sb
    vsb_ref[...] = v_sb
    ksbh_ref[...] = k_sb.astype(BF16)
    vsbh_ref[...] = v_sb.astype(BF16)

    c_q = _rms(col(_OFF_CQ, Q_LORA), qg_ref[...]).astype(BF16)
    q_nope = _dot(c_q, wuq_ref[:, :_UQ_ROPE])
    q_rope = _dot(c_q, wuq_ref[:, _UQ_ROPE:_UQ_ROT])
    q_rot = _dot(c_q, wuq_ref[:, _UQ_ROT:])
    for h in range(MLA_HEADS):
        nope = q_nope[:, h * NOPE_DIM:(h + 1) * NOPE_DIM].astype(BF16)
        q_ref[h, :, :KV_LORA] = _dot(nope, wuk_ref[h]).astype(BF16)
        sl = slice(h * ROPE_SLOT, (h + 1) * ROPE_SLOT)
        q_ref[h, :, KV_LORA:] = (q_rope[:, sl] * cos + q_rot[:, sl] * sin).astype(BF16)


def _proj(h, pre_g, w_in, qg, kvg, w_uq, w_ukt, cos, sin, tm):
    r, d = h.shape
    const = pl.Buffered(1)
    row = lambda w: pl.BlockSpec((tm, w), lambda i: (i, 0))
    full2 = lambda a: pl.BlockSpec(a.shape, lambda i: (0, 0), pipeline_mode=const)
    sds = jax.ShapeDtypeStruct
    return pl.pallas_call(
        _proj_body,
        grid=(r // tm,),
        in_specs=[
            row(d), full2(pre_g), full2(w_in), full2(qg), full2(kvg), full2(w_uq),
            pl.BlockSpec(w_ukt.shape, lambda i: (0, 0, 0), pipeline_mode=const),
            row(ROPE_SLOT), row(ROPE_SLOT),
        ],
        out_specs=[
            pl.BlockSpec((MLA_HEADS, tm, KV_W), lambda i: (0, i, 0)),
            row(KV_W), row(KV_LORA), row(ROPE_DIM), row(CONV_DIM), row(CONV_DIM),
            row(SB_WIDTH), row(SB_WIDTH), row(SB_WIDTH), row(SB_WIDTH), row(SB_WIDTH),
        ],
        out_shape=[
            sds((MLA_HEADS, r, KV_W), BF16), sds((r, KV_W), BF16), sds((r, KV_LORA), F32),
            sds((r, ROPE_DIM), F32), sds((r, CONV_DIM), F32), sds((r, CONV_DIM), F32),
            sds((r, SB_WIDTH), BF16), sds((r, SB_WIDTH), F32), sds((r, SB_WIDTH), F32),
            sds((r, SB_WIDTH), BF16), sds((r, SB_WIDTH), BF16),
        ],
        compiler_params=_params("parallel"),
        name="proj",
    )(h, pre_g, w_in, qg, kvg, w_uq, w_ukt, cos, sin)


def _mla_up(o, wuv_ref, o_ref, tq):
    for h in range(MLA_HEADS):
        oh = o[h * tq:(h + 1) * tq].astype(BF16)
        o_ref[:, h * V_DIM:(h + 1) * V_DIM] = _dot(oh, wuv_ref[h])


def _mla_prompt_body(q_ref, kv_ref, wuv_ref, o_ref, m_ref, l_ref, acc_ref, *, tq, tk):
    qi = pl.program_id(1)
    rows = MLA_HEADS * tq
    q = q_ref[...].reshape(rows, KV_W)
    m_ref[...] = jnp.full_like(m_ref, NEG_INF)
    l_ref[...] = jnp.zeros_like(l_ref)
    acc_ref[...] = jnp.zeros_like(acc_ref)

    def step(kt, masked):
        kv = kv_ref[pl.ds(pl.multiple_of(kt * tk, tk), tk), :]
        s = _dot_nt(q, kv) * MLA_SCALE
        if masked:
            qpos = qi * tq + (lax.broadcasted_iota(jnp.int32, (rows, 1), 0) & (tq - 1))
            kpos = kt * tk + lax.broadcasted_iota(jnp.int32, (1, tk), 1)
            s = jnp.where((kpos >> _CHUNK_SHIFT) <= (qpos >> _CHUNK_SHIFT), s, NEG_INF)
        m_old = m_ref[...]
        m_new = jnp.maximum(m_old, jnp.max(s, axis=-1, keepdims=True))
        alpha = jnp.exp(m_old - m_new)
        p = jnp.exp(s - m_new)
        l_ref[...] = alpha * l_ref[...] + jnp.sum(p, axis=-1, keepdims=True)
        acc_ref[...] = alpha * acc_ref[...] + _dot(p.astype(BF16), kv[:, :KV_LORA])
        m_ref[...] = m_new

    n_full = (qi * tq) // tk
    lax.fori_loop(0, n_full, lambda kt, c: (step(kt, False), c)[1], 0)
    for d in range(-(-tq // tk)):
        step(n_full + d, True)

    _mla_up(acc_ref[...] / l_ref[...], wuv_ref, o_ref, tq)


def _mla_prompt(q, kv, w_uv, batch, seq, tq, tk):
    nq = seq // tq
    rows = MLA_HEADS * tq
    return pl.pallas_call(
        functools.partial(_mla_prompt_body, tq=tq, tk=tk),
        grid=(batch, nq),
        in_specs=[
            pl.BlockSpec((MLA_HEADS, tq, KV_W), lambda b, i: (0, b * nq + i, 0)),
            pl.BlockSpec((seq, KV_W), lambda b, i: (b, 0)),
            pl.BlockSpec(w_uv.shape, lambda b, i: (0, 0, 0)),
        ],
        out_specs=pl.BlockSpec((tq, MLA_WIDTH), lambda b, i: (b * nq + i, 0)),
        out_shape=jax.ShapeDtypeStruct((batch * seq, MLA_WIDTH), F32),
        scratch_shapes=[pltpu.VMEM((rows, 1), F32), pltpu.VMEM((rows, 1), F32),
                        pltpu.VMEM((rows, KV_LORA), F32)],
        compiler_params=_params("parallel", "arbitrary"),
        name="mla_prompt",
    )(q, kv, w_uv)


def _mla_sample_body(q_ref, kvn_ref, cc_ref, ckr_ref, wuv_ref, o_ref, *, sq):
    rows = MLA_HEADS * sq
    q = q_ref[...].reshape(rows, KV_W)
    kvn = kvn_ref[...]
    cc = cc_ref[...].astype(BF16)
    ckr = ckr_ref[...].astype(BF16)
    s_c = (_dot_nt(q[:, :KV_LORA], cc) + _dot_nt(q[:, KV_LORA:KV_LORA + ROPE_DIM], ckr)) * MLA_SCALE
    s_n = _dot_nt(q, kvn) * MLA_SCALE
    m = jnp.maximum(jnp.max(s_c, axis=-1, keepdims=True), jnp.max(s_n, axis=-1, keepdims=True))
    p_c = jnp.exp(s_c - m)
    p_n = jnp.exp(s_n - m)
    l = jnp.sum(p_c, axis=-1, keepdims=True) + jnp.sum(p_n, axis=-1, keepdims=True)
    o = _dot(p_c.astype(BF16), cc) + _dot(p_n.astype(BF16), kvn[:, :KV_LORA])
    _mla_up(o / l, wuv_ref, o_ref, sq)


def _mla_sample(q, kv, cache_ckv, cache_kr, w_uv, layer, row0, dec_batch, sq):
    past = cache_ckv.shape[2]
    assert past % CHUNK == 0 and sq <= CHUNK
    b0 = row0 // sq
    return pl.pallas_call(
        functools.partial(_mla_sample_body, sq=sq),
        grid=(dec_batch,),
        in_specs=[
            pl.BlockSpec((MLA_HEADS, sq, KV_W), lambda b: (0, b0 + b, 0)),
            pl.BlockSpec((sq, KV_W), lambda b: (b0 + b, 0)),
            pl.BlockSpec((None, None, past, KV_LORA), lambda b: (layer, b, 0, 0)),
            pl.BlockSpec((None, None, past, ROPE_DIM), lambda b: (layer, b, 0, 0)),
            pl.BlockSpec(w_uv.shape, lambda b: (0, 0, 0)),
        ],
        out_specs=pl.BlockSpec((sq, MLA_WIDTH), lambda b: (b, 0)),
        out_shape=jax.ShapeDtypeStruct((dec_batch * sq, MLA_WIDTH), F32),
        compiler_params=_params("parallel"),
        name="mla_sample",
    )(q, kv, cache_ckv, cache_kr, w_uv)


def _suffix_ones(n):
    j = lax.broadcasted_iota(jnp.int32, (n, n), 0)
    k = lax.broadcasted_iota(jnp.int32, (n, n), 1)
    return jnp.where(j > k, 1.0, 0.0).astype(BF16)


def _sb_tile(qh, kk, vv, ones, carry, causal):
    logits = _dot_nt(qh, kk) * SB_SCALE
    softplus = jnp.maximum(logits, 0.0) + jnp.log(1.0 + jnp.exp(-jnp.abs(logits)))
    log_keep = -softplus
    if causal is not None:
        log_keep = jnp.where(causal, log_keep, 0.0)
    hi = log_keep.astype(BF16)
    lo = (log_keep - hi.astype(F32)).astype(BF16)
    between = _dot(hi, ones) + _dot(lo, ones) + carry
    w = jnp.exp((logits - softplus) + between)
    if causal is not None:
        w = jnp.where(causal, w, 0.0)
    return _dot(w.astype(BF16), vv), jnp.sum(log_keep, axis=-1, keepdims=True)


def _sb_prompt_body(q_ref, k_ref, v_ref, o_ref, acc_ref, carry_ref, *, t):
    qi = pl.program_id(1)
    ones = _suffix_ones(t)
    causal = lax.broadcasted_iota(jnp.int32, (t, t), 1) < lax.broadcasted_iota(jnp.int32, (t, t), 0)
    for h in range(SB_HEADS):
        hs = slice(h * SB_DIM, (h + 1) * SB_DIM)
        qh = q_ref[:, hs]

        def tile(kt, mask):
            rows = pl.ds(pl.multiple_of(kt * t, t), t)
            return _sb_tile(qh, k_ref[rows, hs], v_ref[rows, hs], ones, carry_ref[...], mask)

        carry_ref[...] = jnp.zeros_like(carry_ref)
        acc_ref[...], carry_ref[...] = tile(qi, causal)

        def step(i, c):
            out, tot = tile(qi - 1 - i, None)
            acc_ref[...] += out
            carry_ref[...] += tot
            return c

        lax.fori_loop(0, qi, step, 0)
        o_ref[:, hs] = acc_ref[...]


def _sb_prompt(q, k, v, batch, seq, t):
    nq = seq // t
    return pl.pallas_call(
        functools.partial(_sb_prompt_body, t=t),
        grid=(batch, nq),
        in_specs=[
            pl.BlockSpec((t, SB_WIDTH), lambda b, i: (b * nq + i, 0)),
            pl.BlockSpec((seq, SB_WIDTH), lambda b, i: (b, 0)),
            pl.BlockSpec((seq, SB_WIDTH), lambda b, i: (b, 0)),
        ],
        out_specs=pl.BlockSpec((t, SB_WIDTH), lambda b, i: (b * nq + i, 0)),
        out_shape=jax.ShapeDtypeStruct((batch * seq, SB_WIDTH), F32),
        scratch_shapes=[pltpu.VMEM((t, SB_DIM), F32), pltpu.VMEM((t, 1), F32)],
        compiler_params=_params("parallel", "arbitrary"),
        name="sb_prompt",
    )(q, k, v)


def _sb_sample_body(q_ref, kn_ref, vn_ref, ck_ref, cv_ref, o_ref, *, sq, past, tk):
    ones_new = _suffix_ones(sq)
    ones = _suffix_ones(tk)
    causal = lax.broadcasted_iota(jnp.int32, (sq, sq), 1) < lax.broadcasted_iota(jnp.int32, (sq, sq), 0)
    for h in range(SB_HEADS):
        hs = slice(h * SB_DIM, (h + 1) * SB_DIM)
        qh = q_ref[:, hs]
        acc, carry = _sb_tile(qh, kn_ref[:, hs], vn_ref[:, hs], ones_new, 0.0, causal)
        for kt in reversed(range(past // tk)):
            rows = slice(kt * tk, (kt + 1) * tk)
            out, tot = _sb_tile(qh, ck_ref[rows, hs].astype(BF16), cv_ref[rows, hs].astype(BF16),
                                ones, carry, None)
            acc += out
            carry += tot
        o_ref[:, hs] = acc


def _sb_sample(q, k, v, cache_k, cache_v, layer, row0, dec_batch, sq):
    past = cache_k.shape[2]
    b0 = row0 // sq
    new = lambda: pl.BlockSpec((sq, SB_WIDTH), lambda b: (b0 + b, 0))
    old = lambda: pl.BlockSpec((None, None, past, SB_WIDTH), lambda b: (layer, b, 0, 0))
    return pl.pallas_call(
        functools.partial(_sb_sample_body, sq=sq, past=past, tk=_pick_tile(256, past)),
        grid=(dec_batch,),
        in_specs=[new(), new(), new(), old(), old()],
        out_specs=pl.BlockSpec((sq, SB_WIDTH), lambda b: (b, 0)),
        out_shape=jax.ShapeDtypeStruct((dec_batch * sq, SB_WIDTH), F32),
        compiler_params=_params("parallel"),
        name="sb_sample",
    )(q, k, v, cache_k, cache_v)


def _merge_body(h_ref, ya_ref, gb_ref, uc_ref, prev_ref, yc_ref, cw_ref, ga_ref, gconv_ref, gc_ref,
                wo_ref, post_ref, o_ref, yb_ref, *, tm):
    cw = cw_ref[...]
    row = lax.broadcasted_iota(jnp.int32, (SEG, 1), 0)
    for s in range(tm // SEG):
        rows = slice(s * SEG, (s + 1) * SEG)
        u = uc_ref[rows, :]
        prev = prev_ref[s]
        u1 = jnp.where(row == 0, prev[1:2], pltpu.roll(u, 1, 0))
        u2 = jnp.where(row == 0, prev[0:1], jnp.where(row == 1, prev[1:2], pltpu.roll(u, 2, 0)))
        conv = cw[0:1] * u2 + cw[1:2] * u1 + cw[2:3] * u
        yb_ref[rows, :] = _rms(gb_ref[rows, :] * conv, gconv_ref[...]).astype(BF16)
    ya = _rms(ya_ref[...], ga_ref[...]).astype(BF16)
    yc = _rms(yc_ref[...], gc_ref[...]).astype(BF16)
    m = (_dot(ya, wo_ref[:MLA_WIDTH, :]) + _dot(yb_ref[...], wo_ref[MLA_WIDTH:MLA_WIDTH + CONV_DIM, :])
         + _dot(yc, wo_ref[MLA_WIDTH + CONV_DIM:, :]))
    o_ref[...] = h_ref[...] + _rms(m, post_ref[...])


def _merge(h, ya, gb, uc, prev, yc, conv_w, ga, gconv, gc, w_o, post_g, tm):
    r, d = h.shape
    row = lambda w: pl.BlockSpec((tm, w), lambda i: (i, 0))
    full2 = lambda a: pl.BlockSpec(a.shape, lambda i: (0, 0))
    return pl.pallas_call(
        functools.partial(_merge_body, tm=tm),
        grid=(r // tm,),
        in_specs=[
            row(d), row(MLA_WIDTH), row(CONV_DIM), row(CONV_DIM),
            pl.BlockSpec((tm // SEG, CONV_W - 1, CONV_DIM), lambda i: (i, 0, 0)),
            row(SB_WIDTH), full2(conv_w), full2(ga), full2(gconv), full2(gc), full2(w_o), full2(post_g),
        ],
        out_specs=row(d),
        out_shape=jax.ShapeDtypeStruct((r, d), F32),
        scratch_shapes=[pltpu.VMEM((tm, CONV_DIM), BF16)],
        compiler_params=_params("parallel"),
        name="merge",
    )(h, ya, gb, uc, prev, yc, conv_w, ga, gconv, gc, w_o, post_g)


def _rot_half_cols(w):
    half = ROPE_DIM // 2
    return jnp.concatenate([-w[:, half:], w[:, :half]], axis=1)


def _pad_slot(w):
    return jnp.pad(w, ((0, 0), (0, ROPE_SLOT - w.shape[1])))


def _pack_w_in(w):
    o = np.cumsum([0, Q_LORA, KV_LORA, ROPE_DIM, CONV_DIM, CONV_DIM, CONV_DIM, SB_WIDTH, SB_WIDTH, SB_WIDTH])
    g = [w[:, o[i]:o[i + 1]] for i in range(9)]
    c_q, c_kv, k_rope = g[0], g[1], g[2]
    return jnp.concatenate([c_q, c_kv] + g[3:] + [_pad_slot(k_rope), _pad_slot(_rot_half_cols(k_rope))],
                           axis=1).astype(BF16)


def _pack_w_uq(w):
    w = w.reshape(Q_LORA, MLA_HEADS, NOPE_DIM + ROPE_DIM)
    nope = w[:, :, :NOPE_DIM].reshape(Q_LORA, -1)
    rope = [w[:, h, NOPE_DIM:] for h in range(MLA_HEADS)]
    return jnp.concatenate([nope] + [_pad_slot(r) for r in rope] + [_pad_slot(_rot_half_cols(r)) for r in rope],
                           axis=1).astype(BF16)


def _rope_tables(pos):
    half = ROPE_DIM // 2
    inv_freq = ROPE_THETA ** (-jnp.arange(half, dtype=F32) / half)
    ang = pos.astype(F32)[:, None] * inv_freq[None, :]
    pad = lambda t: jnp.pad(jnp.concatenate([t, t], axis=1), ((0, 0), (0, ROPE_SLOT - ROPE_DIM)))
    return pad(jnp.cos(ang)), pad(jnp.sin(ang))


def kernel(x_prompt, x_sample, cache_mla_ckv, cache_mla_krope, cache_sb_k, cache_sb_v, state_conv, ffn1_pre_g, ffn1_w_gate, ffn1_w_up, ffn1_w_down, ffn1_post_g, mix_pre_g, w_in, mla_q_norm_g, mla_kv_norm_g, mla_w_uq, mla_w_uk, mla_w_uv, conv_w, out_norm_mla_g, out_norm_conv_g, out_norm_sb_g, w_o, mix_post_g, ffn2_pre_g, ffn2_w_gate, ffn2_w_up, ffn2_w_down, ffn2_post_g):
    batch, seq, d = x_prompt.shape
    dec_batch, sq, _ = x_sample.shape
    depth, _, past, _ = cache_mla_ckv.shape
    rp, rs = batch * seq, dec_batch * sq
    assert d == D_MODEL and sq == SEG and seq % SEG == 0

    tm_ffn = _pick_tile(512, rp, rs)
    tm = _pick_tile(256, rp, rs)
    t_attn = _pick_tile(256, seq)

    pos = jnp.concatenate([jnp.tile(jnp.arange(seq, dtype=jnp.int32), batch),
                           jnp.tile(past + jnp.arange(sq, dtype=jnp.int32), dec_batch)])
    cos, sin = _rope_tables(pos)
    cache_k = cache_sb_k.reshape(depth, dec_batch, past, SB_WIDTH)
    cache_v = cache_sb_v.reshape(depth, dec_batch, past, SB_WIDTH)
    vec = lambda g: g.reshape(1, -1)

    h = jnp.concatenate([x_prompt.reshape(rp, d), x_sample.reshape(rs, d)], axis=0)
    new = []
    for l in range(depth):
        h = _ffn(h, vec(ffn1_pre_g[l]), ffn1_w_gate[l].astype(BF16), ffn1_w_up[l].astype(BF16),
                 ffn1_w_down[l].astype(BF16), vec(ffn1_post_g[l]), tm_ffn)

        w_ukt = jnp.transpose(mla_w_uk[l], (1, 2, 0)).astype(BF16)
        w_uv = jnp.transpose(mla_w_uv[l], (1, 0, 2)).astype(BF16)
        q, kv, ckv, kr, gb, uc, qsb, ksb, vsb, ksbh, vsbh = _proj(
            h, vec(mix_pre_g[l]), _pack_w_in(w_in[l]), vec(mla_q_norm_g[l]), vec(mla_kv_norm_g[l]),
            _pack_w_uq(mla_w_uq[l]), w_ukt, cos, sin, tm)

        ya = jnp.concatenate([
            _mla_prompt(q, kv, w_uv, batch, seq, t_attn, t_attn),
            _mla_sample(q, kv, cache_mla_ckv, cache_mla_krope, w_uv, l, rp, dec_batch, sq)], axis=0)
        yc = jnp.concatenate([
            _sb_prompt(qsb, ksbh, vsbh, batch, seq, t_attn),
            _sb_sample(qsb, ksbh, vsbh, cache_k, cache_v, l, rp, dec_batch, sq)], axis=0)

        tails = uc[:rp].reshape(batch, seq // SEG, SEG, CONV_DIM)[:, :, SEG - (CONV_W - 1):]
        prev_p = jnp.concatenate([jnp.zeros_like(tails[:, :1]), tails[:, :-1]], axis=1)
        prev = jnp.concatenate([prev_p.reshape(-1, CONV_W - 1, CONV_DIM), state_conv[l]], axis=0)
        new_conv_s = uc[rp:].reshape(dec_batch, sq, CONV_DIM)[:, sq - (CONV_W - 1):]

        h = _merge(h, ya, gb, uc, prev, yc, conv_w[l], vec(out_norm_mla_g[l]), vec(out_norm_conv_g[l]),
                   vec(out_norm_sb_g[l]), w_o[l].astype(BF16), vec(mix_post_g[l]), tm)
        h = _ffn(h, vec(ffn2_pre_g[l]), ffn2_w_gate[l].astype(BF16), ffn2_w_up[l].astype(BF16),
                 ffn2_w_down[l].astype(BF16), vec(ffn2_post_g[l]), tm_ffn)

        new.append((ckv, kr, ksb, vsb, tails[:, -1], new_conv_s))

    def stack(i, lo, hi, shape):
        return jnp.stack([n[i][lo:hi].reshape(shape) for n in new], axis=0)

    p4, s4 = (batch, seq, SB_HEADS, SB_DIM), (dec_batch, sq, SB_HEADS, SB_DIM)
    return (h[:rp].reshape(batch, seq, d), h[rp:].reshape(dec_batch, sq, d),
            stack(0, 0, rp, (batch, seq, KV_LORA)), stack(1, 0, rp, (batch, seq, ROPE_DIM)),
            stack(2, 0, rp, p4), stack(3, 0, rp, p4), jnp.stack([n[4] for n in new], axis=0),
            stack(0, rp, rp + rs, (dec_batch, sq, KV_LORA)), stack(1, rp, rp + rs, (dec_batch, sq, ROPE_DIM)),
            stack(2, rp, rp + rs, s4), stack(3, rp, rp + rs, s4), jnp.stack([n[5] for n in new], axis=0))
```

```python
import functools

import jax
import jax.numpy as jnp
import numpy as np
from jax import lax
from jax.experimental import pallas as pl
from jax.experimental.pallas import tpu as pltpu

F32 = jnp.float32
BF16 = jnp.bfloat16

D_MODEL = 2048
CHUNK = 64
EPS = 1e-6
MLA_HEADS = 8
Q_LORA = 512
KV_LORA = 512
NOPE_DIM = 128
ROPE_DIM = 64
V_DIM = 128
ROPE_THETA = 10000.0
MLA_WIDTH = MLA_HEADS * V_DIM
CONV_DIM = 512
CONV_W = 3
SB_HEADS = 4
SB_DIM = 128
SB_WIDTH = SB_HEADS * SB_DIM
_SB_HEAD_COLS = [slice(h * SB_DIM, (h + 1) * SB_DIM) for h in range(SB_HEADS)]
MLA_SCALE = (NOPE_DIM + ROPE_DIM) ** -0.5
SB_SCALE = SB_DIM ** -0.5
NEG_INF = -1e30
_CHUNK_SHIFT = CHUNK.bit_length() - 1
assert 1 << _CHUNK_SHIFT == CHUNK

LANES = 128
ROPE_SLOT = LANES
KV_W = KV_LORA + ROPE_SLOT
SEG = 64
VMEM_LIMIT = 56 * 1024 * 1024

_OFF_CQ, _OFF_CKV, _OFF_GB, _OFF_GC, _OFF_XC, _OFF_QSB, _OFF_KSB, _OFF_VSB = (512 * i for i in range(8))
_OFF_KR = 4096
_OFF_KROT = 4096 + ROPE_SLOT
W_IN_PACKED = 4096 + 2 * ROPE_SLOT
_UQ_ROPE = MLA_HEADS * NOPE_DIM
_UQ_ROT = _UQ_ROPE + MLA_HEADS * ROPE_SLOT
W_UQ_PACKED = _UQ_ROT + MLA_HEADS * ROPE_SLOT


def _params(*sem):
    return pltpu.CompilerParams(dimension_semantics=sem, vmem_limit_bytes=VMEM_LIMIT)


def _rms(x, g):
    return x * lax.rsqrt(jnp.mean(x * x, axis=-1, keepdims=True) + EPS) * g


def _dot(a, b):
    return jnp.dot(a, b, preferred_element_type=F32)


def _dot_nt(a, b):
    return lax.dot_general(a, b, (((1,), (1,)), ((), ())), preferred_element_type=F32)


def _pick_tile(cap, *sizes):
    t = cap
    while any(s % t for s in sizes):
        t //= 2
    return t


def _ffn_body(x_ref, pre_ref, wg_ref, wu_ref, wd_ref, post_ref, o_ref, xn_ref, acc_ref):
    j = pl.program_id(1)

    @pl.when(j == 0)
    def _():
        xn_ref[...] = _rms(x_ref[...], pre_ref[...]).astype(BF16)
        acc_ref[...] = jnp.zeros_like(acc_ref)

    xn = xn_ref[...]
    a = jax.nn.silu(_dot(xn, wg_ref[...])) * _dot(xn, wu_ref[...])
    acc_ref[...] += _dot(a.astype(BF16), wd_ref[...])

    @pl.when(j == pl.num_programs(1) - 1)
    def _():
        o_ref[...] = x_ref[...] + 0.5 * _rms(acc_ref[...], post_ref[...])


def _ffn_chunk(dff):
    return _pick_tile(512, dff)


def _chunk_major(w, tf):
    d, dff = w.shape
    return jnp.transpose(w.reshape(d, dff // tf, tf), (1, 0, 2)).astype(BF16)


def _ffn(x, pre_g, wg, wu, wd, post_g, tm):
    r, d = x.shape
    nc, _, tf = wg.shape
    return pl.pallas_call(
        _ffn_body,
        grid=(r // tm, nc),
        in_specs=[
            pl.BlockSpec((tm, d), lambda i, j: (i, 0)),
            pl.BlockSpec((1, d), lambda i, j: (0, 0)),
            pl.BlockSpec((None, d, tf), lambda i, j: (j, 0, 0)),
            pl.BlockSpec((None, d, tf), lambda i, j: (j, 0, 0)),
            pl.BlockSpec((tf, d), lambda i, j: (j, 0)),
            pl.BlockSpec((1, d), lambda i, j: (0, 0)),
        ],
        out_specs=pl.BlockSpec((tm, d), lambda i, j: (i, 0)),
        out_shape=jax.ShapeDtypeStruct((r, d), F32),
        scratch_shapes=[pltpu.VMEM((tm, d), BF16), pltpu.VMEM((tm, d), F32)],
        compiler_params=_params("parallel", "arbitrary"),
        name="ffn",
    )(x, pre_g, wg, wu, wd, post_g)


def _proj_body(h_ref, pre_ref, win_ref, qg_ref, kvg_ref, wuq_ref, wuk_ref, cos_ref, sin_ref,
               q_ref, kv_ref, ckv_ref, kr_ref, gb_ref, uc_ref, qsb_ref, ksb_ref, vsb_ref,
               ksbh_ref, vsbh_ref):
    xn = _rms(h_ref[...], pre_ref[...]).astype(BF16)
    cos = cos_ref[...]
    sin = sin_ref[...]

    def col(off, width):
        return _dot(xn, win_ref[:, off:off + width])

    c_kv = _rms(col(_OFF_CKV, KV_LORA), kvg_ref[...])
    ckv_ref[...] = c_kv
    kv_ref[:, :KV_LORA] = c_kv.astype(BF16)
    k_rope = col(_OFF_KR, ROPE_SLOT) * cos + col(_OFF_KROT, ROPE_SLOT) * sin
    kv_ref[:, KV_LORA:] = k_rope.astype(BF16)
    kr_ref[...] = k_rope[:, :ROPE_DIM]

    gb_ref[...] = col(_OFF_GB, CONV_DIM)
    uc_ref[...] = col(_OFF_GC, CONV_DIM) * col(_OFF_XC, CONV_DIM)

    qsb_ref[...] = col(_OFF_QSB, SB_WIDTH).astype(BF16)
    k_sb = col(_OFF_KSB, SB_WIDTH)
    v_sb = col(_OFF_VSB, SB_WIDTH)
    tm = k_sb.shape[0]
    for h, hs in enumerate(_SB_HEAD_COLS):
        ksb_ref[pl.ds(h, tm, stride=SB_HEADS), :] = k_sb[:, hs]
        vsb_ref[pl.ds(h, tm, stride=SB_HEADS), :] = v_sb[:, hs]
    ksbh_ref[...] = k_sb.astype(BF16)
    vsbh_ref[...] = v_sb.astype(BF16)

    c_q = _rms(col(_OFF_CQ, Q_LORA), qg_ref[...]).astype(BF16)
    q_nope = _dot(c_q, wuq_ref[:, :_UQ_ROPE])
    q_rope = _dot(c_q, wuq_ref[:, _UQ_ROPE:_UQ_ROT])
    q_rot = _dot(c_q, wuq_ref[:, _UQ_ROT:])
    for h in range(MLA_HEADS):
        nope = q_nope[:, h * NOPE_DIM:(h + 1) * NOPE_DIM].astype(BF16)
        q_ref[h, :, :KV_LORA] = _dot(nope, wuk_ref[h]).astype(BF16)
        sl = slice(h * ROPE_SLOT, (h + 1) * ROPE_SLOT)
        q_ref[h, :, KV_LORA:] = (q_rope[:, sl] * cos + q_rot[:, sl] * sin).astype(BF16)


def _proj(h, pre_g, w_in, qg, kvg, w_uq, w_ukt, cos, sin, tm):
    r, d = h.shape
    const = pl.Buffered(1)
    row = lambda w: pl.BlockSpec((tm, w), lambda i: (i, 0))
    heads = lambda: pl.BlockSpec((tm * SB_HEADS, SB_DIM), lambda i: (i, 0))
    full2 = lambda a: pl.BlockSpec(a.shape, lambda i: (0, 0), pipeline_mode=const)
    sds = jax.ShapeDtypeStruct
    return pl.pallas_call(
        _proj_body,
        grid=(r // tm,),
        in_specs=[
            row(d), full2(pre_g), full2(w_in), full2(qg), full2(kvg), full2(w_uq),
            pl.BlockSpec(w_ukt.shape, lambda i: (0, 0, 0), pipeline_mode=const),
            row(ROPE_SLOT), row(ROPE_SLOT),
        ],
        out_specs=[
            pl.BlockSpec((MLA_HEADS, tm, KV_W), lambda i: (0, i, 0)),
            row(KV_W), row(KV_LORA), row(ROPE_DIM), row(CONV_DIM), row(CONV_DIM),
            row(SB_WIDTH), heads(), heads(), row(SB_WIDTH), row(SB_WIDTH),
        ],
        out_shape=[
            sds((MLA_HEADS, r, KV_W), BF16), sds((r, KV_W), BF16), sds((r, KV_LORA), F32),
            sds((r, ROPE_DIM), F32), sds((r, CONV_DIM), F32), sds((r, CONV_DIM), F32),
            sds((r, SB_WIDTH), BF16), sds((r * SB_HEADS, SB_DIM), F32), sds((r * SB_HEADS, SB_DIM), F32),
            sds((r, SB_WIDTH), BF16), sds((r, SB_WIDTH), BF16),
        ],
        compiler_params=_params("parallel"),
        name="proj",
    )(h, pre_g, w_in, qg, kvg, w_uq, w_ukt, cos, sin)


def _mla_up(o, wuv_ref, o_ref, tq):
    for h in range(MLA_HEADS):
        oh = o[h * tq:(h + 1) * tq].astype(BF16)
        o_ref[:, h * V_DIM:(h + 1) * V_DIM] = _dot(oh, wuv_ref[h])


def _mla_prompt_body(q_ref, kv_ref, wuv_ref, o_ref, m_ref, l_ref, acc_ref, *, tq, tk):
    qi = pl.program_id(1)
    m_ref[...] = jnp.full_like(m_ref, NEG_INF)
    l_ref[...] = jnp.zeros_like(l_ref)
    acc_ref[...] = jnp.zeros_like(acc_ref)

    def step(kt, masked):
        kv = kv_ref[pl.ds(pl.multiple_of(kt * tk, tk), tk), :]
        if masked:
            qpos = qi * tq + lax.broadcasted_iota(jnp.int32, (tq, 1), 0)
            kpos = kt * tk + lax.broadcasted_iota(jnp.int32, (1, tk), 1)
            visible = (kpos >> _CHUNK_SHIFT) <= (qpos >> _CHUNK_SHIFT)

        def scores(h):
            s = _dot_nt(q_ref[h], kv) * MLA_SCALE
            return jnp.where(visible, s, NEG_INF) if masked else s

        s_next = scores(0)
        for h in range(MLA_HEADS):
            s = s_next
            if h + 1 < MLA_HEADS:
                s_next = scores(h + 1)
            rs = slice(h * tq, (h + 1) * tq)
            m_old = m_ref[rs]
            m_new = jnp.maximum(m_old, jnp.max(s, axis=-1, keepdims=True))
            alpha = jnp.exp(m_old - m_new)
            p = jnp.exp(s - pltpu.repeat(m_new, tk // LANES, axis=1))
            l_ref[rs] = alpha * l_ref[rs] + sum(p[:, c * LANES:(c + 1) * LANES] for c in range(tk // LANES))
            acc_ref[rs] = (pltpu.repeat(alpha, KV_LORA // LANES, axis=1) * acc_ref[rs]
                           + _dot(p.astype(BF16), kv[:, :KV_LORA]))
            m_ref[rs] = m_new

    n_full = (qi * tq) // tk
    lax.fori_loop(0, n_full, lambda kt, c: (step(kt, False), c)[1], 0)
    for d in range(-(-tq // tk)):
        step(n_full + d, True)

    l = jnp.sum(l_ref[...], axis=-1, keepdims=True)
    _mla_up(acc_ref[...] / l, wuv_ref, o_ref, tq)


def _mla_prompt(q, kv, w_uv, batch, seq, tq, tk):
    nq = seq // tq
    rows = MLA_HEADS * tq
    return pl.pallas_call(
        functools.partial(_mla_prompt_body, tq=tq, tk=tk),
        grid=(batch, nq),
        in_specs=[
            pl.BlockSpec((MLA_HEADS, tq, KV_W), lambda b, i: (0, b * nq + i, 0)),
            pl.BlockSpec((seq, KV_W), lambda b, i: (b, 0)),
            pl.BlockSpec(w_uv.shape, lambda b, i: (0, 0, 0)),
        ],
        out_specs=pl.BlockSpec((tq, MLA_WIDTH), lambda b, i: (b * nq + i, 0)),
        out_shape=jax.ShapeDtypeStruct((batch * seq, MLA_WIDTH), F32),
        scratch_shapes=[pltpu.VMEM((rows, LANES), F32), pltpu.VMEM((rows, LANES), F32),
                        pltpu.VMEM((rows, KV_LORA), F32)],
        compiler_params=_params("parallel", "arbitrary"),
        name="mla_prompt",
    )(q, kv, w_uv)


def _mla_sample_body(q_ref, kvn_ref, cc_ref, ckr_ref, wuv_ref, o_ref, *, sq):
    rows = MLA_HEADS * sq
    q = q_ref[...].reshape(rows, KV_W)
    kvn = kvn_ref[...]
    cc = cc_ref[...].astype(BF16)
    ckr = ckr_ref[...].astype(BF16)
    s_c = (_dot_nt(q[:, :KV_LORA], cc) + _dot_nt(q[:, KV_LORA:KV_LORA + ROPE_DIM], ckr)) * MLA_SCALE
    s_n = _dot_nt(q, kvn) * MLA_SCALE
    m = jnp.maximum(jnp.max(s_c, axis=-1, keepdims=True), jnp.max(s_n, axis=-1, keepdims=True))
    p_c = jnp.exp(s_c - m)
    p_n = jnp.exp(s_n - m)
    l = jnp.sum(p_c, axis=-1, keepdims=True) + jnp.sum(p_n, axis=-1, keepdims=True)
    o = _dot(p_c.astype(BF16), cc) + _dot(p_n.astype(BF16), kvn[:, :KV_LORA])
    _mla_up(o / l, wuv_ref, o_ref, sq)


def _mla_sample(q, kv, cache_ckv, cache_kr, w_uv, layer, row0, dec_batch, sq):
    past = cache_ckv.shape[2]
    assert past % CHUNK == 0 and sq <= CHUNK
    b0 = row0 // sq
    return pl.pallas_call(
        functools.partial(_mla_sample_body, sq=sq),
        grid=(dec_batch,),
        in_specs=[
            pl.BlockSpec((MLA_HEADS, sq, KV_W), lambda b: (0, b0 + b, 0)),
            pl.BlockSpec((sq, KV_W), lambda b: (b0 + b, 0)),
            pl.BlockSpec((None, None, past, KV_LORA), lambda b: (layer, b, 0, 0)),
            pl.BlockSpec((None, None, past, ROPE_DIM), lambda b: (layer, b, 0, 0)),
            pl.BlockSpec(w_uv.shape, lambda b: (0, 0, 0)),
        ],
        out_specs=pl.BlockSpec((sq, MLA_WIDTH), lambda b: (b, 0)),
        out_shape=jax.ShapeDtypeStruct((dec_batch * sq, MLA_WIDTH), F32),
        compiler_params=_params("parallel"),
        name="mla_sample",
    )(q, kv, cache_ckv, cache_kr, w_uv)


def _suffix_ones(n):
    j = lax.broadcasted_iota(jnp.int32, (n, n), 0)
    k = lax.broadcasted_iota(jnp.int32, (n, n), 1)
    return jnp.where(j > k, 1.0, 0.0).astype(BF16)


def _sb_tiles(qs, ks, vs, carries, ones, causal):
    rows = qs[0].shape[0]
    logits = [_dot_nt(q, k) * SB_SCALE for q, k in zip(qs, ks)]
    softplus, log_keep, split = [], [], []
    for x in logits:
        sp = jnp.maximum(x, 0.0) + jnp.log(1.0 + jnp.exp(-jnp.abs(x)))
        lk = -sp if causal is None else jnp.where(causal, -sp, 0.0)
        hi = lk.astype(BF16)
        lo = (lk - hi.astype(F32)).astype(BF16)
        softplus.append(sp)
        log_keep.append(lk)
        split.append(jnp.concatenate([hi, lo], axis=0))
    sums = [_dot(x, ones) for x in split]
    weights = []
    for x, sp, sm, carry in zip(logits, softplus, sums, carries):
        between = sm[:rows] + sm[rows:] + carry
        w = jnp.exp((x - sp) + between)
        weights.append(w if causal is None else jnp.where(causal, w, 0.0))
    outs = [_dot(w.astype(BF16), v) for w, v in zip(weights, vs)]
    return outs, [jnp.sum(lk, axis=-1, keepdims=True) for lk in log_keep]


def _sb_prompt_body(q_ref, k_ref, v_ref, o_ref, carry_ref, *, t):
    qi = pl.program_id(1)
    ones = _suffix_ones(t)
    causal = lax.broadcasted_iota(jnp.int32, (t, t), 1) < lax.broadcasted_iota(jnp.int32, (t, t), 0)
    o_ref[...] = jnp.zeros_like(o_ref)
    carry_ref[...] = jnp.zeros_like(carry_ref)

    def tile(kt, mask):
        rows = pl.ds(pl.multiple_of(kt * t, t), t)
        outs, tots = _sb_tiles([q_ref[:, hs] for hs in _SB_HEAD_COLS], [k_ref[rows, hs] for hs in _SB_HEAD_COLS],
                               [v_ref[rows, hs] for hs in _SB_HEAD_COLS],
                               [pltpu.repeat(carry_ref[:, hs], t // LANES, axis=1) for hs in _SB_HEAD_COLS],
                               ones, mask)
        for hs, out, tot in zip(_SB_HEAD_COLS, outs, tots):
            o_ref[:, hs] += out
            carry_ref[:, hs] += tot

    tile(qi, causal)
    lax.fori_loop(0, qi, lambda i, c: (tile(qi - 1 - i, None), c)[1], 0)


def _sb_prompt(q, k, v, batch, seq, t):
    nq = seq // t
    return pl.pallas_call(
        functools.partial(_sb_prompt_body, t=t),
        grid=(batch, nq),
        in_specs=[
            pl.BlockSpec((t, SB_WIDTH), lambda b, i: (b * nq + i, 0)),
            pl.BlockSpec((seq, SB_WIDTH), lambda b, i: (b, 0)),
            pl.BlockSpec((seq, SB_WIDTH), lambda b, i: (b, 0)),
        ],
        out_specs=pl.BlockSpec((t, SB_WIDTH), lambda b, i: (b * nq + i, 0)),
        out_shape=jax.ShapeDtypeStruct((batch * seq, SB_WIDTH), F32),
        scratch_shapes=[pltpu.VMEM((t, SB_HEADS * LANES), F32)],
        compiler_params=_params("parallel", "arbitrary"),
        name="sb_prompt",
    )(q, k, v)


def _sb_sample_body(q_ref, kn_ref, vn_ref, ck_ref, cv_ref, o_ref, *, sq, past, tk):
    ones_new = _suffix_ones(sq)
    ones = _suffix_ones(tk)
    causal = lax.broadcasted_iota(jnp.int32, (sq, sq), 1) < lax.broadcasted_iota(jnp.int32, (sq, sq), 0)
    qs = [q_ref[:, hs] for hs in _SB_HEAD_COLS]
    accs, carries = _sb_tiles(qs, [kn_ref[:, hs] for hs in _SB_HEAD_COLS], [vn_ref[:, hs] for hs in _SB_HEAD_COLS],
                              [0.0] * SB_HEADS, ones_new, causal)
    for kt in reversed(range(past // tk)):
        rows = [pl.ds(kt * tk * SB_HEADS + h, tk, stride=SB_HEADS) for h in range(SB_HEADS)]
        outs, tots = _sb_tiles(qs, [ck_ref[r, :].astype(BF16) for r in rows],
                               [cv_ref[r, :].astype(BF16) for r in rows], carries, ones, None)
        accs = [a + o for a, o in zip(accs, outs)]
        carries = [c + t for c, t in zip(carries, tots)]
    for hs, acc in zip(_SB_HEAD_COLS, accs):
        o_ref[:, hs] = acc


def _sb_sample(q, k, v, cache_k, cache_v, layer, row0, dec_batch, sq):
    past = cache_k.shape[2] // SB_HEADS
    b0 = row0 // sq
    new = lambda: pl.BlockSpec((sq, SB_WIDTH), lambda b: (b0 + b, 0))
    old = lambda: pl.BlockSpec((None, None, past * SB_HEADS, SB_DIM), lambda b: (layer, b, 0, 0))
    return pl.pallas_call(
        functools.partial(_sb_sample_body, sq=sq, past=past, tk=_pick_tile(256, past)),
        grid=(dec_batch,),
        in_specs=[new(), new(), new(), old(), old()],
        out_specs=pl.BlockSpec((sq, SB_WIDTH), lambda b: (b, 0)),
        out_shape=jax.ShapeDtypeStruct((dec_batch * sq, SB_WIDTH), F32),
        compiler_params=_params("parallel"),
        name="sb_sample",
    )(q, k, v, cache_k, cache_v)


def _merge_body(h_ref, ya_ref, gb_ref, uc_ref, prev_ref, yc_ref, cw_ref, ga_ref, gconv_ref, gc_ref,
                wo_ref, post_ref, o_ref, yb_ref, *, tm):
    cw = cw_ref[...]
    row = lax.broadcasted_iota(jnp.int32, (SEG, 1), 0)
    for s in range(tm // SEG):
        rows = slice(s * SEG, (s + 1) * SEG)
        u = uc_ref[rows, :]
        prev = prev_ref[s]
        u1 = jnp.where(row == 0, prev[1:2], pltpu.roll(u, 1, 0))
        u2 = jnp.where(row == 0, prev[0:1], jnp.where(row == 1, prev[1:2], pltpu.roll(u, 2, 0)))
        conv = cw[0:1] * u2 + cw[1:2] * u1 + cw[2:3] * u
        yb_ref[rows, :] = _rms(gb_ref[rows, :] * conv, gconv_ref[...]).astype(BF16)
    ya = _rms(ya_ref[...], ga_ref[...]).astype(BF16)
    yc = _rms(yc_ref[...], gc_ref[...]).astype(BF16)
    m = (_dot(ya, wo_ref[:MLA_WIDTH, :]) + _dot(yb_ref[...], wo_ref[MLA_WIDTH:MLA_WIDTH + CONV_DIM, :])
         + _dot(yc, wo_ref[MLA_WIDTH + CONV_DIM:, :]))
    o_ref[...] = h_ref[...] + _rms(m, post_ref[...])


def _merge(h, ya, gb, uc, prev, yc, conv_w, ga, gconv, gc, w_o, post_g, tm):
    r, d = h.shape
    row = lambda w: pl.BlockSpec((tm, w), lambda i: (i, 0))
    full2 = lambda a: pl.BlockSpec(a.shape, lambda i: (0, 0))
    return pl.pallas_call(
        functools.partial(_merge_body, tm=tm),
        grid=(r // tm,),
        in_specs=[
            row(d), row(MLA_WIDTH), row(CONV_DIM), row(CONV_DIM),
            pl.BlockSpec((tm // SEG, CONV_W - 1, CONV_DIM), lambda i: (i, 0, 0)),
            row(SB_WIDTH), full2(conv_w), full2(ga), full2(gconv), full2(gc), full2(w_o), full2(post_g),
        ],
        out_specs=row(d),
        out_shape=jax.ShapeDtypeStruct((r, d), F32),
        scratch_shapes=[pltpu.VMEM((tm, CONV_DIM), BF16)],
        compiler_params=_params("parallel"),
        name="merge",
    )(h, ya, gb, uc, prev, yc, conv_w, ga, gconv, gc, w_o, post_g)


def _rot_half_cols(w):
    half = ROPE_DIM // 2
    return jnp.concatenate([-w[:, half:], w[:, :half]], axis=1)


def _pad_slot(w):
    return jnp.pad(w, ((0, 0), (0, ROPE_SLOT - w.shape[1])))


def _pack_w_in(w):
    o = np.cumsum([0, Q_LORA, KV_LORA, ROPE_DIM, CONV_DIM, CONV_DIM, CONV_DIM, SB_WIDTH, SB_WIDTH, SB_WIDTH])
    g = [w[:, o[i]:o[i + 1]] for i in range(9)]
    c_q, c_kv, k_rope = g[0], g[1], g[2]
    return jnp.concatenate([c_q, c_kv] + g[3:] + [_pad_slot(k_rope), _pad_slot(_rot_half_cols(k_rope))],
                           axis=1).astype(BF16)


def _pack_w_uq(w):
    w = w.reshape(Q_LORA, MLA_HEADS, NOPE_DIM + ROPE_DIM)
    nope = w[:, :, :NOPE_DIM].reshape(Q_LORA, -1)
    rope = [w[:, h, NOPE_DIM:] for h in range(MLA_HEADS)]
    return jnp.concatenate([nope] + [_pad_slot(r) for r in rope] + [_pad_slot(_rot_half_cols(r)) for r in rope],
                           axis=1).astype(BF16)


def _rope_tables(pos):
    half = ROPE_DIM // 2
    inv_freq = ROPE_THETA ** (-jnp.arange(half, dtype=F32) / half)
    ang = pos.astype(F32)[:, None] * inv_freq[None, :]
    pad = lambda t: jnp.pad(jnp.concatenate([t, t], axis=1), ((0, 0), (0, ROPE_SLOT - ROPE_DIM)))
    return pad(jnp.cos(ang)), pad(jnp.sin(ang))


def kernel(x_prompt, x_sample, cache_mla_ckv, cache_mla_krope, cache_sb_k, cache_sb_v, state_conv, ffn1_pre_g, ffn1_w_gate, ffn1_w_up, ffn1_w_down, ffn1_post_g, mix_pre_g, w_in, mla_q_norm_g, mla_kv_norm_g, mla_w_uq, mla_w_uk, mla_w_uv, conv_w, out_norm_mla_g, out_norm_conv_g, out_norm_sb_g, w_o, mix_post_g, ffn2_pre_g, ffn2_w_gate, ffn2_w_up, ffn2_w_down, ffn2_post_g):
    batch, seq, d = x_prompt.shape
    dec_batch, sq, _ = x_sample.shape
    depth, _, past, _ = cache_mla_ckv.shape
    rp, rs = batch * seq, dec_batch * sq
    assert d == D_MODEL and sq == SEG and seq % SEG == 0

    tm_ffn = _pick_tile(512, rp, rs)
    tm = _pick_tile(256, rp, rs)
    t_attn = _pick_tile(256, seq)

    pos = jnp.concatenate([jnp.tile(jnp.arange(seq, dtype=jnp.int32), batch),
                           jnp.tile(past + jnp.arange(sq, dtype=jnp.int32), dec_batch)])
    cos, sin = _rope_tables(pos)
    cache_k = cache_sb_k.reshape(depth, dec_batch, past * SB_HEADS, SB_DIM)
    cache_v = cache_sb_v.reshape(depth, dec_batch, past * SB_HEADS, SB_DIM)
    vec = lambda g: g.reshape(1, -1)

    def ffn(x, pre_g, w_gate, w_up, w_down, post_g):
        tf = _ffn_chunk(w_gate.shape[1])
        return _ffn(x, vec(pre_g), _chunk_major(w_gate, tf), _chunk_major(w_up, tf), w_down.astype(BF16),
                    vec(post_g), tm_ffn)

    h = jnp.concatenate([x_prompt.reshape(rp, d), x_sample.reshape(rs, d)], axis=0)
    new = []
    for l in range(depth):
        h = ffn(h, ffn1_pre_g[l], ffn1_w_gate[l], ffn1_w_up[l], ffn1_w_down[l], ffn1_post_g[l])

        w_ukt = jnp.transpose(mla_w_uk[l], (1, 2, 0)).astype(BF16)
        w_uv = jnp.transpose(mla_w_uv[l], (1, 0, 2)).astype(BF16)
        q, kv, ckv, kr, gb, uc, qsb, ksb, vsb, ksbh, vsbh = _proj(
            h, vec(mix_pre_g[l]), _pack_w_in(w_in[l]), vec(mla_q_norm_g[l]), vec(mla_kv_norm_g[l]),
            _pack_w_uq(mla_w_uq[l]), w_ukt, cos, sin, tm)

        ya = jnp.concatenate([
            _mla_prompt(q, kv, w_uv, batch, seq, t_attn, _pick_tile(2 * t_attn, seq)),
            _mla_sample(q, kv, cache_mla_ckv, cache_mla_krope, w_uv, l, rp, dec_batch, sq)], axis=0)
        yc = jnp.concatenate([
            _sb_prompt(qsb, ksbh, vsbh, batch, seq, t_attn),
            _sb_sample(qsb, ksbh, vsbh, cache_k, cache_v, l, rp, dec_batch, sq)], axis=0)

        tails = uc[:rp].reshape(batch, seq // SEG, SEG, CONV_DIM)[:, :, SEG - (CONV_W - 1):]
        prev_p = jnp.concatenate([jnp.zeros_like(tails[:, :1]), tails[:, :-1]], axis=1)
        prev = jnp.concatenate([prev_p.reshape(-1, CONV_W - 1, CONV_DIM), state_conv[l]], axis=0)
        new_conv_s = uc[rp:].reshape(dec_batch, sq, CONV_DIM)[:, sq - (CONV_W - 1):]

        h = _merge(h, ya, gb, uc, prev, yc, conv_w[l], vec(out_norm_mla_g[l]), vec(out_norm_conv_g[l]),
                   vec(out_norm_sb_g[l]), w_o[l].astype(BF16), vec(mix_post_g[l]), tm)
        h = ffn(h, ffn2_pre_g[l], ffn2_w_gate[l], ffn2_w_up[l], ffn2_w_down[l], ffn2_post_g[l])

        new.append((ckv, kr, ksb, vsb, tails[:, -1], new_conv_s))

    def stack(i, lo, hi, shape, rows_per_token=1):
        return jnp.stack([n[i][lo * rows_per_token:hi * rows_per_token].reshape(shape) for n in new], axis=0)

    p4, s4 = (batch, seq, SB_HEADS, SB_DIM), (dec_batch, sq, SB_HEADS, SB_DIM)
    return (h[:rp].reshape(batch, seq, d), h[rp:].reshape(dec_batch, sq, d),
            stack(0, 0, rp, (batch, seq, KV_LORA)), stack(1, 0, rp, (batch, seq, ROPE_DIM)),
            stack(2, 0, rp, p4, SB_HEADS), stack(3, 0, rp, p4, SB_HEADS), jnp.stack([n[4] for n in new], axis=0),
            stack(0, rp, rp + rs, (dec_batch, sq, KV_LORA)), stack(1, rp, rp + rs, (dec_batch, sq, ROPE_DIM)),
            stack(2, rp, rp + rs, s4, SB_HEADS), stack(3, rp, rp + rs, s4, SB_HEADS),
            jnp.stack([n[5] for n in new], axis=0))
```

```python
import functools

import jax
import jax.numpy as jnp
from jax import lax
from jax.experimental import pallas as pl
from jax.experimental.pallas import tpu as pltpu

F32 = jnp.float32
BF16 = jnp.bfloat16

D_MODEL = 2048
CHUNK = 64
EPS = 1e-6
MLA_HEADS = 8
Q_LORA = 512
KV_LORA = 512
NOPE_DIM = 128
ROPE_DIM = 64
V_DIM = 128
ROPE_THETA = 10000.0
MLA_WIDTH = MLA_HEADS * V_DIM
CONV_DIM = 512
CONV_W = 3
SB_HEADS = 4
SB_DIM = 128
SB_WIDTH = SB_HEADS * SB_DIM
_SB_HEAD_COLS = [slice(h * SB_DIM, (h + 1) * SB_DIM) for h in range(SB_HEADS)]
MLA_SCALE = (NOPE_DIM + ROPE_DIM) ** -0.5
SB_SCALE = SB_DIM ** -0.5
NEG_INF = -1e30
_CHUNK_SHIFT = CHUNK.bit_length() - 1
assert 1 << _CHUNK_SHIFT == CHUNK

LANES = 128
ROPE_SLOT = LANES
KV_W = KV_LORA + ROPE_SLOT
SEG = 64
VMEM_LIMIT = 56 * 1024 * 1024

_OFF_CQ, _OFF_CKV, _OFF_GB, _OFF_GC, _OFF_XC, _OFF_QSB, _OFF_KSB, _OFF_VSB = (512 * i for i in range(8))
_OFF_KR = 4096
_OFF_KROT = 4096 + ROPE_SLOT
W_IN_PACKED = 4096 + 2 * ROPE_SLOT
_UQ_ROPE = MLA_HEADS * NOPE_DIM
_UQ_ROT = _UQ_ROPE + MLA_HEADS * ROPE_SLOT
W_UQ_PACKED = _UQ_ROT + MLA_HEADS * ROPE_SLOT


def _params(*sem):
    return pltpu.CompilerParams(dimension_semantics=sem, vmem_limit_bytes=VMEM_LIMIT)


def _layer_slab(a, layer, n_grid, single_buffer=True):
    zeros = (0,) * (a.ndim - 1)
    index_map = (lambda i: (layer,) + zeros) if n_grid == 1 else (lambda i, j: (layer,) + zeros)
    return pl.BlockSpec((None,) + a.shape[1:], index_map, pipeline_mode=pl.Buffered(1) if single_buffer else None)


def _rms(x, g):
    return x * lax.rsqrt(jnp.mean(x * x, axis=-1, keepdims=True) + EPS) * g


def _dot(a, b):
    return jnp.dot(a, b, preferred_element_type=F32)


def _dot_nt(a, b):
    return lax.dot_general(a, b, (((1,), (1,)), ((), ())), preferred_element_type=F32)


def _lane_tile(x, n):
    return jnp.concatenate([x] * n, axis=1)


def _pick_tile(cap, *sizes):
    t = cap
    while any(s % t for s in sizes):
        t //= 2
    return t


def _ffn_body(x_ref, pre_ref, wg_ref, wu_ref, wd_ref, post_ref, o_ref, xn_ref):
    j = pl.program_id(1)

    @pl.when(j == 0)
    def _():
        xn_ref[...] = _rms(x_ref[...], pre_ref[...]).astype(BF16)
        o_ref[...] = jnp.zeros_like(o_ref)

    xn = xn_ref[...]
    a = jax.nn.silu(_dot(xn, wg_ref[...].astype(BF16))) * _dot(xn, wu_ref[...].astype(BF16))
    o_ref[...] += _dot(a.astype(BF16), wd_ref[...].astype(BF16))

    @pl.when(j == pl.num_programs(1) - 1)
    def _():
        o_ref[...] = x_ref[...] + 0.5 * _rms(o_ref[...], post_ref[...])


def _ffn(x, pre_g, w_gate, w_up, w_down, post_g, layer, tm, tf):
    r, d = x.shape
    dff = w_gate.shape[2]
    return pl.pallas_call(
        _ffn_body,
        grid=(r // tm, dff // tf),
        in_specs=[
            pl.BlockSpec((tm, d), lambda i, j: (i, 0), pipeline_mode=pl.Buffered(1)),
            _layer_slab(pre_g, layer, 2),
            pl.BlockSpec((None, d, tf), lambda i, j: (layer, 0, j)),
            pl.BlockSpec((None, d, tf), lambda i, j: (layer, 0, j)),
            pl.BlockSpec((None, tf, d), lambda i, j: (layer, j, 0)),
            _layer_slab(post_g, layer, 2),
        ],
        out_specs=pl.BlockSpec((tm, d), lambda i, j: (i, 0)),
        out_shape=jax.ShapeDtypeStruct((r, d), F32),
        scratch_shapes=[pltpu.VMEM((tm, d), BF16)],
        compiler_params=_params("parallel", "arbitrary"),
        name="ffn",
    )(x, pre_g, w_gate, w_up, w_down, post_g)


def _proj_body(h_ref, pre_ref, win_ref, qg_ref, kvg_ref, wuq_ref, wuk_ref, cos_ref, sin_ref,
               q_ref, kv_ref, ckv_ref, kr_ref, gb_ref, uc_ref, qsb_ref, ksb_ref, vsb_ref,
               ksbh_ref, vsbh_ref):
    xn = _rms(h_ref[...], pre_ref[...]).astype(BF16)
    cos = cos_ref[...]
    sin = sin_ref[...]

    def col(off, width):
        return _dot(xn, win_ref[:, off:off + width])

    c_kv = _rms(col(_OFF_CKV, KV_LORA), kvg_ref[...])
    ckv_ref[...] = c_kv
    kv_ref[:, :KV_LORA] = c_kv.astype(BF16)
    k_rope = col(_OFF_KR, ROPE_SLOT) * cos + col(_OFF_KROT, ROPE_SLOT) * sin
    kv_ref[:, KV_LORA:] = k_rope.astype(BF16)
    kr_ref[...] = k_rope[:, :ROPE_DIM]

    gb_ref[...] = col(_OFF_GB, CONV_DIM)
    uc_ref[...] = col(_OFF_GC, CONV_DIM) * col(_OFF_XC, CONV_DIM)

    qsb_ref[...] = col(_OFF_QSB, SB_WIDTH).astype(BF16)
    k_sb = col(_OFF_KSB, SB_WIDTH)
    v_sb = col(_OFF_VSB, SB_WIDTH)
    tm = k_sb.shape[0]
    for h, hs in enumerate(_SB_HEAD_COLS):
        ksb_ref[pl.ds(h, tm, stride=SB_HEADS), :] = k_sb[:, hs]
        vsb_ref[pl.ds(h, tm, stride=SB_HEADS), :] = v_sb[:, hs]
    ksbh_ref[...] = k_sb.astype(BF16)
    vsbh_ref[...] = v_sb.astype(BF16)

    c_q = _rms(col(_OFF_CQ, Q_LORA), qg_ref[...]).astype(BF16)
    q_nope = _dot(c_q, wuq_ref[:, :_UQ_ROPE])
    q_rope = _dot(c_q, wuq_ref[:, _UQ_ROPE:_UQ_ROT])
    q_rot = _dot(c_q, wuq_ref[:, _UQ_ROT:])
    for h in range(MLA_HEADS):
        nope = q_nope[:, h * NOPE_DIM:(h + 1) * NOPE_DIM].astype(BF16)
        q_ref[h, :, :KV_LORA] = _dot(nope, wuk_ref[h]).astype(BF16)
        sl = slice(h * ROPE_SLOT, (h + 1) * ROPE_SLOT)
        q_ref[h, :, KV_LORA:] = (q_rope[:, sl] * cos + q_rot[:, sl] * sin).astype(BF16)


def _proj(h, pre_g, w_in, qg, kvg, w_uq, w_ukt, cos, sin, layer, tm):
    r, d = h.shape
    row = lambda w: pl.BlockSpec((tm, w), lambda i: (i, 0))
    heads = lambda: pl.BlockSpec((tm * SB_HEADS, SB_DIM), lambda i: (i, 0))
    slab = lambda a: _layer_slab(a, layer, 1)
    sds = jax.ShapeDtypeStruct
    return pl.pallas_call(
        _proj_body,
        grid=(r // tm,),
        in_specs=[row(d), slab(pre_g), slab(w_in), slab(qg), slab(kvg), slab(w_uq), slab(w_ukt),
                  row(ROPE_SLOT), row(ROPE_SLOT)],
        out_specs=[
            pl.BlockSpec((MLA_HEADS, tm, KV_W), lambda i: (0, i, 0)),
            row(KV_W), row(KV_LORA), row(ROPE_DIM), row(CONV_DIM), row(CONV_DIM),
            row(SB_WIDTH), heads(), heads(), row(SB_WIDTH), row(SB_WIDTH),
        ],
        out_shape=[
            sds((MLA_HEADS, r, KV_W), BF16), sds((r, KV_W), BF16), sds((r, KV_LORA), F32),
            sds((r, ROPE_DIM), F32), sds((r, CONV_DIM), F32), sds((r, CONV_DIM), F32),
            sds((r, SB_WIDTH), BF16), sds((r * SB_HEADS, SB_DIM), F32), sds((r * SB_HEADS, SB_DIM), F32),
            sds((r, SB_WIDTH), BF16), sds((r, SB_WIDTH), BF16),
        ],
        compiler_params=_params("parallel"),
        name="proj",
    )(h, pre_g, w_in, qg, kvg, w_uq, w_ukt, cos, sin)


def _mla_up(o, wuv_ref, o_ref, tq):
    for h in range(MLA_HEADS):
        oh = o[h * tq:(h + 1) * tq].astype(BF16)
        o_ref[:, h * V_DIM:(h + 1) * V_DIM] = _dot(oh, wuv_ref[h])


def _mla_prompt_body(q_ref, kv_ref, wuv_ref, o_ref, m_ref, l_ref, acc_ref, *, tq, tk):
    qi = pl.program_id(1)
    m_ref[...] = jnp.full_like(m_ref, NEG_INF)
    l_ref[...] = jnp.zeros_like(l_ref)
    acc_ref[...] = jnp.zeros_like(acc_ref)

    def step(kt, masked):
        kv = kv_ref[pl.ds(pl.multiple_of(kt * tk, tk), tk), :]
        if masked:
            qpos = qi * tq + lax.broadcasted_iota(jnp.int32, (tq, 1), 0)
            kpos = kt * tk + lax.broadcasted_iota(jnp.int32, (1, tk), 1)
            visible = (kpos >> _CHUNK_SHIFT) <= (qpos >> _CHUNK_SHIFT)

        def scores(h):
            s = _dot_nt(q_ref[h], kv) * MLA_SCALE
            return jnp.where(visible, s, NEG_INF) if masked else s

        s_next = scores(0)
        for h in range(MLA_HEADS):
            s = s_next
            if h + 1 < MLA_HEADS:
                s_next = scores(h + 1)
            rs = slice(h * tq, (h + 1) * tq)
            m_old = m_ref[rs]
            m_new = jnp.maximum(m_old, jnp.max(s, axis=-1, keepdims=True))
            alpha = jnp.exp(m_old - m_new)
            p = jnp.exp(s - _lane_tile(m_new, tk // LANES))
            l_ref[rs] = alpha * l_ref[rs] + sum(p[:, c * LANES:(c + 1) * LANES] for c in range(tk // LANES))
            acc_ref[rs] = (_lane_tile(alpha, KV_LORA // LANES) * acc_ref[rs]
                           + _dot(p.astype(BF16), kv[:, :KV_LORA]))
            m_ref[rs] = m_new

    n_full = (qi * tq) // tk
    lax.fori_loop(0, n_full, lambda kt, c: (step(kt, False), c)[1], 0)
    for d in range(-(-tq // tk)):
        step(n_full + d, True)

    l = jnp.sum(l_ref[...], axis=-1, keepdims=True)
    _mla_up(acc_ref[...] / l, wuv_ref, o_ref, tq)


def _mla_prompt(q, kv, w_uv, layer, batch, seq, tq, tk):
    nq = seq // tq
    rows = MLA_HEADS * tq
    return pl.pallas_call(
        functools.partial(_mla_prompt_body, tq=tq, tk=tk),
        grid=(batch, nq),
        in_specs=[
            pl.BlockSpec((MLA_HEADS, tq, KV_W), lambda b, i: (0, b * nq + i, 0)),
            pl.BlockSpec((seq, KV_W), lambda b, i: (b, 0)),
            _layer_slab(w_uv, layer, 2),
        ],
        out_specs=pl.BlockSpec((tq, MLA_WIDTH), lambda b, i: (b * nq + i, 0)),
        out_shape=jax.ShapeDtypeStruct((batch * seq, MLA_WIDTH), F32),
        scratch_shapes=[pltpu.VMEM((rows, LANES), F32), pltpu.VMEM((rows, LANES), F32),
                        pltpu.VMEM((rows, KV_LORA), F32)],
        compiler_params=_params("parallel", "arbitrary"),
        name="mla_prompt",
    )(q, kv, w_uv)


def _mla_sample_body(q_ref, kvn_ref, cc_ref, ckr_ref, wuv_ref, o_ref, *, sq):
    rows = MLA_HEADS * sq
    q = q_ref[...].reshape(rows, KV_W)
    kvn = kvn_ref[...]
    cc = cc_ref[...].astype(BF16)
    ckr = ckr_ref[...].astype(BF16)
    s_c = (_dot_nt(q[:, :KV_LORA], cc) + _dot_nt(q[:, KV_LORA:KV_LORA + ROPE_DIM], ckr)) * MLA_SCALE
    s_n = _dot_nt(q, kvn) * MLA_SCALE
    m = jnp.maximum(jnp.max(s_c, axis=-1, keepdims=True), jnp.max(s_n, axis=-1, keepdims=True))
    p_c = jnp.exp(s_c - m)
    p_n = jnp.exp(s_n - m)
    l = jnp.sum(p_c, axis=-1, keepdims=True) + jnp.sum(p_n, axis=-1, keepdims=True)
    o = _dot(p_c.astype(BF16), cc) + _dot(p_n.astype(BF16), kvn[:, :KV_LORA])
    _mla_up(o / l, wuv_ref, o_ref, sq)


def _mla_sample(q, kv, cache_ckv, cache_kr, w_uv, layer, row0, dec_batch, sq):
    past = cache_ckv.shape[2]
    assert past % CHUNK == 0 and sq <= CHUNK
    b0 = row0 // sq
    return pl.pallas_call(
        functools.partial(_mla_sample_body, sq=sq),
        grid=(dec_batch,),
        in_specs=[
            pl.BlockSpec((MLA_HEADS, sq, KV_W), lambda b: (0, b0 + b, 0)),
            pl.BlockSpec((sq, KV_W), lambda b: (b0 + b, 0)),
            pl.BlockSpec((None, None, past, KV_LORA), lambda b: (layer, b, 0, 0)),
            pl.BlockSpec((None, None, past, ROPE_DIM), lambda b: (layer, b, 0, 0)),
            _layer_slab(w_uv, layer, 1),
        ],
        out_specs=pl.BlockSpec((sq, MLA_WIDTH), lambda b: (b, 0)),
        out_shape=jax.ShapeDtypeStruct((dec_batch * sq, MLA_WIDTH), F32),
        compiler_params=_params("parallel"),
        name="mla_sample",
    )(q, kv, cache_ckv, cache_kr, w_uv)


def _suffix_ones(n):
    j = lax.broadcasted_iota(jnp.int32, (n, n), 0)
    k = lax.broadcasted_iota(jnp.int32, (n, n), 1)
    return jnp.where(j > k, 1.0, 0.0).astype(BF16)


def _sb_tiles(qs, ks, vs, carries, ones, causal):
    rows = qs[0].shape[0]
    logits = [_dot_nt(q, k) * SB_SCALE for q, k in zip(qs, ks)]
    softplus, log_keep, split = [], [], []
    for x in logits:
        sp = jnp.maximum(x, 0.0) + jnp.log(1.0 + jnp.exp(-jnp.abs(x)))
        lk = -sp if causal is None else jnp.where(causal, -sp, 0.0)
        hi = lk.astype(BF16)
        lo = (lk - hi.astype(F32)).astype(BF16)
        softplus.append(sp)
        log_keep.append(lk)
        split.append(jnp.concatenate([hi, lo], axis=0))
    sums = [_dot(x, ones) for x in split]
    weights = []
    for x, sp, sm, carry in zip(logits, softplus, sums, carries):
        between = sm[:rows] + sm[rows:] + carry
        w = jnp.exp((x - sp) + between)
        weights.append(w if causal is None else jnp.where(causal, w, 0.0))
    outs = [_dot(w.astype(BF16), v) for w, v in zip(weights, vs)]
    return outs, [jnp.sum(lk, axis=-1, keepdims=True) for lk in log_keep]


def _sb_prompt_body(q_ref, k_ref, v_ref, o_ref, carry_ref, *, t):
    qi = pl.program_id(1)
    ones = _suffix_ones(t)
    causal = lax.broadcasted_iota(jnp.int32, (t, t), 1) < lax.broadcasted_iota(jnp.int32, (t, t), 0)
    o_ref[...] = jnp.zeros_like(o_ref)
    carry_ref[...] = jnp.zeros_like(carry_ref)

    def tile(kt, mask):
        rows = pl.ds(pl.multiple_of(kt * t, t), t)
        outs, tots = _sb_tiles([q_ref[:, hs] for hs in _SB_HEAD_COLS], [k_ref[rows, hs] for hs in _SB_HEAD_COLS],
                               [v_ref[rows, hs] for hs in _SB_HEAD_COLS],
                               [_lane_tile(carry_ref[:, hs], t // LANES) for hs in _SB_HEAD_COLS],
                               ones, mask)
        for hs, out, tot in zip(_SB_HEAD_COLS, outs, tots):
            o_ref[:, hs] += out
            carry_ref[:, hs] += tot

    tile(qi, causal)
    lax.fori_loop(0, qi, lambda i, c: (tile(qi - 1 - i, None), c)[1], 0)


def _sb_prompt(q, k, v, batch, seq, t):
    nq = seq // t
    return pl.pallas_call(
        functools.partial(_sb_prompt_body, t=t),
        grid=(batch, nq),
        in_specs=[
            pl.BlockSpec((t, SB_WIDTH), lambda b, i: (b * nq + i, 0)),
            pl.BlockSpec((seq, SB_WIDTH), lambda b, i: (b, 0)),
            pl.BlockSpec((seq, SB_WIDTH), lambda b, i: (b, 0)),
        ],
        out_specs=pl.BlockSpec((t, SB_WIDTH), lambda b, i: (b * nq + i, 0)),
        out_shape=jax.ShapeDtypeStruct((batch * seq, SB_WIDTH), F32),
        scratch_shapes=[pltpu.VMEM((t, SB_HEADS * LANES), F32)],
        compiler_params=_params("parallel", "arbitrary"),
        name="sb_prompt",
    )(q, k, v)


def _sb_sample_body(q_ref, kn_ref, vn_ref, ck_ref, cv_ref, o_ref, *, sq, past, tk):
    ones_new = _suffix_ones(sq)
    ones = _suffix_ones(tk)
    causal = lax.broadcasted_iota(jnp.int32, (sq, sq), 1) < lax.broadcasted_iota(jnp.int32, (sq, sq), 0)
    qs = [q_ref[:, hs] for hs in _SB_HEAD_COLS]
    accs, carries = _sb_tiles(qs, [kn_ref[:, hs] for hs in _SB_HEAD_COLS], [vn_ref[:, hs] for hs in _SB_HEAD_COLS],
                              [0.0] * SB_HEADS, ones_new, causal)
    for kt in reversed(range(past // tk)):
        rows = [pl.ds(kt * tk * SB_HEADS + h, tk, stride=SB_HEADS) for h in range(SB_HEADS)]
        outs, tots = _sb_tiles(qs, [ck_ref[r, :].astype(BF16) for r in rows],
                               [cv_ref[r, :].astype(BF16) for r in rows], carries, ones, None)
        accs = [a + o for a, o in zip(accs, outs)]
        carries = [c + t for c, t in zip(carries, tots)]
    for hs, acc in zip(_SB_HEAD_COLS, accs):
        o_ref[:, hs] = acc


def _sb_sample(q, k, v, cache_k, cache_v, layer, row0, dec_batch, sq):
    past = cache_k.shape[2] // SB_HEADS
    b0 = row0 // sq
    new = lambda: pl.BlockSpec((sq, SB_WIDTH), lambda b: (b0 + b, 0))
    old = lambda: pl.BlockSpec((None, None, past * SB_HEADS, SB_DIM), lambda b: (layer, b, 0, 0))
    return pl.pallas_call(
        functools.partial(_sb_sample_body, sq=sq, past=past, tk=_pick_tile(256, past)),
        grid=(dec_batch,),
        in_specs=[new(), new(), new(), old(), old()],
        out_specs=pl.BlockSpec((sq, SB_WIDTH), lambda b: (b, 0)),
        out_shape=jax.ShapeDtypeStruct((dec_batch * sq, SB_WIDTH), F32),
        compiler_params=_params("parallel"),
        name="sb_sample",
    )(q, k, v, cache_k, cache_v)


def _merge_body(h_ref, yap_ref, yas_ref, gb_ref, uc_ref, prev_ref, ycp_ref, ycs_ref, cw_ref, ga_ref, gconv_ref,
                gc_ref, wo_ref, post_ref, o_ref, yb_ref, *, tm, n_prompt):
    is_prompt = pl.program_id(0) < n_prompt
    cw = cw_ref[...]
    row = lax.broadcasted_iota(jnp.int32, (SEG, 1), 0)
    for s in range(tm // SEG):
        rows = slice(s * SEG, (s + 1) * SEG)
        u = uc_ref[rows, :]
        prev = prev_ref[s]
        u1 = jnp.where(row == 0, prev[1:2], pltpu.roll(u, 1, 0))
        u2 = jnp.where(row == 0, prev[0:1], jnp.where(row == 1, prev[1:2], pltpu.roll(u, 2, 0)))
        conv = cw[0:1] * u2 + cw[1:2] * u1 + cw[2:3] * u
        yb_ref[rows, :] = _rms(gb_ref[rows, :] * conv, gconv_ref[...]).astype(BF16)
    ya = _rms(jnp.where(is_prompt, yap_ref[...], yas_ref[...]), ga_ref[...]).astype(BF16)
    yc = _rms(jnp.where(is_prompt, ycp_ref[...], ycs_ref[...]), gc_ref[...]).astype(BF16)
    m = (_dot(ya, wo_ref[:MLA_WIDTH, :]) + _dot(yb_ref[...], wo_ref[MLA_WIDTH:MLA_WIDTH + CONV_DIM, :])
         + _dot(yc, wo_ref[MLA_WIDTH + CONV_DIM:, :]))
    o_ref[...] = h_ref[...] + _rms(m, post_ref[...])


def _merge(h, ya_p, ya_s, gb, uc, prev, yc_p, yc_s, conv_w, ga, gconv, gc, w_o, post_g, layer, tm):
    r, d = h.shape
    n_prompt = ya_p.shape[0] // tm
    row = lambda w: pl.BlockSpec((tm, w), lambda i: (i, 0))
    prompt = lambda w: pl.BlockSpec((tm, w), lambda i: (jnp.minimum(i, n_prompt - 1), 0))
    sample = lambda w: pl.BlockSpec((tm, w), lambda i: (jnp.maximum(i - n_prompt, 0), 0))
    slab = lambda a: _layer_slab(a, layer, 1)
    return pl.pallas_call(
        functools.partial(_merge_body, tm=tm, n_prompt=n_prompt),
        grid=(r // tm,),
        in_specs=[
            row(d), prompt(MLA_WIDTH), sample(MLA_WIDTH), row(CONV_DIM), row(CONV_DIM),
            pl.BlockSpec((tm // SEG, CONV_W - 1, CONV_DIM), lambda i: (i, 0, 0)),
            prompt(SB_WIDTH), sample(SB_WIDTH), slab(conv_w), slab(ga), slab(gconv), slab(gc), slab(w_o), slab(post_g),
        ],
        out_specs=row(d),
        out_shape=jax.ShapeDtypeStruct((r, d), F32),
        scratch_shapes=[pltpu.VMEM((tm, CONV_DIM), BF16)],
        compiler_params=_params("parallel"),
        name="merge",
    )(h, ya_p, ya_s, gb, uc, prev, yc_p, yc_s, conv_w, ga, gconv, gc, w_o, post_g)


def _rot_half(w):
    half = ROPE_DIM // 2
    return jnp.concatenate([-w[..., half:], w[..., :half]], axis=-1)


def _pad_slot(w):
    return jnp.pad(w, [(0, 0)] * (w.ndim - 1) + [(0, ROPE_SLOT - w.shape[-1])])


def _pack_w_in(w):
    kr0 = Q_LORA + KV_LORA
    k_rope = w[..., kr0:kr0 + ROPE_DIM]
    return jnp.concatenate([w[..., :kr0], w[..., kr0 + ROPE_DIM:], _pad_slot(k_rope), _pad_slot(_rot_half(k_rope))],
                           axis=-1).astype(BF16)


def _pack_w_uq(w):
    depth = w.shape[0]
    w = w.reshape(depth, Q_LORA, MLA_HEADS, NOPE_DIM + ROPE_DIM)
    flat = lambda a: a.reshape(depth, Q_LORA, -1)
    rope = w[..., NOPE_DIM:]
    return jnp.concatenate([flat(w[..., :NOPE_DIM]), flat(_pad_slot(rope)), flat(_pad_slot(_rot_half(rope)))],
                           axis=-1).astype(BF16)


def _rope_tables(pos):
    half = ROPE_DIM // 2
    inv_freq = ROPE_THETA ** (-jnp.arange(half, dtype=F32) / half)
    ang = pos.astype(F32)[:, None] * inv_freq[None, :]
    pad = lambda t: jnp.pad(jnp.concatenate([t, t], axis=1), ((0, 0), (0, ROPE_SLOT - ROPE_DIM)))
    return pad(jnp.cos(ang)), pad(jnp.sin(ang))


def kernel(x_prompt, x_sample, cache_mla_ckv, cache_mla_krope, cache_sb_k, cache_sb_v, state_conv, ffn1_pre_g, ffn1_w_gate, ffn1_w_up, ffn1_w_down, ffn1_post_g, mix_pre_g, w_in, mla_q_norm_g, mla_kv_norm_g, mla_w_uq, mla_w_uk, mla_w_uv, conv_w, out_norm_mla_g, out_norm_conv_g, out_norm_sb_g, w_o, mix_post_g, ffn2_pre_g, ffn2_w_gate, ffn2_w_up, ffn2_w_down, ffn2_post_g):
    batch, seq, d = x_prompt.shape
    dec_batch, sq, _ = x_sample.shape
    depth, _, past, _ = cache_mla_ckv.shape
    rp, rs = batch * seq, dec_batch * sq
    assert d == D_MODEL and sq == SEG and seq % SEG == 0

    tm_ffn = _pick_tile(1024, rp + rs)
    tf_ffn = _pick_tile(256, ffn1_w_gate.shape[2])
    tm = _pick_tile(256, rp, rs)
    t_attn = _pick_tile(256, seq)

    pos = jnp.concatenate([jnp.tile(jnp.arange(seq, dtype=jnp.int32), batch),
                           jnp.tile(past + jnp.arange(sq, dtype=jnp.int32), dec_batch)])
    cos, sin = _rope_tables(pos)
    cache_k = cache_sb_k.reshape(depth, dec_batch, past * SB_HEADS, SB_DIM)
    cache_v = cache_sb_v.reshape(depth, dec_batch, past * SB_HEADS, SB_DIM)
    gain = lambda g: g.reshape(depth, 1, -1)

    w_in_p = _pack_w_in(w_in)
    w_uq_p = _pack_w_uq(mla_w_uq)
    w_ukt = jnp.transpose(mla_w_uk, (0, 2, 3, 1)).astype(BF16)
    w_uv = jnp.transpose(mla_w_uv, (0, 2, 1, 3)).astype(BF16)
    w_o_h = w_o.astype(BF16)

    h = jnp.concatenate([x_prompt.reshape(rp, d), x_sample.reshape(rs, d)], axis=0)
    new = []
    for l in range(depth):
        h = _ffn(h, gain(ffn1_pre_g), ffn1_w_gate, ffn1_w_up, ffn1_w_down, gain(ffn1_post_g), l, tm_ffn, tf_ffn)

        q, kv, ckv, kr, gb, uc, qsb, ksb, vsb, ksbh, vsbh = _proj(
            h, gain(mix_pre_g), w_in_p, gain(mla_q_norm_g), gain(mla_kv_norm_g), w_uq_p, w_ukt, cos, sin, l, tm)

        ya_p = _mla_prompt(q, kv, w_uv, l, batch, seq, t_attn, _pick_tile(2 * t_attn, seq))
        ya_s = _mla_sample(q, kv, cache_mla_ckv, cache_mla_krope, w_uv, l, rp, dec_batch, sq)
        yc_p = _sb_prompt(qsb, ksbh, vsbh, batch, seq, t_attn)
        yc_s = _sb_sample(qsb, ksbh, vsbh, cache_k, cache_v, l, rp, dec_batch, sq)

        tails = uc[:rp].reshape(batch, seq // SEG, SEG, CONV_DIM)[:, :, SEG - (CONV_W - 1):]
        prev_p = jnp.concatenate([jnp.zeros_like(tails[:, :1]), tails[:, :-1]], axis=1)
        prev = jnp.concatenate([prev_p.reshape(-1, CONV_W - 1, CONV_DIM), state_conv[l]], axis=0)
        new_conv_s = uc[rp:].reshape(dec_batch, sq, CONV_DIM)[:, sq - (CONV_W - 1):]

        h = _merge(h, ya_p, ya_s, gb, uc, prev, yc_p, yc_s, conv_w, gain(out_norm_mla_g), gain(out_norm_conv_g),
                   gain(out_norm_sb_g), w_o_h, gain(mix_post_g), l, tm)
        h = _ffn(h, gain(ffn2_pre_g), ffn2_w_gate, ffn2_w_up, ffn2_w_down, gain(ffn2_post_g), l, tm_ffn, tf_ffn)

        new.append((ckv, kr, ksb, vsb, tails[:, -1], new_conv_s))

    def stack(i, lo, hi, shape, rows_per_token=1):
        return jnp.stack([n[i][lo * rows_per_token:hi * rows_per_token].reshape(shape) for n in new], axis=0)

    p4, s4 = (batch, seq, SB_HEADS, SB_DIM), (dec_batch, sq, SB_HEADS, SB_DIM)
    return (h[:rp].reshape(batch, seq, d), h[rp:].reshape(dec_batch, sq, d),
            stack(0, 0, rp, (batch, seq, KV_LORA)), stack(1, 0, rp, (batch, seq, ROPE_DIM)),
            stack(2, 0, rp, p4, SB_HEADS), stack(3, 0, rp, p4, SB_HEADS), jnp.stack([n[4] for n in new], axis=0),
            stack(0, rp, rp + rs, (dec_batch, sq, KV_LORA)), stack(1, rp, rp + rs, (dec_batch, sq, ROPE_DIM)),
            stack(2, rp, rp + rs, s4, SB_HEADS), stack(3, rp, rp + rs, s4, SB_HEADS),
            jnp.stack([n[5] for n in new], axis=0))
```

```python
import functools

import jax
import jax.numpy as jnp
from jax import lax
from jax.experimental import pallas as pl
from jax.experimental.pallas import tpu as pltpu

F32 = jnp.float32
BF16 = jnp.bfloat16

D_MODEL = 2048
CHUNK = 64
EPS = 1e-6
MLA_HEADS = 8
Q_LORA = 512
KV_LORA = 512
NOPE_DIM = 128
ROPE_DIM = 64
V_DIM = 128
ROPE_THETA = 10000.0
MLA_WIDTH = MLA_HEADS * V_DIM
CONV_DIM = 512
CONV_W = 3
SB_HEADS = 4
SB_DIM = 128
SB_WIDTH = SB_HEADS * SB_DIM
_SB_HEAD_COLS = [slice(h * SB_DIM, (h + 1) * SB_DIM) for h in range(SB_HEADS)]
MLA_SCALE = (NOPE_DIM + ROPE_DIM) ** -0.5
SB_SCALE = SB_DIM ** -0.5
NEG_INF = -1e30
_CHUNK_SHIFT = CHUNK.bit_length() - 1
assert 1 << _CHUNK_SHIFT == CHUNK

LANES = 128
ROPE_SLOT = LANES
KV_W = KV_LORA + ROPE_SLOT
SEG = 64
VMEM_LIMIT = 56 * 1024 * 1024

_OFF_CQ, _OFF_CKV, _OFF_GB, _OFF_GC, _OFF_XC, _OFF_QSB, _OFF_KSB, _OFF_VSB = (512 * i for i in range(8))
_OFF_KR = 4096
_OFF_KROT = 4096 + ROPE_SLOT
W_IN_PACKED = 4096 + 2 * ROPE_SLOT
_UQ_ROPE = MLA_HEADS * NOPE_DIM
_UQ_ROT = _UQ_ROPE + MLA_HEADS * ROPE_SLOT
W_UQ_PACKED = _UQ_ROT + MLA_HEADS * ROPE_SLOT


def _params(*sem):
    return pltpu.CompilerParams(dimension_semantics=sem, vmem_limit_bytes=VMEM_LIMIT)


def _layer_slab(a, layer, n_grid, single_buffer=True):
    zeros = (0,) * (a.ndim - 1)
    index_map = (lambda i: (layer,) + zeros) if n_grid == 1 else (lambda i, j: (layer,) + zeros)
    return pl.BlockSpec((None,) + a.shape[1:], index_map, pipeline_mode=pl.Buffered(1) if single_buffer else None)


def _rms(x, g):
    return x * lax.rsqrt(jnp.mean(x * x, axis=-1, keepdims=True) + EPS) * g


def _dot(a, b):
    return jnp.dot(a, b, preferred_element_type=F32)


def _dot_nt(a, b):
    return lax.dot_general(a, b, (((1,), (1,)), ((), ())), preferred_element_type=F32)


def _lane_tile(x, n):
    return jnp.concatenate([x] * n, axis=1)


def _pick_tile(cap, *sizes):
    t = cap
    while any(s % t for s in sizes):
        t //= 2
    return t


def _ffn_body(x_ref, pre_ref, wg_ref, wu_ref, wd_ref, post_ref, o_ref, xn_ref):
    j = pl.program_id(1)

    @pl.when(j == 0)
    def _():
        xn_ref[...] = _rms(x_ref[...], pre_ref[...]).astype(BF16)
        o_ref[...] = jnp.zeros_like(o_ref)

    xn = xn_ref[...]
    a = jax.nn.silu(_dot(xn, wg_ref[...].astype(BF16))) * _dot(xn, wu_ref[...].astype(BF16))
    o_ref[...] += _dot(a.astype(BF16), wd_ref[...].astype(BF16))

    @pl.when(j == pl.num_programs(1) - 1)
    def _():
        o_ref[...] = x_ref[...] + 0.5 * _rms(o_ref[...], post_ref[...])


def _ffn(x, pre_g, w_gate, w_up, w_down, post_g, layer, tm, tf):
    r, d = x.shape
    dff = w_gate.shape[2]
    return pl.pallas_call(
        _ffn_body,
        grid=(r // tm, dff // tf),
        in_specs=[
            pl.BlockSpec((tm, d), lambda i, j: (i, 0), pipeline_mode=pl.Buffered(1)),
            _layer_slab(pre_g, layer, 2),
            pl.BlockSpec((None, d, tf), lambda i, j: (layer, 0, j)),
            pl.BlockSpec((None, d, tf), lambda i, j: (layer, 0, j)),
            pl.BlockSpec((None, tf, d), lambda i, j: (layer, j, 0)),
            _layer_slab(post_g, layer, 2),
        ],
        out_specs=pl.BlockSpec((tm, d), lambda i, j: (i, 0)),
        out_shape=jax.ShapeDtypeStruct((r, d), F32),
        scratch_shapes=[pltpu.VMEM((tm, d), BF16)],
        compiler_params=_params("parallel", "arbitrary"),
        name="ffn",
    )(x, pre_g, w_gate, w_up, w_down, post_g)


def _proj_body(h_ref, pre_ref, win_ref, qg_ref, kvg_ref, wuq_ref, wukt_ref, wuk_ref, wuv_ref, cos_ref, sin_ref,
               qp_ref, kp_ref, vp_ref, q_ref, kv_ref, ckv_ref, kr_ref, gb_ref, uc_ref, qsb_ref, ksb_ref, vsb_ref,
               ksbh_ref, vsbh_ref, *, n_prompt):
    is_prompt = pl.program_id(0) < n_prompt
    xn = _rms(h_ref[...], pre_ref[...]).astype(BF16)
    cos = cos_ref[...]
    sin = sin_ref[...]

    def col(off, width):
        return _dot(xn, win_ref[:, off:off + width])

    c_kv = _rms(col(_OFF_CKV, KV_LORA), kvg_ref[...])
    ckv_ref[...] = c_kv
    c_kv_h = c_kv.astype(BF16)
    k_rope = col(_OFF_KR, ROPE_SLOT) * cos + col(_OFF_KROT, ROPE_SLOT) * sin
    k_rope_h = k_rope.astype(BF16)
    kr_ref[...] = k_rope[:, :ROPE_DIM]

    gb_ref[...] = col(_OFF_GB, CONV_DIM)
    uc_ref[...] = col(_OFF_GC, CONV_DIM) * col(_OFF_XC, CONV_DIM)

    qsb_ref[...] = col(_OFF_QSB, SB_WIDTH).astype(BF16)
    k_sb = col(_OFF_KSB, SB_WIDTH)
    v_sb = col(_OFF_VSB, SB_WIDTH)
    tm = k_sb.shape[0]
    for h, hs in enumerate(_SB_HEAD_COLS):
        ksb_ref[pl.ds(h, tm, stride=SB_HEADS), :] = k_sb[:, hs]
        vsb_ref[pl.ds(h, tm, stride=SB_HEADS), :] = v_sb[:, hs]
    ksbh_ref[...] = k_sb.astype(BF16)
    vsbh_ref[...] = v_sb.astype(BF16)

    c_q = _rms(col(_OFF_CQ, Q_LORA), qg_ref[...]).astype(BF16)
    q_nope = _dot(c_q, wuq_ref[:, :_UQ_ROPE]).astype(BF16)
    q_rope = (_dot(c_q, wuq_ref[:, _UQ_ROPE:_UQ_ROT]) * _lane_tile(cos, MLA_HEADS)
              + _dot(c_q, wuq_ref[:, _UQ_ROT:]) * _lane_tile(sin, MLA_HEADS)).astype(BF16)
    heads = [slice(h * LANES, (h + 1) * LANES) for h in range(MLA_HEADS)]

    @pl.when(is_prompt)
    def _():
        k_nope = _dot(c_kv_h, wuk_ref[...]).astype(BF16)
        v = _dot(c_kv_h, wuv_ref[...]).astype(BF16)
        for h, hs in enumerate(heads):
            qp_ref[h, :, :NOPE_DIM] = q_nope[:, hs]
            qp_ref[h, :, NOPE_DIM:] = q_rope[:, hs]
            kp_ref[h, :, :NOPE_DIM] = k_nope[:, hs]
            kp_ref[h, :, NOPE_DIM:] = k_rope_h
            vp_ref[h] = v[:, hs]

    @pl.when(jnp.logical_not(is_prompt))
    def _():
        kv_ref[:, :KV_LORA] = c_kv_h
        kv_ref[:, KV_LORA:] = k_rope_h
        for h, hs in enumerate(heads):
            q_ref[h, :, :KV_LORA] = _dot(q_nope[:, hs], wukt_ref[h]).astype(BF16)
            q_ref[h, :, KV_LORA:] = q_rope[:, hs]


def _proj(h, pre_g, w_in, qg, kvg, w_uq, w_ukt, w_uk, w_uv, cos, sin, layer, tm, n_prompt_rows):
    r, d = h.shape
    rp, rs = n_prompt_rows, r - n_prompt_rows
    n_prompt = rp // tm
    row = lambda w: pl.BlockSpec((tm, w), lambda i: (i, 0))
    sb_heads = lambda: pl.BlockSpec((tm * SB_HEADS, SB_DIM), lambda i: (i, 0))
    prompt3 = lambda w: pl.BlockSpec((MLA_HEADS, tm, w), lambda i: (0, jnp.minimum(i, n_prompt - 1), 0))
    sample3 = lambda w: pl.BlockSpec((MLA_HEADS, tm, w), lambda i: (0, jnp.maximum(i - n_prompt, 0), 0))
    slab = lambda a: _layer_slab(a, layer, 1)
    sds = jax.ShapeDtypeStruct
    return pl.pallas_call(
        functools.partial(_proj_body, n_prompt=n_prompt),
        grid=(r // tm,),
        in_specs=[row(d), slab(pre_g), slab(w_in), slab(qg), slab(kvg), slab(w_uq), slab(w_ukt), slab(w_uk),
                  slab(w_uv), row(ROPE_SLOT), row(ROPE_SLOT)],
        out_specs=[
            prompt3(NOPE_DIM + ROPE_SLOT), prompt3(NOPE_DIM + ROPE_SLOT), prompt3(V_DIM), sample3(KV_W),
            pl.BlockSpec((tm, KV_W), lambda i: (jnp.maximum(i - n_prompt, 0), 0)),
            row(KV_LORA), row(ROPE_DIM), row(CONV_DIM), row(CONV_DIM),
            row(SB_WIDTH), sb_heads(), sb_heads(), row(SB_WIDTH), row(SB_WIDTH),
        ],
        out_shape=[
            sds((MLA_HEADS, rp, NOPE_DIM + ROPE_SLOT), BF16), sds((MLA_HEADS, rp, NOPE_DIM + ROPE_SLOT), BF16),
            sds((MLA_HEADS, rp, V_DIM), BF16), sds((MLA_HEADS, rs, KV_W), BF16), sds((rs, KV_W), BF16),
            sds((r, KV_LORA), F32), sds((r, ROPE_DIM), F32), sds((r, CONV_DIM), F32), sds((r, CONV_DIM), F32),
            sds((r, SB_WIDTH), BF16), sds((r * SB_HEADS, SB_DIM), F32), sds((r * SB_HEADS, SB_DIM), F32),
            sds((r, SB_WIDTH), BF16), sds((r, SB_WIDTH), BF16),
        ],
        compiler_params=_params("arbitrary"),
        name="proj",
    )(h, pre_g, w_in, qg, kvg, w_uq, w_ukt, w_uk, w_uv, cos, sin)


_LOG2E = 1.4426950408889634


def _mla_prompt_body(q_ref, k_ref, v_ref, o_ref, m_ref, l_ref, acc_ref, *, tq, tk):
    qi = pl.program_id(1)
    m_ref[...] = jnp.full_like(m_ref, NEG_INF)
    l_ref[...] = jnp.zeros_like(l_ref)
    acc_ref[...] = jnp.zeros_like(acc_ref)

    def step(kt, masked):
        keys = pl.ds(pl.multiple_of(kt * tk, tk), tk)
        if masked:
            qpos = qi * tq + lax.broadcasted_iota(jnp.int32, (tq, 1), 0)
            kpos = kt * tk + lax.broadcasted_iota(jnp.int32, (1, tk), 1)
            visible = (kpos >> _CHUNK_SHIFT) <= (qpos >> _CHUNK_SHIFT)

        def scores(h):
            s = _dot_nt(q_ref[h], k_ref[h, keys, :]) * (MLA_SCALE * _LOG2E)
            return jnp.where(visible, s, NEG_INF) if masked else s

        s_next = scores(0)
        for h in range(MLA_HEADS):
            s = s_next
            if h + 1 < MLA_HEADS:
                s_next = scores(h + 1)
            rs = slice(h * tq, (h + 1) * tq)
            m_old = m_ref[rs]
            m_new = jnp.maximum(m_old, jnp.max(s, axis=-1, keepdims=True))
            alpha = jnp.exp2(m_old - m_new)
            p = jnp.exp2(s - _lane_tile(m_new, tk // LANES))
            l_ref[rs] = alpha * l_ref[rs] + sum(p[:, c * LANES:(c + 1) * LANES] for c in range(tk // LANES))
            acc_ref[rs] = alpha * acc_ref[rs] + _dot(p.astype(BF16), v_ref[h, keys, :])
            m_ref[rs] = m_new

    n_full = (qi * tq) // tk
    lax.fori_loop(0, n_full, lambda kt, c: (step(kt, False), c)[1], 0)
    for d in range(-(-tq // tk)):
        step(n_full + d, True)

    for h in range(MLA_HEADS):
        rs = slice(h * tq, (h + 1) * tq)
        o_ref[:, h * V_DIM:(h + 1) * V_DIM] = acc_ref[rs] / jnp.sum(l_ref[rs], axis=-1, keepdims=True)


def _mla_prompt(q, k, v, batch, seq, tq, tk):
    nq = seq // tq
    rows = MLA_HEADS * tq
    whole_batch = lambda w: pl.BlockSpec((MLA_HEADS, seq, w), lambda b, i: (0, b, 0), pipeline_mode=pl.Buffered(1))
    return pl.pallas_call(
        functools.partial(_mla_prompt_body, tq=tq, tk=tk),
        grid=(batch, nq),
        in_specs=[
            pl.BlockSpec((MLA_HEADS, tq, q.shape[2]), lambda b, i: (0, b * nq + i, 0)),
            whole_batch(k.shape[2]), whole_batch(v.shape[2]),
        ],
        out_specs=pl.BlockSpec((tq, MLA_WIDTH), lambda b, i: (b * nq + i, 0)),
        out_shape=jax.ShapeDtypeStruct((batch * seq, MLA_WIDTH), F32),
        scratch_shapes=[pltpu.VMEM((rows, LANES), F32), pltpu.VMEM((rows, LANES), F32),
                        pltpu.VMEM((rows, V_DIM), F32)],
        compiler_params=_params("parallel", "arbitrary"),
        name="mla_prompt",
    )(q, k, v)


def _mla_up(o, wuv_ref, o_ref, tq):
    for h in range(MLA_HEADS):
        oh = o[h * tq:(h + 1) * tq].astype(BF16)
        o_ref[:, h * V_DIM:(h + 1) * V_DIM] = _dot(oh, wuv_ref[h])


def _mla_sample_body(q_ref, kvn_ref, cc_ref, ckr_ref, wuv_ref, o_ref, *, sq):
    rows = MLA_HEADS * sq
    q = q_ref[...].reshape(rows, KV_W)
    kvn = kvn_ref[...]
    cc = cc_ref[...].astype(BF16)
    ckr = ckr_ref[...].astype(BF16)
    s_c = (_dot_nt(q[:, :KV_LORA], cc) + _dot_nt(q[:, KV_LORA:KV_LORA + ROPE_DIM], ckr)) * MLA_SCALE
    s_n = _dot_nt(q, kvn) * MLA_SCALE
    m = jnp.maximum(jnp.max(s_c, axis=-1, keepdims=True), jnp.max(s_n, axis=-1, keepdims=True))
    p_c = jnp.exp(s_c - m)
    p_n = jnp.exp(s_n - m)
    l = jnp.sum(p_c, axis=-1, keepdims=True) + jnp.sum(p_n, axis=-1, keepdims=True)
    o = _dot(p_c.astype(BF16), cc) + _dot(p_n.astype(BF16), kvn[:, :KV_LORA])
    _mla_up(o / l, wuv_ref, o_ref, sq)


def _mla_sample(q, kv, cache_ckv, cache_kr, w_uv, layer, dec_batch, sq):
    past = cache_ckv.shape[2]
    assert past % CHUNK == 0 and sq <= CHUNK
    return pl.pallas_call(
        functools.partial(_mla_sample_body, sq=sq),
        grid=(dec_batch,),
        in_specs=[
            pl.BlockSpec((MLA_HEADS, sq, KV_W), lambda b: (0, b, 0)),
            pl.BlockSpec((sq, KV_W), lambda b: (b, 0)),
            pl.BlockSpec((None, None, past, KV_LORA), lambda b: (layer, b, 0, 0)),
            pl.BlockSpec((None, None, past, ROPE_DIM), lambda b: (layer, b, 0, 0)),
            _layer_slab(w_uv, layer, 1),
        ],
        out_specs=pl.BlockSpec((sq, MLA_WIDTH), lambda b: (b, 0)),
        out_shape=jax.ShapeDtypeStruct((dec_batch * sq, MLA_WIDTH), F32),
        compiler_params=_params("parallel"),
        name="mla_sample",
    )(q, kv, cache_ckv, cache_kr, w_uv)


def _suffix_ones(n):
    j = lax.broadcasted_iota(jnp.int32, (n, n), 0)
    k = lax.broadcasted_iota(jnp.int32, (n, n), 1)
    return jnp.where(j > k, 1.0, 0.0).astype(BF16)


def _sb_tiles(qs, ks, vs, carries, ones, causal):
    rows = qs[0].shape[0]
    logits = [_dot_nt(q, k) * SB_SCALE for q, k in zip(qs, ks)]
    softplus, log_keep, split = [], [], []
    for x in logits:
        sp = jnp.maximum(x, 0.0) + jnp.log(1.0 + jnp.exp(-jnp.abs(x)))
        lk = -sp if causal is None else jnp.where(causal, -sp, 0.0)
        hi = lk.astype(BF16)
        lo = (lk - hi.astype(F32)).astype(BF16)
        softplus.append(sp)
        log_keep.append(lk)
        split.append(jnp.concatenate([hi, lo], axis=0))
    sums = [_dot(x, ones) for x in split]
    weights = []
    for x, sp, sm, carry in zip(logits, softplus, sums, carries):
        between = sm[:rows] + sm[rows:] + carry
        w = jnp.exp((x - sp) + between)
        weights.append(w if causal is None else jnp.where(causal, w, 0.0))
    outs = [_dot(w.astype(BF16), v) for w, v in zip(weights, vs)]
    return outs, [jnp.sum(lk, axis=-1, keepdims=True) for lk in log_keep]


def _sb_prompt_body(q_ref, k_ref, v_ref, o_ref, carry_ref, *, t):
    qi = pl.program_id(1)
    ones = _suffix_ones(t)
    causal = lax.broadcasted_iota(jnp.int32, (t, t), 1) < lax.broadcasted_iota(jnp.int32, (t, t), 0)
    o_ref[...] = jnp.zeros_like(o_ref)
    carry_ref[...] = jnp.zeros_like(carry_ref)

    def tile(kt, mask):
        rows = pl.ds(pl.multiple_of(kt * t, t), t)
        outs, tots = _sb_tiles([q_ref[:, hs] for hs in _SB_HEAD_COLS], [k_ref[rows, hs] for hs in _SB_HEAD_COLS],
                               [v_ref[rows, hs] for hs in _SB_HEAD_COLS],
                               [_lane_tile(carry_ref[:, hs], t // LANES) for hs in _SB_HEAD_COLS],
                               ones, mask)
        for hs, out, tot in zip(_SB_HEAD_COLS, outs, tots):
            o_ref[:, hs] += out
            carry_ref[:, hs] += tot

    tile(qi, causal)
    lax.fori_loop(0, qi, lambda i, c: (tile(qi - 1 - i, None), c)[1], 0)


def _sb_prompt(q, k, v, batch, seq, t):
    nq = seq // t
    return pl.pallas_call(
        functools.partial(_sb_prompt_body, t=t),
        grid=(batch, nq),
        in_specs=[
            pl.BlockSpec((t, SB_WIDTH), lambda b, i: (b * nq + i, 0)),
            pl.BlockSpec((seq, SB_WIDTH), lambda b, i: (b, 0)),
            pl.BlockSpec((seq, SB_WIDTH), lambda b, i: (b, 0)),
        ],
        out_specs=pl.BlockSpec((t, SB_WIDTH), lambda b, i: (b * nq + i, 0)),
        out_shape=jax.ShapeDtypeStruct((batch * seq, SB_WIDTH), F32),
        scratch_shapes=[pltpu.VMEM((t, SB_HEADS * LANES), F32)],
        compiler_params=_params("parallel", "arbitrary"),
        name="sb_prompt",
    )(q, k, v)


def _sb_sample_body(q_ref, kn_ref, vn_ref, ck_ref, cv_ref, o_ref, *, sq, past, tk):
    ones_new = _suffix_ones(sq)
    ones = _suffix_ones(tk)
    causal = lax.broadcasted_iota(jnp.int32, (sq, sq), 1) < lax.broadcasted_iota(jnp.int32, (sq, sq), 0)
    qs = [q_ref[:, hs] for hs in _SB_HEAD_COLS]
    accs, carries = _sb_tiles(qs, [kn_ref[:, hs] for hs in _SB_HEAD_COLS], [vn_ref[:, hs] for hs in _SB_HEAD_COLS],
                              [0.0] * SB_HEADS, ones_new, causal)
    for kt in reversed(range(past // tk)):
        rows = [pl.ds(kt * tk * SB_HEADS + h, tk, stride=SB_HEADS) for h in range(SB_HEADS)]
        outs, tots = _sb_tiles(qs, [ck_ref[r, :].astype(BF16) for r in rows],
                               [cv_ref[r, :].astype(BF16) for r in rows], carries, ones, None)
        accs = [a + o for a, o in zip(accs, outs)]
        carries = [c + t for c, t in zip(carries, tots)]
    for hs, acc in zip(_SB_HEAD_COLS, accs):
        o_ref[:, hs] = acc


def _sb_sample(q, k, v, cache_k, cache_v, layer, row0, dec_batch, sq):
    past = cache_k.shape[2] // SB_HEADS
    b0 = row0 // sq
    new = lambda: pl.BlockSpec((sq, SB_WIDTH), lambda b: (b0 + b, 0))
    old = lambda: pl.BlockSpec((None, None, past * SB_HEADS, SB_DIM), lambda b: (layer, b, 0, 0))
    return pl.pallas_call(
        functools.partial(_sb_sample_body, sq=sq, past=past, tk=_pick_tile(256, past)),
        grid=(dec_batch,),
        in_specs=[new(), new(), new(), old(), old()],
        out_specs=pl.BlockSpec((sq, SB_WIDTH), lambda b: (b, 0)),
        out_shape=jax.ShapeDtypeStruct((dec_batch * sq, SB_WIDTH), F32),
        compiler_params=_params("parallel"),
        name="sb_sample",
    )(q, k, v, cache_k, cache_v)


def _merge_body(h_ref, yap_ref, yas_ref, gb_ref, uc_ref, prev_ref, ycp_ref, ycs_ref, cw_ref, ga_ref, gconv_ref,
                gc_ref, wo_ref, post_ref, o_ref, yb_ref, *, tm, n_prompt):
    is_prompt = pl.program_id(0) < n_prompt
    cw = cw_ref[...]
    row = lax.broadcasted_iota(jnp.int32, (SEG, 1), 0)
    for s in range(tm // SEG):
        rows = slice(s * SEG, (s + 1) * SEG)
        u = uc_ref[rows, :]
        prev = prev_ref[s]
        u1 = jnp.where(row == 0, prev[1:2], pltpu.roll(u, 1, 0))
        u2 = jnp.where(row == 0, prev[0:1], jnp.where(row == 1, prev[1:2], pltpu.roll(u, 2, 0)))
        conv = cw[0:1] * u2 + cw[1:2] * u1 + cw[2:3] * u
        yb_ref[rows, :] = _rms(gb_ref[rows, :] * conv, gconv_ref[...]).astype(BF16)
    ya = _rms(jnp.where(is_prompt, yap_ref[...], yas_ref[...]), ga_ref[...]).astype(BF16)
    yc = _rms(jnp.where(is_prompt, ycp_ref[...], ycs_ref[...]), gc_ref[...]).astype(BF16)
    m = (_dot(ya, wo_ref[:MLA_WIDTH, :]) + _dot(yb_ref[...], wo_ref[MLA_WIDTH:MLA_WIDTH + CONV_DIM, :])
         + _dot(yc, wo_ref[MLA_WIDTH + CONV_DIM:, :]))
    o_ref[...] = h_ref[...] + _rms(m, post_ref[...])


def _merge(h, ya_p, ya_s, gb, uc, prev, yc_p, yc_s, conv_w, ga, gconv, gc, w_o, post_g, layer, tm):
    r, d = h.shape
    n_prompt = ya_p.shape[0] // tm
    row = lambda w: pl.BlockSpec((tm, w), lambda i: (i, 0))
    prompt = lambda w: pl.BlockSpec((tm, w), lambda i: (jnp.minimum(i, n_prompt - 1), 0))
    sample = lambda w: pl.BlockSpec((tm, w), lambda i: (jnp.maximum(i - n_prompt, 0), 0))
    slab = lambda a: _layer_slab(a, layer, 1)
    return pl.pallas_call(
        functools.partial(_merge_body, tm=tm, n_prompt=n_prompt),
        grid=(r // tm,),
        in_specs=[
            row(d), prompt(MLA_WIDTH), sample(MLA_WIDTH), row(CONV_DIM), row(CONV_DIM),
            pl.BlockSpec((tm // SEG, CONV_W - 1, CONV_DIM), lambda i: (i, 0, 0)),
            prompt(SB_WIDTH), sample(SB_WIDTH), slab(conv_w), slab(ga), slab(gconv), slab(gc), slab(w_o), slab(post_g),
        ],
        out_specs=row(d),
        out_shape=jax.ShapeDtypeStruct((r, d), F32),
        scratch_shapes=[pltpu.VMEM((tm, CONV_DIM), BF16)],
        compiler_params=_params("parallel"),
        name="merge",
    )(h, ya_p, ya_s, gb, uc, prev, yc_p, yc_s, conv_w, ga, gconv, gc, w_o, post_g)


def _rot_half(w):
    half = ROPE_DIM // 2
    return jnp.concatenate([-w[..., half:], w[..., :half]], axis=-1)


def _pad_slot(w):
    return jnp.pad(w, [(0, 0)] * (w.ndim - 1) + [(0, ROPE_SLOT - w.shape[-1])])


def _pack_w_in(w):
    kr0 = Q_LORA + KV_LORA
    k_rope = w[..., kr0:kr0 + ROPE_DIM]
    return jnp.concatenate([w[..., :kr0], w[..., kr0 + ROPE_DIM:], _pad_slot(k_rope), _pad_slot(_rot_half(k_rope))],
                           axis=-1).astype(BF16)


def _pack_w_uq(w):
    depth = w.shape[0]
    w = w.reshape(depth, Q_LORA, MLA_HEADS, NOPE_DIM + ROPE_DIM)
    flat = lambda a: a.reshape(depth, Q_LORA, -1)
    rope = w[..., NOPE_DIM:]
    return jnp.concatenate([flat(w[..., :NOPE_DIM]), flat(_pad_slot(rope)), flat(_pad_slot(_rot_half(rope)))],
                           axis=-1).astype(BF16)


def _rope_tables(pos):
    half = ROPE_DIM // 2
    inv_freq = ROPE_THETA ** (-jnp.arange(half, dtype=F32) / half)
    ang = pos.astype(F32)[:, None] * inv_freq[None, :]
    pad = lambda t: jnp.pad(jnp.concatenate([t, t], axis=1), ((0, 0), (0, ROPE_SLOT - ROPE_DIM)))
    return pad(jnp.cos(ang)), pad(jnp.sin(ang))


def kernel(x_prompt, x_sample, cache_mla_ckv, cache_mla_krope, cache_sb_k, cache_sb_v, state_conv, ffn1_pre_g, ffn1_w_gate, ffn1_w_up, ffn1_w_down, ffn1_post_g, mix_pre_g, w_in, mla_q_norm_g, mla_kv_norm_g, mla_w_uq, mla_w_uk, mla_w_uv, conv_w, out_norm_mla_g, out_norm_conv_g, out_norm_sb_g, w_o, mix_post_g, ffn2_pre_g, ffn2_w_gate, ffn2_w_up, ffn2_w_down, ffn2_post_g):
    batch, seq, d = x_prompt.shape
    dec_batch, sq, _ = x_sample.shape
    depth, _, past, _ = cache_mla_ckv.shape
    rp, rs = batch * seq, dec_batch * sq
    assert d == D_MODEL and sq == SEG and seq % SEG == 0

    tm_ffn = _pick_tile(1024, rp + rs)
    tf_ffn = _pick_tile(256, ffn1_w_gate.shape[2])
    tm = _pick_tile(256, rp, rs)
    t_attn = _pick_tile(256, seq)

    pos = jnp.concatenate([jnp.tile(jnp.arange(seq, dtype=jnp.int32), batch),
                           jnp.tile(past + jnp.arange(sq, dtype=jnp.int32), dec_batch)])
    cos, sin = _rope_tables(pos)
    cache_k = cache_sb_k.reshape(depth, dec_batch, past * SB_HEADS, SB_DIM)
    cache_v = cache_sb_v.reshape(depth, dec_batch, past * SB_HEADS, SB_DIM)
    gain = lambda g: g.reshape(depth, 1, -1)

    w_in_p = _pack_w_in(w_in)
    w_uq_p = _pack_w_uq(mla_w_uq)
    w_ukt = jnp.transpose(mla_w_uk, (0, 2, 3, 1)).astype(BF16)
    w_uv = jnp.transpose(mla_w_uv, (0, 2, 1, 3)).astype(BF16)
    w_uk_flat = mla_w_uk.reshape(depth, KV_LORA, MLA_HEADS * NOPE_DIM).astype(BF16)
    w_uv_flat = mla_w_uv.reshape(depth, KV_LORA, MLA_WIDTH).astype(BF16)
    w_o_h = w_o.astype(BF16)

    h = jnp.concatenate([x_prompt.reshape(rp, d), x_sample.reshape(rs, d)], axis=0)
    new = []
    for l in range(depth):
        h = _ffn(h, gain(ffn1_pre_g), ffn1_w_gate, ffn1_w_up, ffn1_w_down, gain(ffn1_post_g), l, tm_ffn, tf_ffn)

        qp, kp, vp, q, kv, ckv, kr, gb, uc, qsb, ksb, vsb, ksbh, vsbh = _proj(
            h, gain(mix_pre_g), w_in_p, gain(mla_q_norm_g), gain(mla_kv_norm_g), w_uq_p, w_ukt, w_uk_flat, w_uv_flat,
            cos, sin, l, tm, rp)

        ya_p = _mla_prompt(qp, kp, vp, batch, seq, t_attn, _pick_tile(2 * t_attn, seq))
        ya_s = _mla_sample(q, kv, cache_mla_ckv, cache_mla_krope, w_uv, l, dec_batch, sq)
        yc_p = _sb_prompt(qsb, ksbh, vsbh, batch, seq, t_attn)
        yc_s = _sb_sample(qsb, ksbh, vsbh, cache_k, cache_v, l, rp, dec_batch, sq)

        tails = uc[:rp].reshape(batch, seq // SEG, SEG, CONV_DIM)[:, :, SEG - (CONV_W - 1):]
        prev_p = jnp.concatenate([jnp.zeros_like(tails[:, :1]), tails[:, :-1]], axis=1)
        prev = jnp.concatenate([prev_p.reshape(-1, CONV_W - 1, CONV_DIM), state_conv[l]], axis=0)
        new_conv_s = uc[rp:].reshape(dec_batch, sq, CONV_DIM)[:, sq - (CONV_W - 1):]

        h = _merge(h, ya_p, ya_s, gb, uc, prev, yc_p, yc_s, conv_w, gain(out_norm_mla_g), gain(out_norm_conv_g),
                   gain(out_norm_sb_g), w_o_h, gain(mix_post_g), l, tm)
        h = _ffn(h, gain(ffn2_pre_g), ffn2_w_gate, ffn2_w_up, ffn2_w_down, gain(ffn2_post_g), l, tm_ffn, tf_ffn)

        new.append((ckv, kr, ksb, vsb, tails[:, -1], new_conv_s))

    def stack(i, lo, hi, shape, rows_per_token=1):
        return jnp.stack([n[i][lo * rows_per_token:hi * rows_per_token].reshape(shape) for n in new], axis=0)

    p4, s4 = (batch, seq, SB_HEADS, SB_DIM), (dec_batch, sq, SB_HEADS, SB_DIM)
    return (h[:rp].reshape(batch, seq, d), h[rp:].reshape(dec_batch, sq, d),
            stack(0, 0, rp, (batch, seq, KV_LORA)), stack(1, 0, rp, (batch, seq, ROPE_DIM)),
            stack(2, 0, rp, p4, SB_HEADS), stack(3, 0, rp, p4, SB_HEADS), jnp.stack([n[4] for n in new], axis=0),
            stack(0, rp, rp + rs, (dec_batch, sq, KV_LORA)), stack(1, rp, rp + rs, (dec_batch, sq, ROPE_DIM)),
            stack(2, rp, rp + rs, s4, SB_HEADS), stack(3, rp, rp + rs, s4, SB_HEADS),
            jnp.stack([n[5] for n in new], axis=0))
```

```python
import functools

import jax
import jax.numpy as jnp
from jax import lax
from jax.experimental import pallas as pl
from jax.experimental.pallas import tpu as pltpu

F32 = jnp.float32
BF16 = jnp.bfloat16

D_MODEL = 2048
CHUNK = 64
EPS = 1e-6
MLA_HEADS = 8
Q_LORA = 512
KV_LORA = 512
NOPE_DIM = 128
ROPE_DIM = 64
V_DIM = 128
ROPE_THETA = 10000.0
MLA_WIDTH = MLA_HEADS * V_DIM
CONV_DIM = 512
CONV_W = 3
SB_HEADS = 4
SB_DIM = 128
SB_WIDTH = SB_HEADS * SB_DIM
_SB_HEAD_COLS = [slice(h * SB_DIM, (h + 1) * SB_DIM) for h in range(SB_HEADS)]
MLA_SCALE = (NOPE_DIM + ROPE_DIM) ** -0.5
SB_SCALE = SB_DIM ** -0.5
NEG_INF = -1e30
_LOG2E = 1.4426950408889634
_CHUNK_SHIFT = CHUNK.bit_length() - 1
assert 1 << _CHUNK_SHIFT == CHUNK

LANES = 128
ROPE_SLOT = LANES
KV_W = KV_LORA + ROPE_SLOT
QK_W = NOPE_DIM + ROPE_SLOT
SEG = 64
VMEM_LIMIT = 56 * 1024 * 1024
_MLA_HEAD_COLS = [slice(h * LANES, (h + 1) * LANES) for h in range(MLA_HEADS)]
assert NOPE_DIM == LANES and V_DIM == LANES

_OFF_CQ, _OFF_CKV, _OFF_GB, _OFF_GC, _OFF_XC, _OFF_QSB, _OFF_KSB, _OFF_VSB = (512 * i for i in range(8))
_OFF_KR = 4096
_OFF_KROT = 4096 + ROPE_SLOT
W_IN_PACKED = 4096 + 2 * ROPE_SLOT
_UQ_ROPE = MLA_HEADS * NOPE_DIM
_UQ_ROT = _UQ_ROPE + MLA_HEADS * ROPE_SLOT
W_UQ_PACKED = _UQ_ROT + MLA_HEADS * ROPE_SLOT


def _params(*sem):
    return pltpu.CompilerParams(dimension_semantics=sem, vmem_limit_bytes=VMEM_LIMIT)


def _layer_slab(a, layer, n_grid, single_buffer=True):
    zeros = (0,) * (a.ndim - 1)
    index_map = (lambda i: (layer,) + zeros) if n_grid == 1 else (lambda i, j: (layer,) + zeros)
    return pl.BlockSpec((None,) + a.shape[1:], index_map, pipeline_mode=pl.Buffered(1) if single_buffer else None)


def _rms(x, g):
    return x * lax.rsqrt(jnp.mean(x * x, axis=-1, keepdims=True) + EPS) * g


def _dot(a, b):
    return jnp.dot(a, b, preferred_element_type=F32)


def _dot_nt(a, b):
    return lax.dot_general(a, b, (((1,), (1,)), ((), ())), preferred_element_type=F32)


def _lane_tile(x, n):
    return jnp.concatenate([x] * n, axis=1)


def _pick_tile(cap, *sizes):
    t = cap
    while any(s % t for s in sizes):
        t //= 2
    return t


def _ffn_body(x_ref, pre_ref, wg_ref, wu_ref, wd_ref, post_ref, o_ref, xn_ref):
    j = pl.program_id(1)

    @pl.when(j == 0)
    def _():
        xn_ref[...] = _rms(x_ref[...], pre_ref[...]).astype(BF16)
        o_ref[...] = jnp.zeros_like(o_ref)

    xn = xn_ref[...]
    a = jax.nn.silu(_dot(xn, wg_ref[...].astype(BF16))) * _dot(xn, wu_ref[...].astype(BF16))
    o_ref[...] += _dot(a.astype(BF16), wd_ref[...].astype(BF16))

    @pl.when(j == pl.num_programs(1) - 1)
    def _():
        o_ref[...] = x_ref[...] + 0.5 * _rms(o_ref[...], post_ref[...])


def _ffn(x, pre_g, w_gate, w_up, w_down, post_g, layer, tm, tf):
    r, d = x.shape
    dff = w_gate.shape[2]
    return pl.pallas_call(
        _ffn_body,
        grid=(r // tm, dff // tf),
        in_specs=[
            pl.BlockSpec((tm, d), lambda i, j: (i, 0), pipeline_mode=pl.Buffered(1)),
            _layer_slab(pre_g, layer, 2),
            pl.BlockSpec((None, d, tf), lambda i, j: (layer, 0, j)),
            pl.BlockSpec((None, d, tf), lambda i, j: (layer, 0, j)),
            pl.BlockSpec((None, tf, d), lambda i, j: (layer, j, 0)),
            _layer_slab(post_g, layer, 2),
        ],
        out_specs=pl.BlockSpec((tm, d), lambda i, j: (i, 0)),
        out_shape=jax.ShapeDtypeStruct((r, d), F32),
        scratch_shapes=[pltpu.VMEM((tm, d), BF16)],
        compiler_params=_params("parallel", "arbitrary"),
        name="ffn",
    )(x, pre_g, w_gate, w_up, w_down, post_g)


def _proj_shared(h_ref, pre_ref, win_ref, qg_ref, kvg_ref, wuq_ref, cos_ref, sin_ref,
                 ckv_ref, kr_ref, ksb_ref, vsb_ref, gb_ref, uc_ref, qsb_ref, ksbh_ref, vsbh_ref):
    xn = _rms(h_ref[...], pre_ref[...]).astype(BF16)
    cos = cos_ref[...]
    sin = sin_ref[...]

    def col(off, width):
        return _dot(xn, win_ref[:, off:off + width])

    c_kv = _rms(col(_OFF_CKV, KV_LORA), kvg_ref[...])
    ckv_ref[...] = c_kv
    k_rope = col(_OFF_KR, ROPE_SLOT) * cos + col(_OFF_KROT, ROPE_SLOT) * sin
    kr_ref[...] = k_rope[:, :ROPE_DIM]

    gb_ref[...] = col(_OFF_GB, CONV_DIM)
    uc_ref[...] = col(_OFF_GC, CONV_DIM) * col(_OFF_XC, CONV_DIM)

    qsb_ref[...] = col(_OFF_QSB, SB_WIDTH).astype(BF16)
    k_sb = col(_OFF_KSB, SB_WIDTH)
    v_sb = col(_OFF_VSB, SB_WIDTH)
    tm = k_sb.shape[0]
    for h, hs in enumerate(_SB_HEAD_COLS):
        ksb_ref[pl.ds(h, tm, stride=SB_HEADS), :] = k_sb[:, hs]
        vsb_ref[pl.ds(h, tm, stride=SB_HEADS), :] = v_sb[:, hs]
    ksbh_ref[...] = k_sb.astype(BF16)
    vsbh_ref[...] = v_sb.astype(BF16)

    c_q = _rms(col(_OFF_CQ, Q_LORA), qg_ref[...]).astype(BF16)
    q_nope = _dot(c_q, wuq_ref[:, :_UQ_ROPE]).astype(BF16)
    q_rope = (_dot(c_q, wuq_ref[:, _UQ_ROPE:_UQ_ROT]) * _lane_tile(cos, MLA_HEADS)
              + _dot(c_q, wuq_ref[:, _UQ_ROT:]) * _lane_tile(sin, MLA_HEADS)).astype(BF16)
    return c_kv.astype(BF16), k_rope.astype(BF16), q_nope, q_rope


def _proj_prompt_body(h_ref, pre_ref, win_ref, qg_ref, kvg_ref, wuq_ref, cos_ref, sin_ref, wuk_ref, wuv_ref,
                      ckv_in, kr_in, ksb_in, vsb_in,
                      ckv_ref, kr_ref, ksb_ref, vsb_ref, gb_ref, uc_ref, qsb_ref, ksbh_ref, vsbh_ref,
                      q_ref, k_ref, v_ref):
    del ckv_in, kr_in, ksb_in, vsb_in
    c_kv, k_rope, q_nope, q_rope = _proj_shared(
        h_ref, pre_ref, win_ref, qg_ref, kvg_ref, wuq_ref, cos_ref, sin_ref,
        ckv_ref, kr_ref, ksb_ref, vsb_ref, gb_ref, uc_ref, qsb_ref, ksbh_ref, vsbh_ref)
    k_nope = _dot(c_kv, wuk_ref[...]).astype(BF16)
    v = _dot(c_kv, wuv_ref[...]).astype(BF16)
    for h, hs in enumerate(_MLA_HEAD_COLS):
        q_ref[h, :, :NOPE_DIM] = q_nope[:, hs]
        q_ref[h, :, NOPE_DIM:] = q_rope[:, hs]
        k_ref[h, :, :NOPE_DIM] = k_nope[:, hs]
        k_ref[h, :, NOPE_DIM:] = k_rope
        v_ref[h] = v[:, hs]


def _proj_sample_body(h_ref, pre_ref, win_ref, qg_ref, kvg_ref, wuq_ref, cos_ref, sin_ref, wukt_ref,
                      ckv_in, kr_in, ksb_in, vsb_in,
                      ckv_ref, kr_ref, ksb_ref, vsb_ref, gb_ref, uc_ref, qsb_ref, ksbh_ref, vsbh_ref,
                      q_ref, kv_ref):
    del ckv_in, kr_in, ksb_in, vsb_in
    c_kv, k_rope, q_nope, q_rope = _proj_shared(
        h_ref, pre_ref, win_ref, qg_ref, kvg_ref, wuq_ref, cos_ref, sin_ref,
        ckv_ref, kr_ref, ksb_ref, vsb_ref, gb_ref, uc_ref, qsb_ref, ksbh_ref, vsbh_ref)
    kv_ref[:, :KV_LORA] = c_kv
    kv_ref[:, KV_LORA:] = k_rope
    for h, hs in enumerate(_MLA_HEAD_COLS):
        q_ref[h, :, :KV_LORA] = _dot(q_nope[:, hs], wukt_ref[h]).astype(BF16)
        q_ref[h, :, KV_LORA:] = q_rope[:, hs]


def _proj(body, h, shared, extra, cos, sin, state, layer, tm, attn_shapes, attn_specs):
    r, d = h.shape
    n_tab = cos.shape[0] // tm
    row = lambda w: pl.BlockSpec((tm, w), lambda i: (i, 0))
    table = pl.BlockSpec((tm, ROPE_SLOT), lambda i: (i % n_tab, 0))
    slab = lambda a: _layer_slab(a, layer, 1)
    state_spec = lambda a: pl.BlockSpec((None, a.shape[1] // (r // tm), a.shape[2]), lambda i: (layer, i, 0))
    sds = jax.ShapeDtypeStruct
    n_in = 1 + len(shared) + 2 + len(extra)
    outs = pl.pallas_call(
        body,
        grid=(r // tm,),
        in_specs=([row(d)] + [slab(a) for a in shared] + [table, table] + [slab(a) for a in extra]
                  + [pl.BlockSpec(memory_space=pl.ANY)] * len(state)),
        out_specs=([state_spec(a) for a in state]
                   + [row(CONV_DIM), row(CONV_DIM), row(SB_WIDTH), row(SB_WIDTH), row(SB_WIDTH)] + attn_specs),
        out_shape=([sds(a.shape, a.dtype) for a in state]
                   + [sds((r, CONV_DIM), F32), sds((r, CONV_DIM), F32), sds((r, SB_WIDTH), BF16),
                      sds((r, SB_WIDTH), BF16), sds((r, SB_WIDTH), BF16)] + attn_shapes),
        input_output_aliases={n_in + k: k for k in range(len(state))},
        compiler_params=_params("parallel"),
        name=body.__name__.strip("_").replace("_body", ""),
    )(h, *shared, cos, sin, *extra, *state)
    return outs[:len(state)], outs[len(state):]


def _proj_prompt(h, shared, w_uk, w_uv, cos, sin, state, layer, tm):
    r = h.shape[0]
    heads3 = lambda w: pl.BlockSpec((MLA_HEADS, tm, w), lambda i: (0, i, 0))
    sds = jax.ShapeDtypeStruct
    return _proj(_proj_prompt_body, h, shared, (w_uk, w_uv), cos, sin, state, layer, tm,
                 [sds((MLA_HEADS, r, QK_W), BF16), sds((MLA_HEADS, r, QK_W), BF16), sds((MLA_HEADS, r, V_DIM), BF16)],
                 [heads3(QK_W), heads3(QK_W), heads3(V_DIM)])


def _proj_sample(h, shared, w_ukt, cos, sin, state, layer, tm):
    r = h.shape[0]
    sds = jax.ShapeDtypeStruct
    return _proj(_proj_sample_body, h, shared, (w_ukt,), cos, sin, state, layer, tm,
                 [sds((MLA_HEADS, r, KV_W), BF16), sds((r, KV_W), BF16)],
                 [pl.BlockSpec((MLA_HEADS, tm, KV_W), lambda i: (0, i, 0)), pl.BlockSpec((tm, KV_W), lambda i: (i, 0))])


def _mla_prompt_body(q_ref, k_ref, v_ref, o_ref, m_ref, l_ref, acc_ref, *, tq, tk):
    qi = pl.program_id(1)
    m_ref[...] = jnp.full_like(m_ref, NEG_INF)
    l_ref[...] = jnp.zeros_like(l_ref)
    acc_ref[...] = jnp.zeros_like(acc_ref)

    def step(kt, masked):
        keys = pl.ds(pl.multiple_of(kt * tk, tk), tk)
        if masked:
            qpos = qi * tq + lax.broadcasted_iota(jnp.int32, (tq, 1), 0)
            kpos = kt * tk + lax.broadcasted_iota(jnp.int32, (1, tk), 1)
            visible = (kpos >> _CHUNK_SHIFT) <= (qpos >> _CHUNK_SHIFT)

        def scores(h):
            s = _dot_nt(q_ref[h], k_ref[h, keys, :]) * (MLA_SCALE * _LOG2E)
            return jnp.where(visible, s, NEG_INF) if masked else s

        s_next = scores(0)
        for h in range(MLA_HEADS):
            s = s_next
            if h + 1 < MLA_HEADS:
                s_next = scores(h + 1)
            rs = slice(h * tq, (h + 1) * tq)
            m_old = m_ref[rs]
            m_new = jnp.maximum(m_old, jnp.max(s, axis=-1, keepdims=True))
            alpha = jnp.exp2(m_old - m_new)
            p = jnp.exp2(s - _lane_tile(m_new, tk // LANES))
            l_ref[rs] = alpha * l_ref[rs] + sum(p[:, c * LANES:(c + 1) * LANES] for c in range(tk // LANES))
            acc_ref[rs] = alpha * acc_ref[rs] + _dot(p.astype(BF16), v_ref[h, keys, :])
            m_ref[rs] = m_new

    n_full = (qi * tq) // tk
    lax.fori_loop(0, n_full, lambda kt, c: (step(kt, False), c)[1], 0)
    for d in range(-(-tq // tk)):
        step(n_full + d, True)

    for h in range(MLA_HEADS):
        rs = slice(h * tq, (h + 1) * tq)
        o_ref[:, h * V_DIM:(h + 1) * V_DIM] = acc_ref[rs] / jnp.sum(l_ref[rs], axis=-1, keepdims=True)


def _mla_prompt(q, k, v, batch, seq, tq, tk):
    nq = seq // tq
    rows = MLA_HEADS * tq
    whole_batch = lambda w: pl.BlockSpec((MLA_HEADS, seq, w), lambda b, i: (0, b, 0), pipeline_mode=pl.Buffered(1))
    return pl.pallas_call(
        functools.partial(_mla_prompt_body, tq=tq, tk=tk),
        grid=(batch, nq),
        in_specs=[
            pl.BlockSpec((MLA_HEADS, tq, QK_W), lambda b, i: (0, b * nq + i, 0)),
            whole_batch(QK_W), whole_batch(V_DIM),
        ],
        out_specs=pl.BlockSpec((tq, MLA_WIDTH), lambda b, i: (b * nq + i, 0)),
        out_shape=jax.ShapeDtypeStruct((batch * seq, MLA_WIDTH), F32),
        scratch_shapes=[pltpu.VMEM((rows, LANES), F32), pltpu.VMEM((rows, LANES), F32),
                        pltpu.VMEM((rows, V_DIM), F32)],
        compiler_params=_params("parallel", "arbitrary"),
        name="mla_prompt",
    )(q, k, v)


def _mla_up(o, wuv_ref, o_ref, tq):
    for h in range(MLA_HEADS):
        oh = o[h * tq:(h + 1) * tq].astype(BF16)
        o_ref[:, h * V_DIM:(h + 1) * V_DIM] = _dot(oh, wuv_ref[h])


def _mla_sample_body(q_ref, kvn_ref, cc_ref, ckr_ref, wuv_ref, o_ref, *, sq):
    rows = MLA_HEADS * sq
    q = q_ref[...].reshape(rows, KV_W)
    kvn = kvn_ref[...]
    cc = cc_ref[...].astype(BF16)
    ckr = ckr_ref[...].astype(BF16)
    s_c = (_dot_nt(q[:, :KV_LORA], cc) + _dot_nt(q[:, KV_LORA:KV_LORA + ROPE_DIM], ckr)) * MLA_SCALE
    s_n = _dot_nt(q, kvn) * MLA_SCALE
    m = jnp.maximum(jnp.max(s_c, axis=-1, keepdims=True), jnp.max(s_n, axis=-1, keepdims=True))
    p_c = jnp.exp(s_c - m)
    p_n = jnp.exp(s_n - m)
    l = jnp.sum(p_c, axis=-1, keepdims=True) + jnp.sum(p_n, axis=-1, keepdims=True)
    o = _dot(p_c.astype(BF16), cc) + _dot(p_n.astype(BF16), kvn[:, :KV_LORA])
    _mla_up(o / l, wuv_ref, o_ref, sq)


def _mla_sample(q, kv, cache_ckv, cache_kr, w_uv, layer, dec_batch, sq):
    past = cache_ckv.shape[2]
    assert past % CHUNK == 0 and sq <= CHUNK
    return pl.pallas_call(
        functools.partial(_mla_sample_body, sq=sq),
        grid=(dec_batch,),
        in_specs=[
            pl.BlockSpec((MLA_HEADS, sq, KV_W), lambda b: (0, b, 0)),
            pl.BlockSpec((sq, KV_W), lambda b: (b, 0)),
            pl.BlockSpec((None, None, past, KV_LORA), lambda b: (layer, b, 0, 0)),
            pl.BlockSpec((None, None, past, ROPE_DIM), lambda b: (layer, b, 0, 0)),
            _layer_slab(w_uv, layer, 1),
        ],
        out_specs=pl.BlockSpec((sq, MLA_WIDTH), lambda b: (b, 0)),
        out_shape=jax.ShapeDtypeStruct((dec_batch * sq, MLA_WIDTH), F32),
        compiler_params=_params("parallel"),
        name="mla_sample",
    )(q, kv, cache_ckv, cache_kr, w_uv)


def _suffix_ones(n):
    j = lax.broadcasted_iota(jnp.int32, (n, n), 0)
    k = lax.broadcasted_iota(jnp.int32, (n, n), 1)
    return jnp.where(j > k, 1.0, 0.0).astype(BF16)


def _sb_tiles(qs, ks, vs, carries, ones, causal):
    rows = qs[0].shape[0]
    logits = [_dot_nt(q, k) * SB_SCALE for q, k in zip(qs, ks)]
    softplus, log_keep, split = [], [], []
    for x in logits:
        sp = jnp.maximum(x, 0.0) + jnp.log(1.0 + jnp.exp(-jnp.abs(x)))
        lk = -sp if causal is None else jnp.where(causal, -sp, 0.0)
        hi = lk.astype(BF16)
        lo = (lk - hi.astype(F32)).astype(BF16)
        softplus.append(sp)
        log_keep.append(lk)
        split.append(jnp.concatenate([hi, lo], axis=0))
    sums = [_dot(x, ones) for x in split]
    weights = []
    for x, sp, sm, carry in zip(logits, softplus, sums, carries):
        between = sm[:rows] + sm[rows:] + carry
        w = jnp.exp((x - sp) + between)
        weights.append(w if causal is None else jnp.where(causal, w, 0.0))
    outs = [_dot(w.astype(BF16), v) for w, v in zip(weights, vs)]
    return outs, [jnp.sum(lk, axis=-1, keepdims=True) for lk in log_keep]


def _sb_prompt_body(q_ref, k_ref, v_ref, o_ref, carry_ref, *, t):
    qi = pl.program_id(1)
    ones = _suffix_ones(t)
    causal = lax.broadcasted_iota(jnp.int32, (t, t), 1) < lax.broadcasted_iota(jnp.int32, (t, t), 0)
    o_ref[...] = jnp.zeros_like(o_ref)
    carry_ref[...] = jnp.zeros_like(carry_ref)

    def tile(kt, mask):
        rows = pl.ds(pl.multiple_of(kt * t, t), t)
        outs, tots = _sb_tiles([q_ref[:, hs] for hs in _SB_HEAD_COLS], [k_ref[rows, hs] for hs in _SB_HEAD_COLS],
                               [v_ref[rows, hs] for hs in _SB_HEAD_COLS],
                               [_lane_tile(carry_ref[:, hs], t // LANES) for hs in _SB_HEAD_COLS],
                               ones, mask)
        for hs, out, tot in zip(_SB_HEAD_COLS, outs, tots):
            o_ref[:, hs] += out
            carry_ref[:, hs] += tot

    tile(qi, causal)
    lax.fori_loop(0, qi, lambda i, c: (tile(qi - 1 - i, None), c)[1], 0)


def _sb_prompt(q, k, v, batch, seq, t):
    nq = seq // t
    return pl.pallas_call(
        functools.partial(_sb_prompt_body, t=t),
        grid=(batch, nq),
        in_specs=[
            pl.BlockSpec((t, SB_WIDTH), lambda b, i: (b * nq + i, 0)),
            pl.BlockSpec((seq, SB_WIDTH), lambda b, i: (b, 0)),
            pl.BlockSpec((seq, SB_WIDTH), lambda b, i: (b, 0)),
        ],
        out_specs=pl.BlockSpec((t, SB_WIDTH), lambda b, i: (b * nq + i, 0)),
        out_shape=jax.ShapeDtypeStruct((batch * seq, SB_WIDTH), F32),
        scratch_shapes=[pltpu.VMEM((t, SB_HEADS * LANES), F32)],
        compiler_params=_params("parallel", "arbitrary"),
        name="sb_prompt",
    )(q, k, v)


def _sb_sample_body(q_ref, kn_ref, vn_ref, ck_ref, cv_ref, o_ref, *, sq, past, tk):
    ones_new = _suffix_ones(sq)
    ones = _suffix_ones(tk)
    causal = lax.broadcasted_iota(jnp.int32, (sq, sq), 1) < lax.broadcasted_iota(jnp.int32, (sq, sq), 0)
    qs = [q_ref[:, hs] for hs in _SB_HEAD_COLS]
    accs, carries = _sb_tiles(qs, [kn_ref[:, hs] for hs in _SB_HEAD_COLS], [vn_ref[:, hs] for hs in _SB_HEAD_COLS],
                              [0.0] * SB_HEADS, ones_new, causal)
    for kt in reversed(range(past // tk)):
        rows = [pl.ds(kt * tk * SB_HEADS + h, tk, stride=SB_HEADS) for h in range(SB_HEADS)]
        outs, tots = _sb_tiles(qs, [ck_ref[r, :].astype(BF16) for r in rows],
                               [cv_ref[r, :].astype(BF16) for r in rows], carries, ones, None)
        accs = [a + o for a, o in zip(accs, outs)]
        carries = [c + t for c, t in zip(carries, tots)]
    for hs, acc in zip(_SB_HEAD_COLS, accs):
        o_ref[:, hs] = acc


def _sb_sample(q, k, v, cache_k, cache_v, layer, dec_batch, sq):
    past = cache_k.shape[2] // SB_HEADS
    new = lambda: pl.BlockSpec((sq, SB_WIDTH), lambda b: (b, 0))
    old = lambda: pl.BlockSpec((None, None, past * SB_HEADS, SB_DIM), lambda b: (layer, b, 0, 0))
    return pl.pallas_call(
        functools.partial(_sb_sample_body, sq=sq, past=past, tk=_pick_tile(256, past)),
        grid=(dec_batch,),
        in_specs=[new(), new(), new(), old(), old()],
        out_specs=pl.BlockSpec((sq, SB_WIDTH), lambda b: (b, 0)),
        out_shape=jax.ShapeDtypeStruct((dec_batch * sq, SB_WIDTH), F32),
        compiler_params=_params("parallel"),
        name="sb_sample",
    )(q, k, v, cache_k, cache_v)


def _merge_body(h_ref, ya_ref, gb_ref, uc_ref, prev_ref, yc_ref, cw_ref, ga_ref, gconv_ref, gc_ref,
                wo_ref, post_ref, o_ref, yb_ref, *, tm):
    cw = cw_ref[...]
    row = lax.broadcasted_iota(jnp.int32, (SEG, 1), 0)
    for s in range(tm // SEG):
        rows = slice(s * SEG, (s + 1) * SEG)
        u = uc_ref[rows, :]
        prev = prev_ref[s]
        u1 = jnp.where(row == 0, prev[1:2], pltpu.roll(u, 1, 0))
        u2 = jnp.where(row == 0, prev[0:1], jnp.where(row == 1, prev[1:2], pltpu.roll(u, 2, 0)))
        conv = cw[0:1] * u2 + cw[1:2] * u1 + cw[2:3] * u
        yb_ref[rows, :] = _rms(gb_ref[rows, :] * conv, gconv_ref[...]).astype(BF16)
    ya = _rms(ya_ref[...], ga_ref[...]).astype(BF16)
    yc = _rms(yc_ref[...], gc_ref[...]).astype(BF16)
    m = (_dot(ya, wo_ref[:MLA_WIDTH, :]) + _dot(yb_ref[...], wo_ref[MLA_WIDTH:MLA_WIDTH + CONV_DIM, :])
         + _dot(yc, wo_ref[MLA_WIDTH + CONV_DIM:, :]))
    o_ref[...] = h_ref[...] + _rms(m, post_ref[...])


def _merge(h, ya, gb, uc, prev, yc, conv_w, ga, gconv, gc, w_o, post_g, layer, tm):
    r, d = h.shape
    row = lambda w: pl.BlockSpec((tm, w), lambda i: (i, 0))
    slab = lambda a: _layer_slab(a, layer, 1)
    return pl.pallas_call(
        functools.partial(_merge_body, tm=tm),
        grid=(r // tm,),
        in_specs=[
            row(d), row(MLA_WIDTH), row(CONV_DIM), row(CONV_DIM),
            pl.BlockSpec((tm // SEG, CONV_W - 1, CONV_DIM), lambda i: (i, 0, 0)),
            row(SB_WIDTH), slab(conv_w), slab(ga), slab(gconv), slab(gc), slab(w_o), slab(post_g),
        ],
        out_specs=row(d),
        out_shape=jax.ShapeDtypeStruct((r, d), F32),
        scratch_shapes=[pltpu.VMEM((tm, CONV_DIM), BF16)],
        compiler_params=_params("parallel"),
        name="merge",
    )(h, ya, gb, uc, prev, yc, conv_w, ga, gconv, gc, w_o, post_g)


def _rot_half(w):
    half = ROPE_DIM // 2
    return jnp.concatenate([-w[..., half:], w[..., :half]], axis=-1)


def _pad_slot(w):
    return jnp.pad(w, [(0, 0)] * (w.ndim - 1) + [(0, ROPE_SLOT - w.shape[-1])])


def _pack_w_in(w):
    kr0 = Q_LORA + KV_LORA
    k_rope = w[..., kr0:kr0 + ROPE_DIM]
    return jnp.concatenate([w[..., :kr0], w[..., kr0 + ROPE_DIM:], _pad_slot(k_rope), _pad_slot(_rot_half(k_rope))],
                           axis=-1).astype(BF16)


def _pack_w_uq(w):
    depth = w.shape[0]
    w = w.reshape(depth, Q_LORA, MLA_HEADS, NOPE_DIM + ROPE_DIM)
    flat = lambda a: a.reshape(depth, Q_LORA, -1)
    rope = w[..., NOPE_DIM:]
    return jnp.concatenate([flat(w[..., :NOPE_DIM]), flat(_pad_slot(rope)), flat(_pad_slot(_rot_half(rope)))],
                           axis=-1).astype(BF16)


def _rope_tables(pos):
    half = ROPE_DIM // 2
    inv_freq = ROPE_THETA ** (-jnp.arange(half, dtype=F32) / half)
    ang = pos.astype(F32)[:, None] * inv_freq[None, :]
    pad = lambda t: jnp.pad(jnp.concatenate([t, t], axis=1), ((0, 0), (0, ROPE_SLOT - ROPE_DIM)))
    return pad(jnp.cos(ang)), pad(jnp.sin(ang))


def kernel(x_prompt, x_sample, cache_mla_ckv, cache_mla_krope, cache_sb_k, cache_sb_v, state_conv, ffn1_pre_g, ffn1_w_gate, ffn1_w_up, ffn1_w_down, ffn1_post_g, mix_pre_g, w_in, mla_q_norm_g, mla_kv_norm_g, mla_w_uq, mla_w_uk, mla_w_uv, conv_w, out_norm_mla_g, out_norm_conv_g, out_norm_sb_g, w_o, mix_post_g, ffn2_pre_g, ffn2_w_gate, ffn2_w_up, ffn2_w_down, ffn2_post_g):
    batch, seq, d = x_prompt.shape
    dec_batch, sq, _ = x_sample.shape
    depth, _, past, _ = cache_mla_ckv.shape
    rp, rs = batch * seq, dec_batch * sq
    assert d == D_MODEL and sq == SEG and seq % SEG == 0

    tm_ffn = _pick_tile(1024, rp, rs)
    tf_ffn = _pick_tile(256, ffn1_w_gate.shape[2])
    tm = _pick_tile(256, seq, rs)
    t_attn = _pick_tile(256, seq)

    cos_p, sin_p = _rope_tables(jnp.arange(seq, dtype=jnp.int32))
    cos_s, sin_s = _rope_tables(jnp.tile(past + jnp.arange(sq, dtype=jnp.int32), tm // sq))
    cache_k = cache_sb_k.reshape(depth, dec_batch, past * SB_HEADS, SB_DIM)
    cache_v = cache_sb_v.reshape(depth, dec_batch, past * SB_HEADS, SB_DIM)
    gain = lambda g: g.reshape(depth, 1, -1)

    shared = (gain(mix_pre_g), _pack_w_in(w_in), gain(mla_q_norm_g), gain(mla_kv_norm_g), _pack_w_uq(mla_w_uq))
    w_ukt = jnp.transpose(mla_w_uk, (0, 2, 3, 1)).astype(BF16)
    w_uv = jnp.transpose(mla_w_uv, (0, 2, 1, 3)).astype(BF16)
    w_uk_flat = mla_w_uk.reshape(depth, KV_LORA, MLA_HEADS * NOPE_DIM).astype(BF16)
    w_uv_flat = mla_w_uv.reshape(depth, KV_LORA, MLA_WIDTH).astype(BF16)
    merge_w = (conv_w, gain(out_norm_mla_g), gain(out_norm_conv_g), gain(out_norm_sb_g), w_o.astype(BF16),
               gain(mix_post_g))

    def new_state(rows):
        return (jnp.zeros((depth, rows, KV_LORA), F32), jnp.zeros((depth, rows, ROPE_DIM), F32),
                jnp.zeros((depth, rows * SB_HEADS, SB_DIM), F32), jnp.zeros((depth, rows * SB_HEADS, SB_DIM), F32))

    h_p, h_s = x_prompt.reshape(rp, d), x_sample.reshape(rs, d)
    state_p, state_s = new_state(rp), new_state(rs)
    conv_p, conv_s = [], []
    for l in range(depth):
        ffn1 = lambda x: _ffn(x, gain(ffn1_pre_g), ffn1_w_gate, ffn1_w_up, ffn1_w_down, gain(ffn1_post_g), l,
                              tm_ffn, tf_ffn)
        ffn2 = lambda x: _ffn(x, gain(ffn2_pre_g), ffn2_w_gate, ffn2_w_up, ffn2_w_down, gain(ffn2_post_g), l,
                              tm_ffn, tf_ffn)
        h_p, h_s = ffn1(h_p), ffn1(h_s)

        state_p, (gb, uc, qsb, ksbh, vsbh, q, k, v) = _proj_prompt(h_p, shared, w_uk_flat, w_uv_flat, cos_p, sin_p,
                                                                  state_p, l, tm)
        ya = _mla_prompt(q, k, v, batch, seq, t_attn, _pick_tile(2 * t_attn, seq))
        yc = _sb_prompt(qsb, ksbh, vsbh, batch, seq, t_attn)
        tails = uc.reshape(batch, seq // SEG, SEG, CONV_DIM)[:, :, SEG - (CONV_W - 1):]
        prev = jnp.concatenate([jnp.zeros_like(tails[:, :1]), tails[:, :-1]], axis=1)
        h_p = _merge(h_p, ya, gb, uc, prev.reshape(-1, CONV_W - 1, CONV_DIM), yc, *merge_w, l, tm)
        conv_p.append(tails[:, -1])

        state_s, (gb, uc, qsb, ksbh, vsbh, q, kv) = _proj_sample(h_s, shared, w_ukt, cos_s, sin_s, state_s, l, tm)
        ya = _mla_sample(q, kv, cache_mla_ckv, cache_mla_krope, w_uv, l, dec_batch, sq)
        yc = _sb_sample(qsb, ksbh, vsbh, cache_k, cache_v, l, dec_batch, sq)
        h_s = _merge(h_s, ya, gb, uc, state_conv[l], yc, *merge_w, l, tm)
        conv_s.append(uc.reshape(dec_batch, sq, CONV_DIM)[:, sq - (CONV_W - 1):])

        h_p, h_s = ffn2(h_p), ffn2(h_s)

    def state_out(state, lead):
        ckv, kr, sbk, sbv = state
        return (ckv.reshape(depth, *lead, KV_LORA), kr.reshape(depth, *lead, ROPE_DIM),
                sbk.reshape(depth, *lead, SB_HEADS, SB_DIM), sbv.reshape(depth, *lead, SB_HEADS, SB_DIM))

    return (h_p.reshape(batch, seq, d), h_s.reshape(dec_batch, sq, d),
            *state_out(state_p, (batch, seq)), jnp.stack(conv_p, axis=0),
            *state_out(state_s, (dec_batch, sq)), jnp.stack(conv_s, axis=0))
```

```python
import functools

import jax
import jax.numpy as jnp
from jax import lax
from jax.experimental import pallas as pl
from jax.experimental.pallas import tpu as pltpu

F32 = jnp.float32
BF16 = jnp.bfloat16

D_MODEL = 2048
CHUNK = 64
EPS = 1e-6
MLA_HEADS = 8
Q_LORA = 512
KV_LORA = 512
NOPE_DIM = 128
ROPE_DIM = 64
V_DIM = 128
ROPE_THETA = 10000.0
MLA_WIDTH = MLA_HEADS * V_DIM
CONV_DIM = 512
CONV_W = 3
SB_HEADS = 4
SB_DIM = 128
SB_WIDTH = SB_HEADS * SB_DIM
_SB_HEAD_COLS = [slice(h * SB_DIM, (h + 1) * SB_DIM) for h in range(SB_HEADS)]
MLA_SCALE = (NOPE_DIM + ROPE_DIM) ** -0.5
SB_SCALE = SB_DIM ** -0.5
NEG_INF = -1e30
_LOG2E = 1.4426950408889634
_CHUNK_SHIFT = CHUNK.bit_length() - 1
assert 1 << _CHUNK_SHIFT == CHUNK

LANES = 128
ROPE_SLOT = LANES
KV_W = KV_LORA + ROPE_SLOT
QK_W = NOPE_DIM + ROPE_SLOT
SEG = 64
VMEM_LIMIT = 56 * 1024 * 1024
_MLA_HEAD_COLS = [slice(h * LANES, (h + 1) * LANES) for h in range(MLA_HEADS)]
assert NOPE_DIM == LANES and V_DIM == LANES

_OFF_CQ, _OFF_CKV, _OFF_GB, _OFF_GC, _OFF_XC, _OFF_QSB, _OFF_KSB, _OFF_VSB = (512 * i for i in range(8))
_OFF_KR = 4096
_OFF_KROT = 4096 + ROPE_SLOT
W_IN_PACKED = 4096 + 2 * ROPE_SLOT
_UQ_ROPE = MLA_HEADS * NOPE_DIM
_UQ_ROT = _UQ_ROPE + MLA_HEADS * ROPE_SLOT
W_UQ_PACKED = _UQ_ROT + MLA_HEADS * ROPE_SLOT


def _params(*sem):
    return pltpu.CompilerParams(dimension_semantics=sem, vmem_limit_bytes=VMEM_LIMIT)


def _layer_slab(a, layer, n_grid, single_buffer=True):
    zeros = (0,) * (a.ndim - 1)
    index_map = (lambda i: (layer,) + zeros) if n_grid == 1 else (lambda i, j: (layer,) + zeros)
    return pl.BlockSpec((None,) + a.shape[1:], index_map, pipeline_mode=pl.Buffered(1) if single_buffer else None)


def _rms(x, g):
    return x * lax.rsqrt(jnp.mean(x * x, axis=-1, keepdims=True) + EPS) * g


def _dot(a, b):
    return jnp.dot(a, b, preferred_element_type=F32)


def _dot_nt(a, b):
    return lax.dot_general(a, b, (((1,), (1,)), ((), ())), preferred_element_type=F32)


def _lane_tile(x, n):
    return jnp.concatenate([x] * n, axis=1)


def _pick_tile(cap, *sizes):
    t = cap
    while any(s % t for s in sizes):
        t //= 2
    return t


def _ffn_body(x_ref, pre_ref, wg_ref, wu_ref, wd_ref, post_ref, o_ref, xn_ref):
    j = pl.program_id(1)

    @pl.when(j == 0)
    def _():
        xn_ref[...] = _rms(x_ref[...], pre_ref[...]).astype(BF16)
        o_ref[...] = jnp.zeros_like(o_ref)

    xn = xn_ref[...]
    a = jax.nn.silu(_dot(xn, wg_ref[...].astype(BF16))) * _dot(xn, wu_ref[...].astype(BF16))
    o_ref[...] += _dot(a.astype(BF16), wd_ref[...].astype(BF16))

    @pl.when(j == pl.num_programs(1) - 1)
    def _():
        o_ref[...] = x_ref[...] + 0.5 * _rms(o_ref[...], post_ref[...])


def _ffn(x, pre_g, w_gate, w_up, w_down, post_g, layer, tm, tf):
    r, d = x.shape
    dff = w_gate.shape[2]
    return pl.pallas_call(
        _ffn_body,
        grid=(r // tm, dff // tf),
        in_specs=[
            pl.BlockSpec((tm, d), lambda i, j: (i, 0), pipeline_mode=pl.Buffered(1)),
            _layer_slab(pre_g, layer, 2),
            pl.BlockSpec((None, d, tf), lambda i, j: (layer, 0, j)),
            pl.BlockSpec((None, d, tf), lambda i, j: (layer, 0, j)),
            pl.BlockSpec((None, tf, d), lambda i, j: (layer, j, 0)),
            _layer_slab(post_g, layer, 2),
        ],
        out_specs=pl.BlockSpec((tm, d), lambda i, j: (i, 0)),
        out_shape=jax.ShapeDtypeStruct((r, d), F32),
        scratch_shapes=[pltpu.VMEM((tm, d), BF16)],
        compiler_params=_params("parallel", "arbitrary"),
        name="ffn",
    )(x, pre_g, w_gate, w_up, w_down, post_g)


def _proj_shared(h_ref, pre_ref, win_ref, qg_ref, kvg_ref, wuq_ref, cos_ref, sin_ref,
                 ckv_ref, kr_ref, ksb_ref, vsb_ref, gb_ref, uc_ref, qsb_ref, ksbh_ref, vsbh_ref):
    xn = _rms(h_ref[...], pre_ref[...]).astype(BF16)
    cos = cos_ref[...]
    sin = sin_ref[...]

    def col(off, width):
        return _dot(xn, win_ref[:, off:off + width])

    c_kv = _rms(col(_OFF_CKV, KV_LORA), kvg_ref[...])
    ckv_ref[...] = c_kv
    k_rope = col(_OFF_KR, ROPE_SLOT) * cos + col(_OFF_KROT, ROPE_SLOT) * sin
    kr_ref[...] = k_rope[:, :ROPE_DIM]

    gb_ref[...] = col(_OFF_GB, CONV_DIM)
    uc_ref[...] = col(_OFF_GC, CONV_DIM) * col(_OFF_XC, CONV_DIM)

    qsb_ref[...] = col(_OFF_QSB, SB_WIDTH).astype(BF16)
    k_sb = col(_OFF_KSB, SB_WIDTH)
    v_sb = col(_OFF_VSB, SB_WIDTH)
    tm = k_sb.shape[0]
    for h, hs in enumerate(_SB_HEAD_COLS):
        ksb_ref[pl.ds(h, tm, stride=SB_HEADS), :] = k_sb[:, hs]
        vsb_ref[pl.ds(h, tm, stride=SB_HEADS), :] = v_sb[:, hs]
    ksbh_ref[...] = k_sb.astype(BF16)
    vsbh_ref[...] = v_sb.astype(BF16)

    c_q = _rms(col(_OFF_CQ, Q_LORA), qg_ref[...]).astype(BF16)
    q_nope = _dot(c_q, wuq_ref[:, :_UQ_ROPE]).astype(BF16)
    q_rope = (_dot(c_q, wuq_ref[:, _UQ_ROPE:_UQ_ROT]) * _lane_tile(cos, MLA_HEADS)
              + _dot(c_q, wuq_ref[:, _UQ_ROT:]) * _lane_tile(sin, MLA_HEADS)).astype(BF16)
    return c_kv.astype(BF16), k_rope.astype(BF16), q_nope, q_rope


def _proj_prompt_body(h_ref, pre_ref, win_ref, qg_ref, kvg_ref, wuq_ref, cos_ref, sin_ref, wuk_ref, wuv_ref,
                      ckv_in, kr_in, ksb_in, vsb_in,
                      ckv_ref, kr_ref, ksb_ref, vsb_ref, gb_ref, uc_ref, qsb_ref, ksbh_ref, vsbh_ref,
                      q_ref, k_ref, v_ref):
    del ckv_in, kr_in, ksb_in, vsb_in
    c_kv, k_rope, q_nope, q_rope = _proj_shared(
        h_ref, pre_ref, win_ref, qg_ref, kvg_ref, wuq_ref, cos_ref, sin_ref,
        ckv_ref, kr_ref, ksb_ref, vsb_ref, gb_ref, uc_ref, qsb_ref, ksbh_ref, vsbh_ref)
    k_nope = _dot(c_kv, wuk_ref[...]).astype(BF16)
    v = _dot(c_kv, wuv_ref[...]).astype(BF16)
    for h, hs in enumerate(_MLA_HEAD_COLS):
        q_ref[h, :, :NOPE_DIM] = q_nope[:, hs]
        q_ref[h, :, NOPE_DIM:] = q_rope[:, hs]
        k_ref[h, :, :NOPE_DIM] = k_nope[:, hs]
        k_ref[h, :, NOPE_DIM:] = k_rope
        v_ref[h] = v[:, hs]


def _proj_sample_body(h_ref, pre_ref, win_ref, qg_ref, kvg_ref, wuq_ref, cos_ref, sin_ref, wukt_ref,
                      ckv_in, kr_in, ksb_in, vsb_in,
                      ckv_ref, kr_ref, ksb_ref, vsb_ref, gb_ref, uc_ref, qsb_ref, ksbh_ref, vsbh_ref,
                      q_ref, kv_ref):
    del ckv_in, kr_in, ksb_in, vsb_in
    c_kv, k_rope, q_nope, q_rope = _proj_shared(
        h_ref, pre_ref, win_ref, qg_ref, kvg_ref, wuq_ref, cos_ref, sin_ref,
        ckv_ref, kr_ref, ksb_ref, vsb_ref, gb_ref, uc_ref, qsb_ref, ksbh_ref, vsbh_ref)
    kv_ref[:, :KV_LORA] = c_kv
    kv_ref[:, KV_LORA:] = k_rope
    for h, hs in enumerate(_MLA_HEAD_COLS):
        q_ref[h, :, :KV_LORA] = _dot(q_nope[:, hs], wukt_ref[h]).astype(BF16)
        q_ref[h, :, KV_LORA:] = q_rope[:, hs]


def _proj(body, h, shared, extra, cos, sin, state, layer, tm, attn_shapes, attn_specs):
    r, d = h.shape
    n_tab = cos.shape[0] // tm
    row = lambda w: pl.BlockSpec((tm, w), lambda i: (i, 0))
    table = pl.BlockSpec((tm, ROPE_SLOT), lambda i: (i % n_tab, 0))
    slab = lambda a: _layer_slab(a, layer, 1)
    state_spec = lambda a: pl.BlockSpec((None, a.shape[1] // (r // tm), a.shape[2]), lambda i: (layer, i, 0))
    sds = jax.ShapeDtypeStruct
    n_in = 1 + len(shared) + 2 + len(extra)
    outs = pl.pallas_call(
        body,
        grid=(r // tm,),
        in_specs=([row(d)] + [slab(a) for a in shared] + [table, table] + [slab(a) for a in extra]
                  + [pl.BlockSpec(memory_space=pl.ANY)] * len(state)),
        out_specs=([state_spec(a) for a in state]
                   + [row(CONV_DIM), row(CONV_DIM), row(SB_WIDTH), row(SB_WIDTH), row(SB_WIDTH)] + attn_specs),
        out_shape=([sds(a.shape, a.dtype) for a in state]
                   + [sds((r, CONV_DIM), F32), sds((r, CONV_DIM), F32), sds((r, SB_WIDTH), BF16),
                      sds((r, SB_WIDTH), BF16), sds((r, SB_WIDTH), BF16)] + attn_shapes),
        input_output_aliases={n_in + k: k for k in range(len(state))},
        compiler_params=_params("parallel"),
        name=body.__name__.strip("_").replace("_body", ""),
    )(h, *shared, cos, sin, *extra, *state)
    return outs[:len(state)], outs[len(state):]


def _proj_prompt(h, shared, w_uk, w_uv, cos, sin, state, layer, tm):
    r = h.shape[0]
    heads3 = lambda w: pl.BlockSpec((MLA_HEADS, tm, w), lambda i: (0, i, 0))
    sds = jax.ShapeDtypeStruct
    return _proj(_proj_prompt_body, h, shared, (w_uk, w_uv), cos, sin, state, layer, tm,
                 [sds((MLA_HEADS, r, QK_W), BF16), sds((MLA_HEADS, r, QK_W), BF16), sds((MLA_HEADS, r, V_DIM), BF16)],
                 [heads3(QK_W), heads3(QK_W), heads3(V_DIM)])


def _proj_sample(h, shared, w_ukt, cos, sin, state, layer, tm):
    r = h.shape[0]
    sds = jax.ShapeDtypeStruct
    return _proj(_proj_sample_body, h, shared, (w_ukt,), cos, sin, state, layer, tm,
                 [sds((MLA_HEADS, r, KV_W), BF16), sds((r, KV_W), BF16)],
                 [pl.BlockSpec((MLA_HEADS, tm, KV_W), lambda i: (0, i, 0)), pl.BlockSpec((tm, KV_W), lambda i: (i, 0))])


def _mla_prompt_body(q_ref, k_ref, v_ref, o_ref, m_ref, l_ref, acc_ref, *, tq, tk):
    qi = pl.program_id(1)
    m_ref[...] = jnp.full_like(m_ref, NEG_INF)
    l_ref[...] = jnp.zeros_like(l_ref)
    acc_ref[...] = jnp.zeros_like(acc_ref)

    def step(kts, masked):
        keys = [pl.ds(pl.multiple_of(kt * tk, tk), tk) for kt in kts]
        if masked:
            qpos = qi * tq + lax.broadcasted_iota(jnp.int32, (tq, 1), 0)
            kpos = [kt * tk + lax.broadcasted_iota(jnp.int32, (1, tk), 1) for kt in kts]
            visible = [(kp >> _CHUNK_SHIFT) <= (qpos >> _CHUNK_SHIFT) for kp in kpos]

        def scores(h):
            s = [_dot_nt(q_ref[h], k_ref[h, kk, :]) * (MLA_SCALE * _LOG2E) for kk in keys]
            return [jnp.where(vis, x, NEG_INF) for vis, x in zip(visible, s)] if masked else s

        s_next = scores(0)
        for h in range(MLA_HEADS):
            s = s_next
            if h + 1 < MLA_HEADS:
                s_next = scores(h + 1)
            rs = slice(h * tq, (h + 1) * tq)
            m_old = m_ref[rs]
            m_new = functools.reduce(jnp.maximum, [m_old] + [jnp.max(x, axis=-1, keepdims=True) for x in s])
            alpha = jnp.exp2(m_old - m_new)
            p = [jnp.exp2(x - _lane_tile(m_new, tk // LANES)) for x in s]
            l_ref[rs] = alpha * l_ref[rs] + sum(x[:, c * LANES:(c + 1) * LANES] for x in p for c in range(tk // LANES))
            acc_ref[rs] = alpha * acc_ref[rs] + sum(_dot(x.astype(BF16), v_ref[h, kk, :]) for x, kk in zip(p, keys))
            m_ref[rs] = m_new

    n_full = (qi * tq) // tk
    lax.fori_loop(0, n_full // 2, lambda i, c: (step([2 * i, 2 * i + 1], False), c)[1], 0)

    @pl.when(n_full % 2 == 1)
    def _():
        step([n_full - 1], False)

    for d in range(-(-tq // tk)):
        step([n_full + d], True)

    for h in range(MLA_HEADS):
        rs = slice(h * tq, (h + 1) * tq)
        o_ref[:, h * V_DIM:(h + 1) * V_DIM] = acc_ref[rs] / jnp.sum(l_ref[rs], axis=-1, keepdims=True)


def _mla_prompt(q, k, v, batch, seq, tq, tk):
    nq = seq // tq
    rows = MLA_HEADS * tq
    whole_batch = lambda w: pl.BlockSpec((MLA_HEADS, seq, w), lambda b, i: (0, b, 0), pipeline_mode=pl.Buffered(1))
    return pl.pallas_call(
        functools.partial(_mla_prompt_body, tq=tq, tk=tk),
        grid=(batch, nq),
        in_specs=[
            pl.BlockSpec((MLA_HEADS, tq, QK_W), lambda b, i: (0, b * nq + i, 0)),
            whole_batch(QK_W), whole_batch(V_DIM),
        ],
        out_specs=pl.BlockSpec((tq, MLA_WIDTH), lambda b, i: (b * nq + i, 0)),
        out_shape=jax.ShapeDtypeStruct((batch * seq, MLA_WIDTH), F32),
        scratch_shapes=[pltpu.VMEM((rows, LANES), F32), pltpu.VMEM((rows, LANES), F32),
                        pltpu.VMEM((rows, V_DIM), F32)],
        compiler_params=_params("parallel", "arbitrary"),
        name="mla_prompt",
    )(q, k, v)


def _mla_up(o, wuv_ref, o_ref, tq):
    for h in range(MLA_HEADS):
        oh = o[h * tq:(h + 1) * tq].astype(BF16)
        o_ref[:, h * V_DIM:(h + 1) * V_DIM] = _dot(oh, wuv_ref[h])


def _mla_sample_body(q_ref, kvn_ref, cc_ref, ckr_ref, wuv_ref, o_ref, *, sq, tk):
    rows = MLA_HEADS * sq
    q = q_ref[...].reshape(rows, KV_W)
    q_lat, q_rope = q[:, :KV_LORA], q[:, KV_LORA:KV_LORA + ROPE_DIM]
    scale = MLA_SCALE * _LOG2E

    def cache_tile(t):
        keys = slice(t * tk, (t + 1) * tk)
        cc = cc_ref[keys, :].astype(BF16)
        return (_dot_nt(q_lat, cc) + _dot(q_rope, ckr_ref[:, keys].astype(BF16))) * scale, cc

    def new_tile():
        kvn = kvn_ref[...]
        return _dot_nt(q, kvn) * scale, kvn[:, :KV_LORA]

    tiles = [functools.partial(cache_tile, t) for t in range(cc_ref.shape[0] // tk)] + [new_tile]
    m = jnp.full((rows, 1), NEG_INF, F32)
    l = jnp.zeros((rows, 1), F32)
    acc = jnp.zeros((rows, KV_LORA), F32)
    nxt = tiles[0]()
    for i in range(len(tiles)):
        s, values = nxt
        if i + 1 < len(tiles):
            nxt = tiles[i + 1]()
        m_new = jnp.maximum(m, jnp.max(s, axis=-1, keepdims=True))
        alpha = jnp.exp2(m - m_new)
        p = jnp.exp2(s - m_new)
        l = alpha * l + jnp.sum(p, axis=-1, keepdims=True)
        acc = alpha * acc + _dot(p.astype(BF16), values)
        m = m_new
    _mla_up(acc / l, wuv_ref, o_ref, sq)


def _mla_sample(q, kv, cache_ckv, cache_kr_t, w_uv, layer, dec_batch, sq):
    past = cache_ckv.shape[2]
    assert past % CHUNK == 0 and sq <= CHUNK
    return pl.pallas_call(
        functools.partial(_mla_sample_body, sq=sq, tk=_pick_tile(512, past)),
        grid=(dec_batch,),
        in_specs=[
            pl.BlockSpec((MLA_HEADS, sq, KV_W), lambda b: (0, b, 0)),
            pl.BlockSpec((sq, KV_W), lambda b: (b, 0)),
            pl.BlockSpec((None, None, past, KV_LORA), lambda b: (layer, b, 0, 0)),
            pl.BlockSpec((None, None, ROPE_DIM, past), lambda b: (layer, b, 0, 0)),
            _layer_slab(w_uv, layer, 1),
        ],
        out_specs=pl.BlockSpec((sq, MLA_WIDTH), lambda b: (b, 0)),
        out_shape=jax.ShapeDtypeStruct((dec_batch * sq, MLA_WIDTH), F32),
        compiler_params=_params("parallel"),
        name="mla_sample",
    )(q, kv, cache_ckv, cache_kr_t, w_uv)


def _suffix_ones(n):
    j = lax.broadcasted_iota(jnp.int32, (n, n), 0)
    k = lax.broadcasted_iota(jnp.int32, (n, n), 1)
    return jnp.where(j > k, -1.0, 0.0).astype(BF16)


def _sb_tiles(qs, ks, vs, carries, ones, causal, right_of=None):
    rows = qs[0].shape[0]
    logits = [_dot_nt(q, k) * (SB_SCALE * _LOG2E) for q, k in zip(qs, ks)]
    softplus, drop, split = [], [], []
    for x in logits:
        sp = jnp.maximum(x, 0.0) + jnp.log(1.0 + jnp.exp2(-jnp.abs(x))) * _LOG2E
        dr = sp if causal is None else jnp.where(causal, sp, 0.0)
        hi = dr.astype(BF16)
        lo = (dr - hi.astype(F32)).astype(BF16)
        softplus.append(sp)
        drop.append(dr)
        split.append(jnp.concatenate([hi, lo], axis=0))
    sums = [_dot(x, ones) for x in split]
    totals = [-jnp.sum(dr, axis=-1, keepdims=True) for dr in drop]
    weights = []
    for i, (x, sp, sm, carry) in enumerate(zip(logits, softplus, sums, carries)):
        if right_of is not None and right_of[i] is not None:
            carry = carry + totals[right_of[i]]
        between = sm[:rows] + sm[rows:] + carry
        w = jnp.exp2((x - sp) + between)
        weights.append(w if causal is None else jnp.where(causal, w, 0.0))
    outs = [_dot(w.astype(BF16), v) for w, v in zip(weights, vs)]
    return outs, totals


def _sb_prompt_body(q_ref, k_ref, v_ref, o_ref, carry_ref, *, t):
    qi = pl.program_id(1)
    ones = _suffix_ones(t)
    causal = lax.broadcasted_iota(jnp.int32, (t, t), 1) < lax.broadcasted_iota(jnp.int32, (t, t), 0)
    o_ref[...] = jnp.zeros_like(o_ref)
    carry_ref[...] = jnp.zeros_like(carry_ref)

    def tiles(kts, mask):
        rows = [pl.ds(pl.multiple_of(kt * t, t), t) for kt in kts]
        chains = [(r, hs) for r in rows for hs in _SB_HEAD_COLS]
        carries = [_lane_tile(carry_ref[:, hs], t // LANES) for _, hs in chains]
        right_of = [None] * SB_HEADS + list(range(len(chains) - SB_HEADS))
        outs, tots = _sb_tiles([q_ref[:, hs] for _, hs in chains], [k_ref[r, hs] for r, hs in chains],
                               [v_ref[r, hs] for r, hs in chains], carries, ones, mask, right_of)
        for (_, hs), out, tot in zip(chains, outs, tots):
            o_ref[:, hs] += out
            carry_ref[:, hs] += tot

    tiles([qi], causal)
    lax.fori_loop(0, qi // 2, lambda i, c: (tiles([qi - 1 - 2 * i, qi - 2 - 2 * i], None), c)[1], 0)

    @pl.when(qi % 2 == 1)
    def _():
        tiles([0], None)


def _sb_prompt(q, k, v, batch, seq, t):
    nq = seq // t
    return pl.pallas_call(
        functools.partial(_sb_prompt_body, t=t),
        grid=(batch, nq),
        in_specs=[
            pl.BlockSpec((t, SB_WIDTH), lambda b, i: (b * nq + i, 0)),
            pl.BlockSpec((seq, SB_WIDTH), lambda b, i: (b, 0)),
            pl.BlockSpec((seq, SB_WIDTH), lambda b, i: (b, 0)),
        ],
        out_specs=pl.BlockSpec((t, SB_WIDTH), lambda b, i: (b * nq + i, 0)),
        out_shape=jax.ShapeDtypeStruct((batch * seq, SB_WIDTH), F32),
        scratch_shapes=[pltpu.VMEM((t, SB_HEADS * LANES), F32)],
        compiler_params=_params("parallel", "arbitrary"),
        name="sb_prompt",
    )(q, k, v)


def _sb_sample_body(q_ref, kn_ref, vn_ref, ck_ref, cv_ref, o_ref, *, sq, past, tk):
    ones_new = _suffix_ones(sq)
    ones = _suffix_ones(tk)
    causal = lax.broadcasted_iota(jnp.int32, (sq, sq), 1) < lax.broadcasted_iota(jnp.int32, (sq, sq), 0)
    qs = [q_ref[:, hs] for hs in _SB_HEAD_COLS]
    accs, carries = _sb_tiles(qs, [kn_ref[:, hs] for hs in _SB_HEAD_COLS], [vn_ref[:, hs] for hs in _SB_HEAD_COLS],
                              [0.0] * SB_HEADS, ones_new, causal)
    for kt in reversed(range(past // tk)):
        rows = [pl.ds(kt * tk * SB_HEADS + h, tk, stride=SB_HEADS) for h in range(SB_HEADS)]
        outs, tots = _sb_tiles(qs, [ck_ref[r, :].astype(BF16) for r in rows],
                               [cv_ref[r, :].astype(BF16) for r in rows], carries, ones, None)
        accs = [a + o for a, o in zip(accs, outs)]
        carries = [c + t for c, t in zip(carries, tots)]
    for hs, acc in zip(_SB_HEAD_COLS, accs):
        o_ref[:, hs] = acc


def _sb_sample(q, k, v, cache_k, cache_v, layer, dec_batch, sq):
    past = cache_k.shape[2] // SB_HEADS
    new = lambda: pl.BlockSpec((sq, SB_WIDTH), lambda b: (b, 0))
    old = lambda: pl.BlockSpec((None, None, past * SB_HEADS, SB_DIM), lambda b: (layer, b, 0, 0))
    return pl.pallas_call(
        functools.partial(_sb_sample_body, sq=sq, past=past, tk=_pick_tile(256, past)),
        grid=(dec_batch,),
        in_specs=[new(), new(), new(), old(), old()],
        out_specs=pl.BlockSpec((sq, SB_WIDTH), lambda b: (b, 0)),
        out_shape=jax.ShapeDtypeStruct((dec_batch * sq, SB_WIDTH), F32),
        compiler_params=_params("parallel"),
        name="sb_sample",
    )(q, k, v, cache_k, cache_v)


def _merge_body(h_ref, ya_ref, gb_ref, uc_ref, prev_ref, yc_ref, cw_ref, ga_ref, gconv_ref, gc_ref,
                wo_ref, post_ref, o_ref, yb_ref, *, tm):
    cw = cw_ref[...]
    row = lax.broadcasted_iota(jnp.int32, (SEG, 1), 0)
    for s in range(tm // SEG):
        rows = slice(s * SEG, (s + 1) * SEG)
        u = uc_ref[rows, :]
        prev = prev_ref[s]
        u1 = jnp.where(row == 0, prev[1:2], pltpu.roll(u, 1, 0))
        u2 = jnp.where(row == 0, prev[0:1], jnp.where(row == 1, prev[1:2], pltpu.roll(u, 2, 0)))
        conv = cw[0:1] * u2 + cw[1:2] * u1 + cw[2:3] * u
        yb_ref[rows, :] = _rms(gb_ref[rows, :] * conv, gconv_ref[...]).astype(BF16)
    ya = _rms(ya_ref[...], ga_ref[...]).astype(BF16)
    yc = _rms(yc_ref[...], gc_ref[...]).astype(BF16)
    m = (_dot(ya, wo_ref[:MLA_WIDTH, :]) + _dot(yb_ref[...], wo_ref[MLA_WIDTH:MLA_WIDTH + CONV_DIM, :])
         + _dot(yc, wo_ref[MLA_WIDTH + CONV_DIM:, :]))
    o_ref[...] = h_ref[...] + _rms(m, post_ref[...])


def _merge(h, ya, gb, uc, prev, yc, conv_w, ga, gconv, gc, w_o, post_g, layer, tm):
    r, d = h.shape
    row = lambda w: pl.BlockSpec((tm, w), lambda i: (i, 0))
    slab = lambda a: _layer_slab(a, layer, 1)
    return pl.pallas_call(
        functools.partial(_merge_body, tm=tm),
        grid=(r // tm,),
        in_specs=[
            row(d), row(MLA_WIDTH), row(CONV_DIM), row(CONV_DIM),
            pl.BlockSpec((tm // SEG, CONV_W - 1, CONV_DIM), lambda i: (i, 0, 0)),
            row(SB_WIDTH), slab(conv_w), slab(ga), slab(gconv), slab(gc), slab(w_o), slab(post_g),
        ],
        out_specs=row(d),
        out_shape=jax.ShapeDtypeStruct((r, d), F32),
        scratch_shapes=[pltpu.VMEM((tm, CONV_DIM), BF16)],
        compiler_params=_params("parallel"),
        name="merge",
    )(h, ya, gb, uc, prev, yc, conv_w, ga, gconv, gc, w_o, post_g)


def _rot_half(w):
    half = ROPE_DIM // 2
    return jnp.concatenate([-w[..., half:], w[..., :half]], axis=-1)


def _pad_slot(w):
    return jnp.pad(w, [(0, 0)] * (w.ndim - 1) + [(0, ROPE_SLOT - w.shape[-1])])


def _pack_w_in(w):
    kr0 = Q_LORA + KV_LORA
    k_rope = w[..., kr0:kr0 + ROPE_DIM]
    return jnp.concatenate([w[..., :kr0], w[..., kr0 + ROPE_DIM:], _pad_slot(k_rope), _pad_slot(_rot_half(k_rope))],
                           axis=-1).astype(BF16)


def _pack_w_uq(w):
    depth = w.shape[0]
    w = w.reshape(depth, Q_LORA, MLA_HEADS, NOPE_DIM + ROPE_DIM)
    flat = lambda a: a.reshape(depth, Q_LORA, -1)
    rope = w[..., NOPE_DIM:]
    return jnp.concatenate([flat(w[..., :NOPE_DIM]), flat(_pad_slot(rope)), flat(_pad_slot(_rot_half(rope)))],
                           axis=-1).astype(BF16)


def _rope_tables(pos):
    half = ROPE_DIM // 2
    inv_freq = ROPE_THETA ** (-jnp.arange(half, dtype=F32) / half)
    ang = pos.astype(F32)[:, None] * inv_freq[None, :]
    pad = lambda t: jnp.pad(jnp.concatenate([t, t], axis=1), ((0, 0), (0, ROPE_SLOT - ROPE_DIM)))
    return pad(jnp.cos(ang)), pad(jnp.sin(ang))


def kernel(x_prompt, x_sample, cache_mla_ckv, cache_mla_krope, cache_sb_k, cache_sb_v, state_conv, ffn1_pre_g, ffn1_w_gate, ffn1_w_up, ffn1_w_down, ffn1_post_g, mix_pre_g, w_in, mla_q_norm_g, mla_kv_norm_g, mla_w_uq, mla_w_uk, mla_w_uv, conv_w, out_norm_mla_g, out_norm_conv_g, out_norm_sb_g, w_o, mix_post_g, ffn2_pre_g, ffn2_w_gate, ffn2_w_up, ffn2_w_down, ffn2_post_g):
    batch, seq, d = x_prompt.shape
    dec_batch, sq, _ = x_sample.shape
    depth, _, past, _ = cache_mla_ckv.shape
    rp, rs = batch * seq, dec_batch * sq
    assert d == D_MODEL and sq == SEG and seq % SEG == 0

    tm_ffn = _pick_tile(1024, rp, rs)
    tf_ffn = _pick_tile(256, ffn1_w_gate.shape[2])
    tm = _pick_tile(256, seq, rs)
    t_attn = _pick_tile(256, seq)

    cos_p, sin_p = _rope_tables(jnp.arange(seq, dtype=jnp.int32))
    cos_s, sin_s = _rope_tables(jnp.tile(past + jnp.arange(sq, dtype=jnp.int32), tm // sq))
    cache_kr_t = jnp.swapaxes(cache_mla_krope, 2, 3)
    cache_k = cache_sb_k.reshape(depth, dec_batch, past * SB_HEADS, SB_DIM)
    cache_v = cache_sb_v.reshape(depth, dec_batch, past * SB_HEADS, SB_DIM)
    gain = lambda g: g.reshape(depth, 1, -1)

    shared = (gain(mix_pre_g), _pack_w_in(w_in), gain(mla_q_norm_g), gain(mla_kv_norm_g), _pack_w_uq(mla_w_uq))
    w_ukt = jnp.transpose(mla_w_uk, (0, 2, 3, 1)).astype(BF16)
    w_uv = jnp.transpose(mla_w_uv, (0, 2, 1, 3)).astype(BF16)
    w_uk_flat = mla_w_uk.reshape(depth, KV_LORA, MLA_HEADS * NOPE_DIM).astype(BF16)
    w_uv_flat = mla_w_uv.reshape(depth, KV_LORA, MLA_WIDTH).astype(BF16)
    merge_w = (conv_w, gain(out_norm_mla_g), gain(out_norm_conv_g), gain(out_norm_sb_g), w_o.astype(BF16),
               gain(mix_post_g))

    def new_state(rows):
        return (jnp.zeros((depth, rows, KV_LORA), F32), jnp.zeros((depth, rows, ROPE_DIM), F32),
                jnp.zeros((depth, rows * SB_HEADS, SB_DIM), F32), jnp.zeros((depth, rows * SB_HEADS, SB_DIM), F32))

    h_p, h_s = x_prompt.reshape(rp, d), x_sample.reshape(rs, d)
    state_p, state_s = new_state(rp), new_state(rs)
    conv_p, conv_s = [], []
    for l in range(depth):
        ffn1 = lambda x: _ffn(x, gain(ffn1_pre_g), ffn1_w_gate, ffn1_w_up, ffn1_w_down, gain(ffn1_post_g), l,
                              tm_ffn, tf_ffn)
        ffn2 = lambda x: _ffn(x, gain(ffn2_pre_g), ffn2_w_gate, ffn2_w_up, ffn2_w_down, gain(ffn2_post_g), l,
                              tm_ffn, tf_ffn)
        h_p, h_s = ffn1(h_p), ffn1(h_s)

        state_p, (gb, uc, qsb, ksbh, vsbh, q, k, v) = _proj_prompt(h_p, shared, w_uk_flat, w_uv_flat, cos_p, sin_p,
                                                                  state_p, l, tm)
        ya = _mla_prompt(q, k, v, batch, seq, t_attn, _pick_tile(2 * t_attn, seq))
        yc = _sb_prompt(qsb, ksbh, vsbh, batch, seq, t_attn)
        tails = uc.reshape(batch, seq // SEG, SEG, CONV_DIM)[:, :, SEG - (CONV_W - 1):]
        prev = jnp.concatenate([jnp.zeros_like(tails[:, :1]), tails[:, :-1]], axis=1)
        h_p = _merge(h_p, ya, gb, uc, prev.reshape(-1, CONV_W - 1, CONV_DIM), yc, *merge_w, l, tm)
        conv_p.append(tails[:, -1])

        state_s, (gb, uc, qsb, ksbh, vsbh, q, kv) = _proj_sample(h_s, shared, w_ukt, cos_s, sin_s, state_s, l, tm)
        ya = _mla_sample(q, kv, cache_mla_ckv, cache_kr_t, w_uv, l, dec_batch, sq)
        yc = _sb_sample(qsb, ksbh, vsbh, cache_k, cache_v, l, dec_batch, sq)
        h_s = _merge(h_s, ya, gb, uc, state_conv[l], yc, *merge_w, l, tm)
        conv_s.append(uc.reshape(dec_batch, sq, CONV_DIM)[:, sq - (CONV_W - 1):])

        h_p, h_s = ffn2(h_p), ffn2(h_s)

    def state_out(state, lead):
        ckv, kr, sbk, sbv = state
        return (ckv.reshape(depth, *lead, KV_LORA), kr.reshape(depth, *lead, ROPE_DIM),
                sbk.reshape(depth, *lead, SB_HEADS, SB_DIM), sbv.reshape(depth, *lead, SB_HEADS, SB_DIM))

    return (h_p.reshape(batch, seq, d), h_s.reshape(dec_batch, sq, d),
            *state_out(state_p, (batch, seq)), jnp.stack(conv_p, axis=0),
            *state_out(state_s, (dec_batch, sq)), jnp.stack(conv_s, axis=0))
```

```python
import functools

import jax
import jax.numpy as jnp
from jax import lax
from jax.experimental import pallas as pl
from jax.experimental.pallas import tpu as pltpu

F32 = jnp.float32
BF16 = jnp.bfloat16

D_MODEL = 2048
CHUNK = 64
EPS = 1e-6
MLA_HEADS = 8
Q_LORA = 512
KV_LORA = 512
NOPE_DIM = 128
ROPE_DIM = 64
V_DIM = 128
ROPE_THETA = 10000.0
MLA_WIDTH = MLA_HEADS * V_DIM
CONV_DIM = 512
CONV_W = 3
SB_HEADS = 4
SB_DIM = 128
SB_WIDTH = SB_HEADS * SB_DIM
_SB_HEAD_COLS = [slice(h * SB_DIM, (h + 1) * SB_DIM) for h in range(SB_HEADS)]
MLA_SCALE = (NOPE_DIM + ROPE_DIM) ** -0.5
SB_SCALE = SB_DIM ** -0.5
NEG_INF = -1e30
_LOG2E = 1.4426950408889634
_CHUNK_SHIFT = CHUNK.bit_length() - 1
assert 1 << _CHUNK_SHIFT == CHUNK

LANES = 128
ROPE_SLOT = LANES
KV_W = KV_LORA + ROPE_SLOT
QK_W = NOPE_DIM + ROPE_SLOT
SEG = 64
VMEM_LIMIT = 56 * 1024 * 1024
FFN_VMEM_LIMIT = 60 * 1024 * 1024
_MLA_HEAD_COLS = [slice(h * LANES, (h + 1) * LANES) for h in range(MLA_HEADS)]
assert NOPE_DIM == LANES and V_DIM == LANES

_OFF_CQ, _OFF_CKV, _OFF_GB, _OFF_GC, _OFF_XC, _OFF_QSB, _OFF_KSB, _OFF_VSB = (512 * i for i in range(8))
_OFF_KR = 4096
_OFF_KROT = 4096 + ROPE_SLOT
W_IN_PACKED = 4096 + 2 * ROPE_SLOT
_UQ_ROPE = MLA_HEADS * NOPE_DIM
_UQ_ROT = _UQ_ROPE + MLA_HEADS * ROPE_SLOT
W_UQ_PACKED = _UQ_ROT + MLA_HEADS * ROPE_SLOT


def _params(*sem, vmem_limit=VMEM_LIMIT):
    return pltpu.CompilerParams(dimension_semantics=sem, vmem_limit_bytes=vmem_limit)


def _layer_slab(a, layer, n_grid, single_buffer=True):
    zeros = (0,) * (a.ndim - 1)
    index_map = (lambda i: (layer,) + zeros) if n_grid == 1 else (lambda i, j: (layer,) + zeros)
    return pl.BlockSpec((None,) + a.shape[1:], index_map, pipeline_mode=pl.Buffered(1) if single_buffer else None)


def _rms(x, g):
    return x * lax.rsqrt(jnp.mean(x * x, axis=-1, keepdims=True) + EPS) * g


def _dot(a, b):
    return jnp.dot(a, b, preferred_element_type=F32)


def _dot_nt(a, b):
    return lax.dot_general(a, b, (((1,), (1,)), ((), ())), preferred_element_type=F32)


def _lane_tile(x, n):
    return jnp.concatenate([x] * n, axis=1)


def _pick_tile(cap, *sizes):
    t = cap
    while any(s % t for s in sizes):
        t //= 2
    return t


def _ffn_body(x_ref, pre_ref, wg_ref, wu_ref, wd_ref, post_ref, o_ref, xn_ref):
    j = pl.program_id(1)

    @pl.when(j == 0)
    def _():
        xn_ref[...] = _rms(x_ref[...], pre_ref[...]).astype(BF16)
        o_ref[...] = jnp.zeros_like(o_ref)

    xn = xn_ref[...]
    a = jax.nn.silu(_dot(xn, wg_ref[...].astype(BF16))) * _dot(xn, wu_ref[...].astype(BF16))
    o_ref[...] += _dot(a.astype(BF16), wd_ref[...].astype(BF16))

    @pl.when(j == pl.num_programs(1) - 1)
    def _():
        o_ref[...] = x_ref[...] + 0.5 * _rms(o_ref[...], post_ref[...])


def _ffn(x, pre_g, w_gate, w_up, w_down, post_g, layer, tm, tf):
    r, d = x.shape
    dff = w_gate.shape[2]
    return pl.pallas_call(
        _ffn_body,
        grid=(r // tm, dff // tf),
        in_specs=[
            pl.BlockSpec((tm, d), lambda i, j: (i, 0)),
            _layer_slab(pre_g, layer, 2),
            pl.BlockSpec((None, d, tf), lambda i, j: (layer, 0, j)),
            pl.BlockSpec((None, d, tf), lambda i, j: (layer, 0, j)),
            pl.BlockSpec((None, tf, d), lambda i, j: (layer, j, 0)),
            _layer_slab(post_g, layer, 2),
        ],
        out_specs=pl.BlockSpec((tm, d), lambda i, j: (i, 0)),
        out_shape=jax.ShapeDtypeStruct((r, d), F32),
        scratch_shapes=[pltpu.VMEM((tm, d), BF16)],
        compiler_params=_params("parallel", "arbitrary", vmem_limit=FFN_VMEM_LIMIT),
        name="ffn",
    )(x, pre_g, w_gate, w_up, w_down, post_g)


def _proj_shared(h_ref, pre_ref, win_ref, qg_ref, kvg_ref, wuq_ref, cos_ref, sin_ref,
                 ckv_ref, kr_ref, ksb_ref, vsb_ref, gb_ref, uc_ref, qsb_ref, ksbh_ref, vsbh_ref):
    xn = _rms(h_ref[...], pre_ref[...]).astype(BF16)
    cos = cos_ref[...]
    sin = sin_ref[...]

    def col(off, width):
        return _dot_nt(xn, win_ref[off:off + width, :])

    c_kv = _rms(col(_OFF_CKV, KV_LORA), kvg_ref[...])
    ckv_ref[...] = c_kv
    kr_both = col(_OFF_KR, 2 * ROPE_SLOT)
    k_rope = kr_both[:, :ROPE_SLOT] * cos + kr_both[:, ROPE_SLOT:] * sin
    kr_ref[...] = k_rope[:, :ROPE_DIM]

    gb_ref[...] = col(_OFF_GB, CONV_DIM)
    uc_ref[...] = col(_OFF_GC, CONV_DIM) * col(_OFF_XC, CONV_DIM)

    qsb_ref[...] = col(_OFF_QSB, SB_WIDTH).astype(BF16)
    k_sb = col(_OFF_KSB, SB_WIDTH)
    v_sb = col(_OFF_VSB, SB_WIDTH)
    tm = k_sb.shape[0]
    for h, hs in enumerate(_SB_HEAD_COLS):
        ksb_ref[pl.ds(h, tm, stride=SB_HEADS), :] = k_sb[:, hs]
        vsb_ref[pl.ds(h, tm, stride=SB_HEADS), :] = v_sb[:, hs]
    ksbh_ref[...] = k_sb.astype(BF16)
    vsbh_ref[...] = v_sb.astype(BF16)

    c_q = _rms(col(_OFF_CQ, Q_LORA), qg_ref[...]).astype(BF16)
    q_nope = _dot(c_q, wuq_ref[:, :_UQ_ROPE]).astype(BF16)
    q_rope = (_dot(c_q, wuq_ref[:, _UQ_ROPE:_UQ_ROT]) * _lane_tile(cos, MLA_HEADS)
              + _dot(c_q, wuq_ref[:, _UQ_ROT:]) * _lane_tile(sin, MLA_HEADS)).astype(BF16)
    return c_kv.astype(BF16), k_rope.astype(BF16), q_nope, q_rope


def _proj_prompt_body(h_ref, pre_ref, win_ref, qg_ref, kvg_ref, wuq_ref, cos_ref, sin_ref, wuk_ref, wuv_ref,
                      ckv_in, kr_in, ksb_in, vsb_in,
                      ckv_ref, kr_ref, ksb_ref, vsb_ref, gb_ref, uc_ref, qsb_ref, ksbh_ref, vsbh_ref,
                      q_ref, k_ref, v_ref):
    del ckv_in, kr_in, ksb_in, vsb_in
    c_kv, k_rope, q_nope, q_rope = _proj_shared(
        h_ref, pre_ref, win_ref, qg_ref, kvg_ref, wuq_ref, cos_ref, sin_ref,
        ckv_ref, kr_ref, ksb_ref, vsb_ref, gb_ref, uc_ref, qsb_ref, ksbh_ref, vsbh_ref)
    k_nope = _dot(c_kv, wuk_ref[...]).astype(BF16)
    v = _dot(c_kv, wuv_ref[...]).astype(BF16)
    for h, hs in enumerate(_MLA_HEAD_COLS):
        q_ref[h, :, :NOPE_DIM] = q_nope[:, hs]
        q_ref[h, :, NOPE_DIM:] = q_rope[:, hs]
        k_ref[h, :, :NOPE_DIM] = k_nope[:, hs]
        k_ref[h, :, NOPE_DIM:] = k_rope
        v_ref[h] = v[:, hs]


def _proj_sample_body(h_ref, pre_ref, win_ref, qg_ref, kvg_ref, wuq_ref, cos_ref, sin_ref, wukt_ref,
                      ckv_in, kr_in, ksb_in, vsb_in,
                      ckv_ref, kr_ref, ksb_ref, vsb_ref, gb_ref, uc_ref, qsb_ref, ksbh_ref, vsbh_ref,
                      q_ref, kv_ref):
    del ckv_in, kr_in, ksb_in, vsb_in
    c_kv, k_rope, q_nope, q_rope = _proj_shared(
        h_ref, pre_ref, win_ref, qg_ref, kvg_ref, wuq_ref, cos_ref, sin_ref,
        ckv_ref, kr_ref, ksb_ref, vsb_ref, gb_ref, uc_ref, qsb_ref, ksbh_ref, vsbh_ref)
    kv_ref[:, :KV_LORA] = c_kv
    kv_ref[:, KV_LORA:] = k_rope
    for h, hs in enumerate(_MLA_HEAD_COLS):
        q_ref[h, :, :KV_LORA] = _dot(q_nope[:, hs], wukt_ref[h]).astype(BF16)
        q_ref[h, :, KV_LORA:] = q_rope[:, hs]


def _proj(body, h, shared, extra, cos, sin, state, layer, tm, attn_shapes, attn_specs):
    r, d = h.shape
    n_tab = cos.shape[0] // tm
    row = lambda w: pl.BlockSpec((tm, w), lambda i: (i, 0))
    table = pl.BlockSpec((tm, ROPE_SLOT), lambda i: (i % n_tab, 0))
    slab = lambda a: _layer_slab(a, layer, 1)
    state_spec = lambda a: pl.BlockSpec((None, a.shape[1] // (r // tm), a.shape[2]), lambda i: (layer, i, 0))
    sds = jax.ShapeDtypeStruct
    n_in = 1 + len(shared) + 2 + len(extra)
    outs = pl.pallas_call(
        body,
        grid=(r // tm,),
        in_specs=([row(d)] + [slab(a) for a in shared] + [table, table] + [slab(a) for a in extra]
                  + [pl.BlockSpec(memory_space=pl.ANY)] * len(state)),
        out_specs=([state_spec(a) for a in state]
                   + [row(CONV_DIM), row(CONV_DIM), row(SB_WIDTH), row(SB_WIDTH), row(SB_WIDTH)] + attn_specs),
        out_shape=([sds(a.shape, a.dtype) for a in state]
                   + [sds((r, CONV_DIM), F32), sds((r, CONV_DIM), F32), sds((r, SB_WIDTH), BF16),
                      sds((r, SB_WIDTH), BF16), sds((r, SB_WIDTH), BF16)] + attn_shapes),
        input_output_aliases={n_in + k: k for k in range(len(state))},
        compiler_params=_params("parallel"),
        name=body.__name__.strip("_").replace("_body", ""),
    )(h, *shared, cos, sin, *extra, *state)
    return outs[:len(state)], outs[len(state):]


def _proj_prompt(h, shared, w_uk, w_uv, cos, sin, state, layer, tm):
    r = h.shape[0]
    heads3 = lambda w: pl.BlockSpec((MLA_HEADS, tm, w), lambda i: (0, i, 0))
    sds = jax.ShapeDtypeStruct
    return _proj(_proj_prompt_body, h, shared, (w_uk, w_uv), cos, sin, state, layer, tm,
                 [sds((MLA_HEADS, r, QK_W), BF16), sds((MLA_HEADS, r, QK_W), BF16), sds((MLA_HEADS, r, V_DIM), BF16)],
                 [heads3(QK_W), heads3(QK_W), heads3(V_DIM)])


def _proj_sample(h, shared, w_ukt, cos, sin, state, layer, tm):
    r = h.shape[0]
    sds = jax.ShapeDtypeStruct
    return _proj(_proj_sample_body, h, shared, (w_ukt,), cos, sin, state, layer, tm,
                 [sds((MLA_HEADS, r, KV_W), BF16), sds((r, KV_W), BF16)],
                 [pl.BlockSpec((MLA_HEADS, tm, KV_W), lambda i: (0, i, 0)), pl.BlockSpec((tm, KV_W), lambda i: (i, 0))])


def _mla_prompt_body(q_ref, k_ref, v_ref, o_ref, m_ref, l_ref, acc_ref, *, tq, tk):
    qi = pl.program_id(1)
    m_ref[...] = jnp.full_like(m_ref, NEG_INF)
    l_ref[...] = jnp.zeros_like(l_ref)
    acc_ref[...] = jnp.zeros_like(acc_ref)

    def step(kts, masked):
        keys = [pl.ds(pl.multiple_of(kt * tk, tk), tk) for kt in kts]
        if masked:
            qpos = qi * tq + lax.broadcasted_iota(jnp.int32, (tq, 1), 0)
            kpos = [kt * tk + lax.broadcasted_iota(jnp.int32, (1, tk), 1) for kt in kts]
            visible = [(kp >> _CHUNK_SHIFT) <= (qpos >> _CHUNK_SHIFT) for kp in kpos]

        def scores(h):
            s = [_dot_nt(q_ref[h], k_ref[h, kk, :]) * (MLA_SCALE * _LOG2E) for kk in keys]
            return [jnp.where(vis, x, NEG_INF) for vis, x in zip(visible, s)] if masked else s

        s_next = scores(0)
        for h in range(MLA_HEADS):
            s = s_next
            if h + 1 < MLA_HEADS:
                s_next = scores(h + 1)
            rs = slice(h * tq, (h + 1) * tq)
            m_old = m_ref[rs]
            m_new = functools.reduce(jnp.maximum, [m_old] + [jnp.max(x, axis=-1, keepdims=True) for x in s])
            alpha = jnp.exp2(m_old - m_new)
            p = [jnp.exp2(x - _lane_tile(m_new, tk // LANES)) for x in s]
            l_ref[rs] = alpha * l_ref[rs] + sum(x[:, c * LANES:(c + 1) * LANES] for x in p for c in range(tk // LANES))
            acc_ref[rs] = alpha * acc_ref[rs] + sum(_dot(x.astype(BF16), v_ref[h, kk, :]) for x, kk in zip(p, keys))
            m_ref[rs] = m_new

    n_full = (qi * tq) // tk
    lax.fori_loop(0, n_full // 2, lambda i, c: (step([2 * i, 2 * i + 1], False), c)[1], 0)

    @pl.when(n_full % 2 == 1)
    def _():
        step([n_full - 1], False)

    for d in range(-(-tq // tk)):
        step([n_full + d], True)

    for h in range(MLA_HEADS):
        rs = slice(h * tq, (h + 1) * tq)
        o_ref[:, h * V_DIM:(h + 1) * V_DIM] = acc_ref[rs] / jnp.sum(l_ref[rs], axis=-1, keepdims=True)


def _mla_prompt(q, k, v, batch, seq, tq, tk):
    nq = seq // tq
    rows = MLA_HEADS * tq
    whole_batch = lambda w: pl.BlockSpec((MLA_HEADS, seq, w), lambda b, i: (0, b, 0), pipeline_mode=pl.Buffered(1))
    return pl.pallas_call(
        functools.partial(_mla_prompt_body, tq=tq, tk=tk),
        grid=(batch, nq),
        in_specs=[
            pl.BlockSpec((MLA_HEADS, tq, QK_W), lambda b, i: (0, b * nq + i, 0)),
            whole_batch(QK_W), whole_batch(V_DIM),
        ],
        out_specs=pl.BlockSpec((tq, MLA_WIDTH), lambda b, i: (b * nq + i, 0)),
        out_shape=jax.ShapeDtypeStruct((batch * seq, MLA_WIDTH), F32),
        scratch_shapes=[pltpu.VMEM((rows, LANES), F32), pltpu.VMEM((rows, LANES), F32),
                        pltpu.VMEM((rows, V_DIM), F32)],
        compiler_params=_params("parallel", "arbitrary"),
        name="mla_prompt",
    )(q, k, v)


def _mla_up(o, wuv_ref, o_ref, tq):
    for h in range(MLA_HEADS):
        oh = o[h * tq:(h + 1) * tq].astype(BF16)
        o_ref[:, h * V_DIM:(h + 1) * V_DIM] = _dot(oh, wuv_ref[h])


def _mla_sample_body(q_ref, kvn_ref, cc_ref, ckr_ref, wuv_ref, o_ref, *, sq, tk):
    rows = MLA_HEADS * sq
    q = q_ref[...].reshape(rows, KV_W)
    q_lat, q_rope = q[:, :KV_LORA], q[:, KV_LORA:KV_LORA + ROPE_DIM]
    scale = MLA_SCALE * _LOG2E

    def cache_tile(t):
        keys = slice(t * tk, (t + 1) * tk)
        cc = cc_ref[keys, :].astype(BF16)
        return (_dot_nt(q_lat, cc) + _dot(q_rope, ckr_ref[:, keys].astype(BF16))) * scale, cc

    def new_tile():
        kvn = kvn_ref[...]
        return _dot_nt(q, kvn) * scale, kvn[:, :KV_LORA]

    tiles = [functools.partial(cache_tile, t) for t in range(cc_ref.shape[0] // tk)] + [new_tile]
    m = jnp.full((rows, 1), NEG_INF, F32)
    l = jnp.zeros((rows, 1), F32)
    acc = jnp.zeros((rows, KV_LORA), F32)
    nxt = tiles[0]()
    for i in range(len(tiles)):
        s, values = nxt
        if i + 1 < len(tiles):
            nxt = tiles[i + 1]()
        m_new = jnp.maximum(m, jnp.max(s, axis=-1, keepdims=True))
        alpha = jnp.exp2(m - m_new)
        p = jnp.exp2(s - m_new)
        l = alpha * l + jnp.sum(p, axis=-1, keepdims=True)
        acc = alpha * acc + _dot(p.astype(BF16), values)
        m = m_new
    _mla_up(acc / l, wuv_ref, o_ref, sq)


def _mla_sample(q, kv, cache_ckv, cache_kr_t, w_uv, layer, dec_batch, sq):
    past = cache_ckv.shape[2]
    assert past % CHUNK == 0 and sq <= CHUNK
    return pl.pallas_call(
        functools.partial(_mla_sample_body, sq=sq, tk=_pick_tile(512, past)),
        grid=(dec_batch,),
        in_specs=[
            pl.BlockSpec((MLA_HEADS, sq, KV_W), lambda b: (0, b, 0)),
            pl.BlockSpec((sq, KV_W), lambda b: (b, 0)),
            pl.BlockSpec((None, None, past, KV_LORA), lambda b: (layer, b, 0, 0)),
            pl.BlockSpec((None, None, ROPE_DIM, past), lambda b: (layer, b, 0, 0)),
            _layer_slab(w_uv, layer, 1),
        ],
        out_specs=pl.BlockSpec((sq, MLA_WIDTH), lambda b: (b, 0)),
        out_shape=jax.ShapeDtypeStruct((dec_batch * sq, MLA_WIDTH), F32),
        compiler_params=_params("parallel"),
        name="mla_sample",
    )(q, kv, cache_ckv, cache_kr_t, w_uv)


def _suffix_ones(n):
    j = lax.broadcasted_iota(jnp.int32, (n, n), 0)
    k = lax.broadcasted_iota(jnp.int32, (n, n), 1)
    return jnp.where(j > k, -1.0, 0.0).astype(BF16)


def _sb_tiles(qs, ks, vs, carries, ones, causal, right_of=None):
    rows = qs[0].shape[0]
    logits = [_dot_nt(q, k) * (SB_SCALE * _LOG2E) for q, k in zip(qs, ks)]
    softplus, drop, split = [], [], []
    for x in logits:
        sp = jnp.maximum(x, 0.0) + jnp.log(1.0 + jnp.exp2(-jnp.abs(x))) * _LOG2E
        dr = sp if causal is None else jnp.where(causal, sp, 0.0)
        hi = dr.astype(BF16)
        lo = (dr - hi.astype(F32)).astype(BF16)
        softplus.append(sp)
        drop.append(dr)
        split.append(jnp.concatenate([hi, lo], axis=0))
    sums = [_dot(x, ones) for x in split]
    totals = [-jnp.sum(dr, axis=-1, keepdims=True) for dr in drop]
    weights = []
    for i, (x, sp, sm, carry) in enumerate(zip(logits, softplus, sums, carries)):
        if right_of is not None and right_of[i] is not None:
            carry = carry + totals[right_of[i]]
        between = sm[:rows] + sm[rows:] + carry
        w = jnp.exp2((x - sp) + between)
        weights.append(w if causal is None else jnp.where(causal, w, 0.0))
    outs = [_dot(w.astype(BF16), v) for w, v in zip(weights, vs)]
    return outs, totals


def _sb_prompt_body(q_ref, k_ref, v_ref, o_ref, carry_ref, *, t):
    qi = pl.program_id(1)
    ones = _suffix_ones(t)
    causal = lax.broadcasted_iota(jnp.int32, (t, t), 1) < lax.broadcasted_iota(jnp.int32, (t, t), 0)
    o_ref[...] = jnp.zeros_like(o_ref)
    carry_ref[...] = jnp.zeros_like(carry_ref)

    def tiles(kts, mask):
        rows = [pl.ds(pl.multiple_of(kt * t, t), t) for kt in kts]
        chains = [(r, hs) for r in rows for hs in _SB_HEAD_COLS]
        carries = [_lane_tile(carry_ref[:, hs], t // LANES) for _, hs in chains]
        right_of = [None] * SB_HEADS + list(range(len(chains) - SB_HEADS))
        outs, tots = _sb_tiles([q_ref[:, hs] for _, hs in chains], [k_ref[r, hs] for r, hs in chains],
                               [v_ref[r, hs] for r, hs in chains], carries, ones, mask, right_of)
        for (_, hs), out, tot in zip(chains, outs, tots):
            o_ref[:, hs] += out
            carry_ref[:, hs] += tot

    tiles([qi], causal)
    lax.fori_loop(0, qi // 2, lambda i, c: (tiles([qi - 1 - 2 * i, qi - 2 - 2 * i], None), c)[1], 0)

    @pl.when(qi % 2 == 1)
    def _():
        tiles([0], None)


def _sb_prompt(q, k, v, batch, seq, t):
    nq = seq // t
    return pl.pallas_call(
        functools.partial(_sb_prompt_body, t=t),
        grid=(batch, nq),
        in_specs=[
            pl.BlockSpec((t, SB_WIDTH), lambda b, i: (b * nq + i, 0)),
            pl.BlockSpec((seq, SB_WIDTH), lambda b, i: (b, 0)),
            pl.BlockSpec((seq, SB_WIDTH), lambda b, i: (b, 0)),
        ],
        out_specs=pl.BlockSpec((t, SB_WIDTH), lambda b, i: (b * nq + i, 0)),
        out_shape=jax.ShapeDtypeStruct((batch * seq, SB_WIDTH), F32),
        scratch_shapes=[pltpu.VMEM((t, SB_HEADS * LANES), F32)],
        compiler_params=_params("parallel", "arbitrary"),
        name="sb_prompt",
    )(q, k, v)


def _sb_sample_body(q_ref, kn_ref, vn_ref, ck_ref, cv_ref, o_ref, *, sq, past, tk):
    ones_new = _suffix_ones(sq)
    ones = _suffix_ones(tk)
    causal = lax.broadcasted_iota(jnp.int32, (sq, sq), 1) < lax.broadcasted_iota(jnp.int32, (sq, sq), 0)
    qs = [q_ref[:, hs] for hs in _SB_HEAD_COLS]
    accs, carries = _sb_tiles(qs, [kn_ref[:, hs] for hs in _SB_HEAD_COLS], [vn_ref[:, hs] for hs in _SB_HEAD_COLS],
                              [0.0] * SB_HEADS, ones_new, causal)
    n_tiles = past // tk
    group = 2 if n_tiles % 2 == 0 else 1
    for g in reversed(range(n_tiles // group)):
        kts = [g * group + group - 1 - i for i in range(group)]
        rows = [pl.ds(kt * tk * SB_HEADS + h, tk, stride=SB_HEADS) for kt in kts for h in range(SB_HEADS)]
        right_of = [None] * SB_HEADS + list(range(len(rows) - SB_HEADS))
        outs, tots = _sb_tiles(qs * group, [ck_ref[r, :].astype(BF16) for r in rows],
                               [cv_ref[r, :].astype(BF16) for r in rows], carries * group, ones, None, right_of)
        for i in range(len(rows)):
            accs[i % SB_HEADS] = accs[i % SB_HEADS] + outs[i]
            carries[i % SB_HEADS] = carries[i % SB_HEADS] + tots[i]
    for hs, acc in zip(_SB_HEAD_COLS, accs):
        o_ref[:, hs] = acc


def _sb_sample(q, k, v, cache_k, cache_v, layer, dec_batch, sq):
    past = cache_k.shape[2] // SB_HEADS
    new = lambda: pl.BlockSpec((sq, SB_WIDTH), lambda b: (b, 0))
    old = lambda: pl.BlockSpec((None, None, past * SB_HEADS, SB_DIM), lambda b: (layer, b, 0, 0))
    return pl.pallas_call(
        functools.partial(_sb_sample_body, sq=sq, past=past, tk=_pick_tile(256, past)),
        grid=(dec_batch,),
        in_specs=[new(), new(), new(), old(), old()],
        out_specs=pl.BlockSpec((sq, SB_WIDTH), lambda b: (b, 0)),
        out_shape=jax.ShapeDtypeStruct((dec_batch * sq, SB_WIDTH), F32),
        compiler_params=_params("parallel"),
        name="sb_sample",
    )(q, k, v, cache_k, cache_v)


def _merge_body(h_ref, ya_ref, gb_ref, uc_ref, prev_ref, yc_ref, cw_ref, ga_ref, gconv_ref, gc_ref,
                wo_ref, post_ref, o_ref, yb_ref, *, tm):
    ya = _rms(ya_ref[...], ga_ref[...]).astype(BF16)
    m = _dot(ya, wo_ref[:MLA_WIDTH, :])
    yc = _rms(yc_ref[...], gc_ref[...]).astype(BF16)
    m += _dot(yc, wo_ref[MLA_WIDTH + CONV_DIM:, :])
    cw = cw_ref[...]
    row = lax.broadcasted_iota(jnp.int32, (SEG, 1), 0)
    for s in range(tm // SEG):
        rows = slice(s * SEG, (s + 1) * SEG)
        u = uc_ref[rows, :]
        prev = prev_ref[s]
        u1 = jnp.where(row == 0, prev[1:2], pltpu.roll(u, 1, 0))
        u2 = jnp.where(row == 0, prev[0:1], jnp.where(row == 1, prev[1:2], pltpu.roll(u, 2, 0)))
        conv = cw[0:1] * u2 + cw[1:2] * u1 + cw[2:3] * u
        yb_ref[rows, :] = _rms(gb_ref[rows, :] * conv, gconv_ref[...]).astype(BF16)
    m += _dot(yb_ref[...], wo_ref[MLA_WIDTH:MLA_WIDTH + CONV_DIM, :])
    o_ref[...] = h_ref[...] + _rms(m, post_ref[...])


def _merge(h, ya, gb, uc, prev, yc, conv_w, ga, gconv, gc, w_o, post_g, layer, tm):
    r, d = h.shape
    row = lambda w: pl.BlockSpec((tm, w), lambda i: (i, 0))
    slab = lambda a: _layer_slab(a, layer, 1)
    return pl.pallas_call(
        functools.partial(_merge_body, tm=tm),
        grid=(r // tm,),
        in_specs=[
            row(d), row(MLA_WIDTH), row(CONV_DIM), row(CONV_DIM),
            pl.BlockSpec((tm // SEG, CONV_W - 1, CONV_DIM), lambda i: (i, 0, 0)),
            row(SB_WIDTH), slab(conv_w), slab(ga), slab(gconv), slab(gc), slab(w_o), slab(post_g),
        ],
        out_specs=row(d),
        out_shape=jax.ShapeDtypeStruct((r, d), F32),
        scratch_shapes=[pltpu.VMEM((tm, CONV_DIM), BF16)],
        compiler_params=_params("parallel"),
        name="merge",
    )(h, ya, gb, uc, prev, yc, conv_w, ga, gconv, gc, w_o, post_g)


def _rot_half(w):
    half = ROPE_DIM // 2
    return jnp.concatenate([-w[..., half:], w[..., :half]], axis=-1)


def _pad_slot(w):
    return jnp.pad(w, [(0, 0)] * (w.ndim - 1) + [(0, ROPE_SLOT - w.shape[-1])])


def _pack_w_in(w):
    wt = jnp.swapaxes(w, 1, 2)
    kr0 = Q_LORA + KV_LORA
    half = ROPE_DIM // 2
    k_rope = wt[:, kr0:kr0 + ROPE_DIM]
    rot = jnp.concatenate([-k_rope[:, half:], k_rope[:, :half]], axis=1)
    pad = lambda a: jnp.pad(a, ((0, 0), (0, ROPE_SLOT - ROPE_DIM), (0, 0)))
    return jnp.concatenate([wt[:, :kr0], wt[:, kr0 + ROPE_DIM:], pad(k_rope), pad(rot)], axis=1).astype(BF16)


def _pack_w_uq(w):
    depth = w.shape[0]
    w = w.reshape(depth, Q_LORA, MLA_HEADS, NOPE_DIM + ROPE_DIM)
    flat = lambda a: a.reshape(depth, Q_LORA, -1)
    rope = w[..., NOPE_DIM:]
    return jnp.concatenate([flat(w[..., :NOPE_DIM]), flat(_pad_slot(rope)), flat(_pad_slot(_rot_half(rope)))],
                           axis=-1).astype(BF16)


def _rope_tables(pos):
    half = ROPE_DIM // 2
    inv_freq = ROPE_THETA ** (-jnp.arange(half, dtype=F32) / half)
    ang = pos.astype(F32)[:, None] * inv_freq[None, :]
    pad = lambda t: jnp.pad(jnp.concatenate([t, t], axis=1), ((0, 0), (0, ROPE_SLOT - ROPE_DIM)))
    return pad(jnp.cos(ang)), pad(jnp.sin(ang))


def kernel(x_prompt, x_sample, cache_mla_ckv, cache_mla_krope, cache_sb_k, cache_sb_v, state_conv, ffn1_pre_g, ffn1_w_gate, ffn1_w_up, ffn1_w_down, ffn1_post_g, mix_pre_g, w_in, mla_q_norm_g, mla_kv_norm_g, mla_w_uq, mla_w_uk, mla_w_uv, conv_w, out_norm_mla_g, out_norm_conv_g, out_norm_sb_g, w_o, mix_post_g, ffn2_pre_g, ffn2_w_gate, ffn2_w_up, ffn2_w_down, ffn2_post_g):
    batch, seq, d = x_prompt.shape
    dec_batch, sq, _ = x_sample.shape
    depth, _, past, _ = cache_mla_ckv.shape
    rp, rs = batch * seq, dec_batch * sq
    assert d == D_MODEL and sq == SEG and seq % SEG == 0

    tm_ffn = _pick_tile(1024, rp, rs)
    tf_ffn = _pick_tile(256, ffn1_w_gate.shape[2])
    tm = _pick_tile(256, seq, rs)
    t_attn = _pick_tile(256, seq)

    cos_p, sin_p = _rope_tables(jnp.arange(seq, dtype=jnp.int32))
    cos_s, sin_s = _rope_tables(jnp.tile(past + jnp.arange(sq, dtype=jnp.int32), tm // sq))
    cache_kr_t = jnp.swapaxes(cache_mla_krope, 2, 3)
    cache_k = cache_sb_k.reshape(depth, dec_batch, past * SB_HEADS, SB_DIM)
    cache_v = cache_sb_v.reshape(depth, dec_batch, past * SB_HEADS, SB_DIM)
    gain = lambda g: g.reshape(depth, 1, -1)

    shared = (gain(mix_pre_g), _pack_w_in(w_in), gain(mla_q_norm_g), gain(mla_kv_norm_g), _pack_w_uq(mla_w_uq))
    w_ukt = jnp.transpose(mla_w_uk, (0, 2, 3, 1)).astype(BF16)
    w_uv = jnp.transpose(mla_w_uv, (0, 2, 1, 3)).astype(BF16)
    w_uk_flat = mla_w_uk.reshape(depth, KV_LORA, MLA_HEADS * NOPE_DIM).astype(BF16)
    w_uv_flat = mla_w_uv.reshape(depth, KV_LORA, MLA_WIDTH).astype(BF16)
    merge_w = (conv_w, gain(out_norm_mla_g), gain(out_norm_conv_g), gain(out_norm_sb_g), w_o.astype(BF16),
               gain(mix_post_g))

    def new_state(rows):
        return (jnp.zeros((depth, rows, KV_LORA), F32), jnp.zeros((depth, rows, ROPE_DIM), F32),
                jnp.zeros((depth, rows * SB_HEADS, SB_DIM), F32), jnp.zeros((depth, rows * SB_HEADS, SB_DIM), F32))

    h_p, h_s = x_prompt.reshape(rp, d), x_sample.reshape(rs, d)
    state_p, state_s = new_state(rp), new_state(rs)
    conv_p, conv_s = [], []
    for l in range(depth):
        ffn1 = lambda x: _ffn(x, gain(ffn1_pre_g), ffn1_w_gate, ffn1_w_up, ffn1_w_down, gain(ffn1_post_g), l,
                              tm_ffn, tf_ffn)
        ffn2 = lambda x: _ffn(x, gain(ffn2_pre_g), ffn2_w_gate, ffn2_w_up, ffn2_w_down, gain(ffn2_post_g), l,
                              tm_ffn, tf_ffn)
        h_p, h_s = ffn1(h_p), ffn1(h_s)

        state_p, (gb, uc, qsb, ksbh, vsbh, q, k, v) = _proj_prompt(h_p, shared, w_uk_flat, w_uv_flat, cos_p, sin_p,
                                                                  state_p, l, tm)
        ya = _mla_prompt(q, k, v, batch, seq, t_attn, _pick_tile(2 * t_attn, seq))
        yc = _sb_prompt(qsb, ksbh, vsbh, batch, seq, t_attn)
        tails = uc.reshape(batch, seq // SEG, SEG, CONV_DIM)[:, :, SEG - (CONV_W - 1):]
        prev = jnp.concatenate([jnp.zeros_like(tails[:, :1]), tails[:, :-1]], axis=1)
        h_p = _merge(h_p, ya, gb, uc, prev.reshape(-1, CONV_W - 1, CONV_DIM), yc, *merge_w, l, tm)
        conv_p.append(tails[:, -1])

        state_s, (gb, uc, qsb, ksbh, vsbh, q, kv) = _proj_sample(h_s, shared, w_ukt, cos_s, sin_s, state_s, l, tm)
        ya = _mla_sample(q, kv, cache_mla_ckv, cache_kr_t, w_uv, l, dec_batch, sq)
        yc = _sb_sample(qsb, ksbh, vsbh, cache_k, cache_v, l, dec_batch, sq)
        h_s = _merge(h_s, ya, gb, uc, state_conv[l], yc, *merge_w, l, tm)
        conv_s.append(uc.reshape(dec_batch, sq, CONV_DIM)[:, sq - (CONV_W - 1):])

        h_p, h_s = ffn2(h_p), ffn2(h_s)

    def state_out(state, lead):
        ckv, kr, sbk, sbv = state
        return (ckv.reshape(depth, *lead, KV_LORA), kr.reshape(depth, *lead, ROPE_DIM),
                sbk.reshape(depth, *lead, SB_HEADS, SB_DIM), sbv.reshape(depth, *lead, SB_HEADS, SB_DIM))

    return (h_p.reshape(batch, seq, d), h_s.reshape(dec_batch, sq, d),
            *state_out(state_p, (batch, seq)), jnp.stack(conv_p, axis=0),
            *state_out(state_s, (dec_batch, sq)), jnp.stack(conv_s, axis=0))
```

```python
import functools

import jax
import jax.numpy as jnp
from jax import lax
from jax.experimental import pallas as pl
from jax.experimental.pallas import tpu as pltpu

F32 = jnp.float32
BF16 = jnp.bfloat16

D_MODEL = 2048
CHUNK = 64
EPS = 1e-6
MLA_HEADS = 8
Q_LORA = 512
KV_LORA = 512
NOPE_DIM = 128
ROPE_DIM = 64
V_DIM = 128
ROPE_THETA = 10000.0
MLA_WIDTH = MLA_HEADS * V_DIM
CONV_DIM = 512
CONV_W = 3
SB_HEADS = 4
SB_DIM = 128
SB_WIDTH = SB_HEADS * SB_DIM
_SB_HEAD_COLS = [slice(h * SB_DIM, (h + 1) * SB_DIM) for h in range(SB_HEADS)]
MLA_SCALE = (NOPE_DIM + ROPE_DIM) ** -0.5
SB_SCALE = SB_DIM ** -0.5
NEG_INF = -1e30
_LOG2E = 1.4426950408889634
_CHUNK_SHIFT = CHUNK.bit_length() - 1
assert 1 << _CHUNK_SHIFT == CHUNK

LANES = 128
ROPE_SLOT = LANES
KV_W = KV_LORA + ROPE_SLOT
QK_W = NOPE_DIM + ROPE_SLOT
SEG = 64
VMEM_LIMIT = 56 * 1024 * 1024
FFN_VMEM_LIMIT = 60 * 1024 * 1024
_MLA_HEAD_COLS = [slice(h * LANES, (h + 1) * LANES) for h in range(MLA_HEADS)]
assert NOPE_DIM == LANES and V_DIM == LANES

_OFF_CQ, _OFF_CKV, _OFF_KROPE = 0, Q_LORA, Q_LORA + KV_LORA
_OFF_GB, _OFF_GC, _OFF_XC, _OFF_QSB, _OFF_KSB, _OFF_VSB = (_OFF_KROPE + ROPE_DIM + 512 * i for i in range(6))
assert all(off % 16 == 0 for off in (_OFF_GB, _OFF_GC, _OFF_XC, _OFF_QSB, _OFF_KSB, _OFF_VSB))
_UQ_ROPE = MLA_HEADS * NOPE_DIM
_UQ_ROT = _UQ_ROPE + MLA_HEADS * ROPE_SLOT
W_UQ_PACKED = _UQ_ROT + MLA_HEADS * ROPE_SLOT


def _params(*sem, vmem_limit=VMEM_LIMIT):
    return pltpu.CompilerParams(dimension_semantics=sem, vmem_limit_bytes=vmem_limit)


def _layer_slab(a, layer, n_grid, single_buffer=True):
    zeros = (0,) * (a.ndim - 1)
    index_map = (lambda i: (layer,) + zeros) if n_grid == 1 else (lambda i, j: (layer,) + zeros)
    return pl.BlockSpec((None,) + a.shape[1:], index_map, pipeline_mode=pl.Buffered(1) if single_buffer else None)


def _rms(x, g):
    return x * lax.rsqrt(jnp.mean(x * x, axis=-1, keepdims=True) + EPS) * g


def _dot(a, b):
    return jnp.dot(a, b, preferred_element_type=F32)


def _dot_nt(a, b):
    return lax.dot_general(a, b, (((1,), (1,)), ((), ())), preferred_element_type=F32)


def _lane_tile(x, n):
    return jnp.concatenate([x] * n, axis=1)


def _pick_tile(cap, *sizes):
    t = cap
    while any(s % t for s in sizes):
        t //= 2
    return t


def _ffn_body(x_ref, pre_ref, wg_ref, wu_ref, wd_ref, post_ref, o_ref, xn_ref):
    j = pl.program_id(1)

    @pl.when(j == 0)
    def _():
        xn_ref[...] = _rms(x_ref[...], pre_ref[...]).astype(BF16)
        o_ref[...] = jnp.zeros_like(o_ref)

    xn = xn_ref[...]
    a = jax.nn.silu(_dot(xn, wg_ref[...].astype(BF16))) * _dot(xn, wu_ref[...].astype(BF16))
    o_ref[...] += _dot(a.astype(BF16), wd_ref[...].astype(BF16))

    @pl.when(j == pl.num_programs(1) - 1)
    def _():
        o_ref[...] = x_ref[...] + 0.5 * _rms(o_ref[...], post_ref[...])


def _ffn(x, pre_g, w_gate, w_up, w_down, post_g, layer, tm, tf):
    r, d = x.shape
    dff = w_gate.shape[2]
    return pl.pallas_call(
        _ffn_body,
        grid=(r // tm, dff // tf),
        in_specs=[
            pl.BlockSpec((tm, d), lambda i, j: (i, 0)),
            _layer_slab(pre_g, layer, 2),
            pl.BlockSpec((None, d, tf), lambda i, j: (layer, 0, j)),
            pl.BlockSpec((None, d, tf), lambda i, j: (layer, 0, j)),
            pl.BlockSpec((None, tf, d), lambda i, j: (layer, j, 0)),
            _layer_slab(post_g, layer, 2),
        ],
        out_specs=pl.BlockSpec((tm, d), lambda i, j: (i, 0)),
        out_shape=jax.ShapeDtypeStruct((r, d), F32),
        scratch_shapes=[pltpu.VMEM((tm, d), BF16)],
        compiler_params=_params("parallel", "arbitrary", vmem_limit=FFN_VMEM_LIMIT),
        name="ffn",
    )(x, pre_g, w_gate, w_up, w_down, post_g)


def _proj_shared(h_ref, pre_ref, win_ref, wkr_ref, qg_ref, kvg_ref, wuq_ref, cos_ref, sin_ref,
                 ckv_ref, kr_ref, ksb_ref, vsb_ref, gb_ref, uc_ref, qsb_ref, ksbh_ref, vsbh_ref):
    xn = _rms(h_ref[...], pre_ref[...]).astype(BF16)
    cos = cos_ref[...]
    sin = sin_ref[...]

    def col(off, width):
        return _dot_nt(xn, win_ref[off:off + width, :])

    c_kv = _rms(col(_OFF_CKV, KV_LORA), kvg_ref[...])
    ckv_ref[...] = c_kv
    kr_both = _dot_nt(xn, wkr_ref[...])
    k_rope = kr_both[:, :ROPE_SLOT] * cos + kr_both[:, ROPE_SLOT:] * sin
    kr_ref[...] = k_rope[:, :ROPE_DIM]

    gb_ref[...] = col(_OFF_GB, CONV_DIM)
    uc_ref[...] = col(_OFF_GC, CONV_DIM) * col(_OFF_XC, CONV_DIM)

    qsb_ref[...] = col(_OFF_QSB, SB_WIDTH).astype(BF16)
    k_sb = col(_OFF_KSB, SB_WIDTH)
    v_sb = col(_OFF_VSB, SB_WIDTH)
    tm = k_sb.shape[0]
    for h, hs in enumerate(_SB_HEAD_COLS):
        ksb_ref[pl.ds(h, tm, stride=SB_HEADS), :] = k_sb[:, hs]
        vsb_ref[pl.ds(h, tm, stride=SB_HEADS), :] = v_sb[:, hs]
    ksbh_ref[...] = k_sb.astype(BF16)
    vsbh_ref[...] = v_sb.astype(BF16)

    c_q = _rms(col(_OFF_CQ, Q_LORA), qg_ref[...]).astype(BF16)
    q_nope = _dot(c_q, wuq_ref[:, :_UQ_ROPE]).astype(BF16)
    q_rope = (_dot(c_q, wuq_ref[:, _UQ_ROPE:_UQ_ROT]) * _lane_tile(cos, MLA_HEADS)
              + _dot(c_q, wuq_ref[:, _UQ_ROT:]) * _lane_tile(sin, MLA_HEADS)).astype(BF16)
    return c_kv.astype(BF16), k_rope.astype(BF16), q_nope, q_rope


def _proj_prompt_body(h_ref, pre_ref, win_ref, wkr_ref, qg_ref, kvg_ref, wuq_ref, cos_ref, sin_ref, wuk_ref, wuv_ref,
                      ckv_in, kr_in, ksb_in, vsb_in,
                      ckv_ref, kr_ref, ksb_ref, vsb_ref, gb_ref, uc_ref, qsb_ref, ksbh_ref, vsbh_ref,
                      q_ref, k_ref, v_ref):
    del ckv_in, kr_in, ksb_in, vsb_in
    c_kv, k_rope, q_nope, q_rope = _proj_shared(
        h_ref, pre_ref, win_ref, wkr_ref, qg_ref, kvg_ref, wuq_ref, cos_ref, sin_ref,
        ckv_ref, kr_ref, ksb_ref, vsb_ref, gb_ref, uc_ref, qsb_ref, ksbh_ref, vsbh_ref)
    k_nope = _dot(c_kv, wuk_ref[...]).astype(BF16)
    v = _dot(c_kv, wuv_ref[...]).astype(BF16)
    for h, hs in enumerate(_MLA_HEAD_COLS):
        q_ref[h, :, :NOPE_DIM] = q_nope[:, hs]
        q_ref[h, :, NOPE_DIM:] = q_rope[:, hs]
        k_ref[h, :, :NOPE_DIM] = k_nope[:, hs]
        k_ref[h, :, NOPE_DIM:] = k_rope
        v_ref[h] = v[:, hs]


def _proj_sample_body(h_ref, pre_ref, win_ref, wkr_ref, qg_ref, kvg_ref, wuq_ref, cos_ref, sin_ref, wukt_ref,
                      ckv_in, kr_in, ksb_in, vsb_in,
                      ckv_ref, kr_ref, ksb_ref, vsb_ref, gb_ref, uc_ref, qsb_ref, ksbh_ref, vsbh_ref,
                      q_ref, kv_ref):
    del ckv_in, kr_in, ksb_in, vsb_in
    c_kv, k_rope, q_nope, q_rope = _proj_shared(
        h_ref, pre_ref, win_ref, wkr_ref, qg_ref, kvg_ref, wuq_ref, cos_ref, sin_ref,
        ckv_ref, kr_ref, ksb_ref, vsb_ref, gb_ref, uc_ref, qsb_ref, ksbh_ref, vsbh_ref)
    kv_ref[:, :KV_LORA] = c_kv
    kv_ref[:, KV_LORA:] = k_rope
    for h, hs in enumerate(_MLA_HEAD_COLS):
        q_ref[h, :, :KV_LORA] = _dot(q_nope[:, hs], wukt_ref[h]).astype(BF16)
        q_ref[h, :, KV_LORA:] = q_rope[:, hs]


def _proj(body, h, shared, extra, cos, sin, state, layer, tm, attn_shapes, attn_specs):
    r, d = h.shape
    n_tab = cos.shape[0] // tm
    row = lambda w: pl.BlockSpec((tm, w), lambda i: (i, 0))
    table = pl.BlockSpec((tm, ROPE_SLOT), lambda i: (i % n_tab, 0))
    slab = lambda a: _layer_slab(a, layer, 1)
    state_spec = lambda a: pl.BlockSpec((None, a.shape[1] // (r // tm), a.shape[2]), lambda i: (layer, i, 0))
    sds = jax.ShapeDtypeStruct
    n_in = 1 + len(shared) + 2 + len(extra)
    outs = pl.pallas_call(
        body,
        grid=(r // tm,),
        in_specs=([row(d)] + [slab(a) for a in shared] + [table, table] + [slab(a) for a in extra]
                  + [pl.BlockSpec(memory_space=pl.ANY)] * len(state)),
        out_specs=([state_spec(a) for a in state]
                   + [row(CONV_DIM), row(CONV_DIM), row(SB_WIDTH), row(SB_WIDTH), row(SB_WIDTH)] + attn_specs),
        out_shape=([sds(a.shape, a.dtype) for a in state]
                   + [sds((r, CONV_DIM), F32), sds((r, CONV_DIM), F32), sds((r, SB_WIDTH), BF16),
                      sds((r, SB_WIDTH), BF16), sds((r, SB_WIDTH), BF16)] + attn_shapes),
        input_output_aliases={n_in + k: k for k in range(len(state))},
        compiler_params=_params("parallel"),
        name=body.__name__.strip("_").replace("_body", ""),
    )(h, *shared, cos, sin, *extra, *state)
    return outs[:len(state)], outs[len(state):]


def _proj_prompt(h, shared, w_uk, w_uv, cos, sin, state, layer, tm):
    r = h.shape[0]
    heads3 = lambda w: pl.BlockSpec((MLA_HEADS, tm, w), lambda i: (0, i, 0))
    sds = jax.ShapeDtypeStruct
    return _proj(_proj_prompt_body, h, shared, (w_uk, w_uv), cos, sin, state, layer, tm,
                 [sds((MLA_HEADS, r, QK_W), BF16), sds((MLA_HEADS, r, QK_W), BF16), sds((MLA_HEADS, r, V_DIM), BF16)],
                 [heads3(QK_W), heads3(QK_W), heads3(V_DIM)])


def _proj_sample(h, shared, w_ukt, cos, sin, state, layer, tm):
    r = h.shape[0]
    sds = jax.ShapeDtypeStruct
    return _proj(_proj_sample_body, h, shared, (w_ukt,), cos, sin, state, layer, tm,
                 [sds((MLA_HEADS, r, KV_W), BF16), sds((r, KV_W), BF16)],
                 [pl.BlockSpec((MLA_HEADS, tm, KV_W), lambda i: (0, i, 0)), pl.BlockSpec((tm, KV_W), lambda i: (i, 0))])


def _mla_prompt_body(q_ref, k_ref, v_ref, o_ref, m_ref, l_ref, acc_ref, *, tq, tk):
    qi = pl.program_id(1)
    m_ref[...] = jnp.full_like(m_ref, NEG_INF)
    l_ref[...] = jnp.zeros_like(l_ref)
    acc_ref[...] = jnp.zeros_like(acc_ref)

    def step(kts, masked):
        keys = [pl.ds(pl.multiple_of(kt * tk, tk), tk) for kt in kts]
        if masked:
            qpos = qi * tq + lax.broadcasted_iota(jnp.int32, (tq, 1), 0)
            kpos = [kt * tk + lax.broadcasted_iota(jnp.int32, (1, tk), 1) for kt in kts]
            visible = [(kp >> _CHUNK_SHIFT) <= (qpos >> _CHUNK_SHIFT) for kp in kpos]

        def scores(h):
            s = [_dot_nt(q_ref[h], k_ref[h, kk, :]) * (MLA_SCALE * _LOG2E) for kk in keys]
            return [jnp.where(vis, x, NEG_INF) for vis, x in zip(visible, s)] if masked else s

        s_next = scores(0)
        for h in range(MLA_HEADS):
            s = s_next
            if h + 1 < MLA_HEADS:
                s_next = scores(h + 1)
            rs = slice(h * tq, (h + 1) * tq)
            m_old = m_ref[rs]
            m_new = functools.reduce(jnp.maximum, [m_old] + [jnp.max(x, axis=-1, keepdims=True) for x in s])
            alpha = jnp.exp2(m_old - m_new)
            p = [jnp.exp2(x - _lane_tile(m_new, tk // LANES)) for x in s]
            l_ref[rs] = alpha * l_ref[rs] + sum(x[:, c * LANES:(c + 1) * LANES] for x in p for c in range(tk // LANES))
            acc_ref[rs] = alpha * acc_ref[rs] + sum(_dot(x.astype(BF16), v_ref[h, kk, :]) for x, kk in zip(p, keys))
            m_ref[rs] = m_new

    n_full = (qi * tq) // tk
    lax.fori_loop(0, n_full // 2, lambda i, c: (step([2 * i, 2 * i + 1], False), c)[1], 0)

    @pl.when(n_full % 2 == 1)
    def _():
        step([n_full - 1], False)

    for d in range(-(-tq // tk)):
        step([n_full + d], True)

    for h in range(MLA_HEADS):
        rs = slice(h * tq, (h + 1) * tq)
        o_ref[:, h * V_DIM:(h + 1) * V_DIM] = acc_ref[rs] / jnp.sum(l_ref[rs], axis=-1, keepdims=True)


def _mla_prompt(q, k, v, batch, seq, tq, tk):
    nq = seq // tq
    rows = MLA_HEADS * tq
    whole_batch = lambda w: pl.BlockSpec((MLA_HEADS, seq, w), lambda b, i: (0, b, 0), pipeline_mode=pl.Buffered(1))
    return pl.pallas_call(
        functools.partial(_mla_prompt_body, tq=tq, tk=tk),
        grid=(batch, nq),
        in_specs=[
            pl.BlockSpec((MLA_HEADS, tq, QK_W), lambda b, i: (0, b * nq + i, 0)),
            whole_batch(QK_W), whole_batch(V_DIM),
        ],
        out_specs=pl.BlockSpec((tq, MLA_WIDTH), lambda b, i: (b * nq + i, 0)),
        out_shape=jax.ShapeDtypeStruct((batch * seq, MLA_WIDTH), F32),
        scratch_shapes=[pltpu.VMEM((rows, LANES), F32), pltpu.VMEM((rows, LANES), F32),
                        pltpu.VMEM((rows, V_DIM), F32)],
        compiler_params=_params("parallel", "arbitrary"),
        name="mla_prompt",
    )(q, k, v)


def _mla_up(o, wuv_ref, o_ref, tq):
    for h in range(MLA_HEADS):
        oh = o[h * tq:(h + 1) * tq].astype(BF16)
        o_ref[:, h * V_DIM:(h + 1) * V_DIM] = _dot(oh, wuv_ref[h])


def _mla_sample_body(q_ref, kvn_ref, cc_ref, ckr_ref, wuv_ref, o_ref, *, sq, tk):
    rows = MLA_HEADS * sq
    q = q_ref[...].reshape(rows, KV_W)
    q_lat, q_rope = q[:, :KV_LORA], q[:, KV_LORA:KV_LORA + ROPE_DIM]
    scale = MLA_SCALE * _LOG2E

    def cache_tile(t):
        keys = slice(t * tk, (t + 1) * tk)
        cc = cc_ref[keys, :].astype(BF16)
        return (_dot_nt(q_lat, cc) + _dot(q_rope, ckr_ref[:, keys].astype(BF16))) * scale, cc

    def new_tile():
        kvn = kvn_ref[...]
        return _dot_nt(q, kvn) * scale, kvn[:, :KV_LORA]

    tiles = [functools.partial(cache_tile, t) for t in range(cc_ref.shape[0] // tk)] + [new_tile]
    m = jnp.full((rows, 1), NEG_INF, F32)
    l = jnp.zeros((rows, 1), F32)
    acc = jnp.zeros((rows, KV_LORA), F32)
    nxt = tiles[0]()
    for i in range(len(tiles)):
        s, values = nxt
        if i + 1 < len(tiles):
            nxt = tiles[i + 1]()
        m_new = jnp.maximum(m, jnp.max(s, axis=-1, keepdims=True))
        alpha = jnp.exp2(m - m_new)
        p = jnp.exp2(s - m_new)
        l = alpha * l + jnp.sum(p, axis=-1, keepdims=True)
        acc = alpha * acc + _dot(p.astype(BF16), values)
        m = m_new
    _mla_up(acc / l, wuv_ref, o_ref, sq)


def _mla_sample(q, kv, cache_ckv, cache_kr_t, w_uv, layer, dec_batch, sq):
    past = cache_ckv.shape[2]
    assert past % CHUNK == 0 and sq <= CHUNK
    return pl.pallas_call(
        functools.partial(_mla_sample_body, sq=sq, tk=_pick_tile(512, past)),
        grid=(dec_batch,),
        in_specs=[
            pl.BlockSpec((MLA_HEADS, sq, KV_W), lambda b: (0, b, 0)),
            pl.BlockSpec((sq, KV_W), lambda b: (b, 0)),
            pl.BlockSpec((None, None, past, KV_LORA), lambda b: (layer, b, 0, 0)),
            pl.BlockSpec((None, None, ROPE_DIM, past), lambda b: (layer, b, 0, 0)),
            _layer_slab(w_uv, layer, 1),
        ],
        out_specs=pl.BlockSpec((sq, MLA_WIDTH), lambda b: (b, 0)),
        out_shape=jax.ShapeDtypeStruct((dec_batch * sq, MLA_WIDTH), F32),
        compiler_params=_params("parallel"),
        name="mla_sample",
    )(q, kv, cache_ckv, cache_kr_t, w_uv)


def _suffix_ones(n):
    j = lax.broadcasted_iota(jnp.int32, (n, n), 0)
    k = lax.broadcasted_iota(jnp.int32, (n, n), 1)
    return jnp.where(j > k, -1.0, 0.0).astype(BF16)


def _sb_tiles(qs, ks, vs, carries, ones, causal, right_of=None):
    rows = qs[0].shape[0]
    logits = [_dot_nt(q, k) * (SB_SCALE * _LOG2E) for q, k in zip(qs, ks)]
    softplus, drop, split = [], [], []
    for x in logits:
        sp = jnp.maximum(x, 0.0) + jnp.log(1.0 + jnp.exp2(-jnp.abs(x))) * _LOG2E
        dr = sp if causal is None else jnp.where(causal, sp, 0.0)
        hi = dr.astype(BF16)
        lo = (dr - hi.astype(F32)).astype(BF16)
        softplus.append(sp)
        drop.append(dr)
        split.append(jnp.concatenate([hi, lo], axis=0))
    sums = [_dot(x, ones) for x in split]
    totals = [-jnp.sum(dr, axis=-1, keepdims=True) for dr in drop]
    weights = []
    for i, (x, sp, sm, carry) in enumerate(zip(logits, softplus, sums, carries)):
        if right_of is not None and right_of[i] is not None:
            carry = carry + totals[right_of[i]]
        between = sm[:rows] + sm[rows:] + carry
        w = jnp.exp2((x - sp) + between)
        weights.append(w if causal is None else jnp.where(causal, w, 0.0))
    outs = [_dot(w.astype(BF16), v) for w, v in zip(weights, vs)]
    return outs, totals


def _sb_prompt_body(q_ref, k_ref, v_ref, o_ref, carry_ref, *, t):
    qi = pl.program_id(1)
    ones = _suffix_ones(t)
    causal = lax.broadcasted_iota(jnp.int32, (t, t), 1) < lax.broadcasted_iota(jnp.int32, (t, t), 0)
    o_ref[...] = jnp.zeros_like(o_ref)
    carry_ref[...] = jnp.zeros_like(carry_ref)

    def tiles(kts, mask):
        rows = [pl.ds(pl.multiple_of(kt * t, t), t) for kt in kts]
        chains = [(r, hs) for r in rows for hs in _SB_HEAD_COLS]
        carries = [_lane_tile(carry_ref[:, hs], t // LANES) for _, hs in chains]
        right_of = [None] * SB_HEADS + list(range(len(chains) - SB_HEADS))
        outs, tots = _sb_tiles([q_ref[:, hs] for _, hs in chains], [k_ref[r, hs] for r, hs in chains],
                               [v_ref[r, hs] for r, hs in chains], carries, ones, mask, right_of)
        for (_, hs), out, tot in zip(chains, outs, tots):
            o_ref[:, hs] += out
            carry_ref[:, hs] += tot

    tiles([qi], causal)
    lax.fori_loop(0, qi // 2, lambda i, c: (tiles([qi - 1 - 2 * i, qi - 2 - 2 * i], None), c)[1], 0)

    @pl.when(qi % 2 == 1)
    def _():
        tiles([0], None)


def _sb_prompt(q, k, v, batch, seq, t):
    nq = seq // t
    return pl.pallas_call(
        functools.partial(_sb_prompt_body, t=t),
        grid=(batch, nq),
        in_specs=[
            pl.BlockSpec((t, SB_WIDTH), lambda b, i: (b * nq + i, 0)),
            pl.BlockSpec((seq, SB_WIDTH), lambda b, i: (b, 0)),
            pl.BlockSpec((seq, SB_WIDTH), lambda b, i: (b, 0)),
        ],
        out_specs=pl.BlockSpec((t, SB_WIDTH), lambda b, i: (b * nq + i, 0)),
        out_shape=jax.ShapeDtypeStruct((batch * seq, SB_WIDTH), F32),
        scratch_shapes=[pltpu.VMEM((t, SB_HEADS * LANES), F32)],
        compiler_params=_params("parallel", "arbitrary"),
        name="sb_prompt",
    )(q, k, v)


def _sb_sample_body(q_ref, kn_ref, vn_ref, ck_ref, cv_ref, o_ref, *, sq, past, tk):
    ones_new = _suffix_ones(sq)
    ones = _suffix_ones(tk)
    causal = lax.broadcasted_iota(jnp.int32, (sq, sq), 1) < lax.broadcasted_iota(jnp.int32, (sq, sq), 0)
    qs = [q_ref[:, hs] for hs in _SB_HEAD_COLS]
    accs, carries = _sb_tiles(qs, [kn_ref[:, hs] for hs in _SB_HEAD_COLS], [vn_ref[:, hs] for hs in _SB_HEAD_COLS],
                              [0.0] * SB_HEADS, ones_new, causal)
    n_tiles = past // tk
    group = 2 if n_tiles % 2 == 0 else 1
    for g in reversed(range(n_tiles // group)):
        kts = [g * group + group - 1 - i for i in range(group)]
        rows = [pl.ds(kt * tk * SB_HEADS + h, tk, stride=SB_HEADS) for kt in kts for h in range(SB_HEADS)]
        right_of = [None] * SB_HEADS + list(range(len(rows) - SB_HEADS))
        outs, tots = _sb_tiles(qs * group, [ck_ref[r, :].astype(BF16) for r in rows],
                               [cv_ref[r, :].astype(BF16) for r in rows], carries * group, ones, None, right_of)
        for i in range(len(rows)):
            accs[i % SB_HEADS] = accs[i % SB_HEADS] + outs[i]
            carries[i % SB_HEADS] = carries[i % SB_HEADS] + tots[i]
    for hs, acc in zip(_SB_HEAD_COLS, accs):
        o_ref[:, hs] = acc


def _sb_sample(q, k, v, cache_k, cache_v, layer, dec_batch, sq):
    past = cache_k.shape[2] // SB_HEADS
    new = lambda: pl.BlockSpec((sq, SB_WIDTH), lambda b: (b, 0))
    old = lambda: pl.BlockSpec((None, None, past * SB_HEADS, SB_DIM), lambda b: (layer, b, 0, 0))
    return pl.pallas_call(
        functools.partial(_sb_sample_body, sq=sq, past=past, tk=_pick_tile(256, past)),
        grid=(dec_batch,),
        in_specs=[new(), new(), new(), old(), old()],
        out_specs=pl.BlockSpec((sq, SB_WIDTH), lambda b: (b, 0)),
        out_shape=jax.ShapeDtypeStruct((dec_batch * sq, SB_WIDTH), F32),
        compiler_params=_params("parallel"),
        name="sb_sample",
    )(q, k, v, cache_k, cache_v)


def _merge_body(h_ref, ya_ref, gb_ref, uc_ref, prev_ref, yc_ref, cw_ref, ga_ref, gconv_ref, gc_ref,
                wo_ref, post_ref, o_ref, yb_ref, *, tm):
    ya = _rms(ya_ref[...], ga_ref[...]).astype(BF16)
    m = _dot(ya, wo_ref[:MLA_WIDTH, :])
    yc = _rms(yc_ref[...], gc_ref[...]).astype(BF16)
    m += _dot(yc, wo_ref[MLA_WIDTH + CONV_DIM:, :])
    cw = cw_ref[...]
    row = lax.broadcasted_iota(jnp.int32, (SEG, 1), 0)
    for s in range(tm // SEG):
        rows = slice(s * SEG, (s + 1) * SEG)
        u = uc_ref[rows, :]
        prev = prev_ref[s]
        u1 = jnp.where(row == 0, prev[1:2], pltpu.roll(u, 1, 0))
        u2 = jnp.where(row == 0, prev[0:1], jnp.where(row == 1, prev[1:2], pltpu.roll(u, 2, 0)))
        conv = cw[0:1] * u2 + cw[1:2] * u1 + cw[2:3] * u
        yb_ref[rows, :] = _rms(gb_ref[rows, :] * conv, gconv_ref[...]).astype(BF16)
    m += _dot(yb_ref[...], wo_ref[MLA_WIDTH:MLA_WIDTH + CONV_DIM, :])
    o_ref[...] = h_ref[...] + _rms(m, post_ref[...])


def _merge(h, ya, gb, uc, prev, yc, conv_w, ga, gconv, gc, w_o, post_g, layer, tm):
    r, d = h.shape
    row = lambda w: pl.BlockSpec((tm, w), lambda i: (i, 0))
    slab = lambda a: _layer_slab(a, layer, 1)
    return pl.pallas_call(
        functools.partial(_merge_body, tm=tm),
        grid=(r // tm,),
        in_specs=[
            row(d), row(MLA_WIDTH), row(CONV_DIM), row(CONV_DIM),
            pl.BlockSpec((tm // SEG, CONV_W - 1, CONV_DIM), lambda i: (i, 0, 0)),
            row(SB_WIDTH), slab(conv_w), slab(ga), slab(gconv), slab(gc), slab(w_o), slab(post_g),
        ],
        out_specs=row(d),
        out_shape=jax.ShapeDtypeStruct((r, d), F32),
        scratch_shapes=[pltpu.VMEM((tm, CONV_DIM), BF16)],
        compiler_params=_params("parallel"),
        name="merge",
    )(h, ya, gb, uc, prev, yc, conv_w, ga, gconv, gc, w_o, post_g)


def _rot_half(w):
    half = ROPE_DIM // 2
    return jnp.concatenate([-w[..., half:], w[..., :half]], axis=-1)


def _pad_slot(w):
    return jnp.pad(w, [(0, 0)] * (w.ndim - 1) + [(0, ROPE_SLOT - w.shape[-1])])


def _pack_w_in(w):
    wt = jnp.swapaxes(w, 1, 2)
    half = ROPE_DIM // 2
    k_rope = wt[:, _OFF_KROPE:_OFF_KROPE + ROPE_DIM]
    rot = jnp.concatenate([-k_rope[:, half:], k_rope[:, :half]], axis=1)
    pad = lambda a: jnp.pad(a, ((0, 0), (0, ROPE_SLOT - ROPE_DIM), (0, 0)))
    return wt.astype(BF16), jnp.concatenate([pad(k_rope), pad(rot)], axis=1).astype(BF16)


def _pack_w_uq(w):
    depth = w.shape[0]
    w = w.reshape(depth, Q_LORA, MLA_HEADS, NOPE_DIM + ROPE_DIM)
    flat = lambda a: a.reshape(depth, Q_LORA, -1)
    rope = w[..., NOPE_DIM:]
    return jnp.concatenate([flat(w[..., :NOPE_DIM]), flat(_pad_slot(rope)), flat(_pad_slot(_rot_half(rope)))],
                           axis=-1).astype(BF16)


def _rope_tables(pos):
    half = ROPE_DIM // 2
    inv_freq = ROPE_THETA ** (-jnp.arange(half, dtype=F32) / half)
    ang = pos.astype(F32)[:, None] * inv_freq[None, :]
    pad = lambda t: jnp.pad(jnp.concatenate([t, t], axis=1), ((0, 0), (0, ROPE_SLOT - ROPE_DIM)))
    return pad(jnp.cos(ang)), pad(jnp.sin(ang))


def kernel(x_prompt, x_sample, cache_mla_ckv, cache_mla_krope, cache_sb_k, cache_sb_v, state_conv, ffn1_pre_g, ffn1_w_gate, ffn1_w_up, ffn1_w_down, ffn1_post_g, mix_pre_g, w_in, mla_q_norm_g, mla_kv_norm_g, mla_w_uq, mla_w_uk, mla_w_uv, conv_w, out_norm_mla_g, out_norm_conv_g, out_norm_sb_g, w_o, mix_post_g, ffn2_pre_g, ffn2_w_gate, ffn2_w_up, ffn2_w_down, ffn2_post_g):
    batch, seq, d = x_prompt.shape
    dec_batch, sq, _ = x_sample.shape
    depth, _, past, _ = cache_mla_ckv.shape
    rp, rs = batch * seq, dec_batch * sq
    assert d == D_MODEL and sq == SEG and seq % SEG == 0

    tm_ffn = _pick_tile(1024, rp, rs)
    tf_ffn = _pick_tile(256, ffn1_w_gate.shape[2])
    tm = _pick_tile(256, seq, rs)
    tm_merge = _pick_tile(512, seq, rs)
    t_attn = _pick_tile(256, seq)

    cos_p, sin_p = _rope_tables(jnp.arange(seq, dtype=jnp.int32))
    cos_s, sin_s = _rope_tables(jnp.tile(past + jnp.arange(sq, dtype=jnp.int32), tm // sq))
    cache_kr_t = jnp.swapaxes(cache_mla_krope, 2, 3)
    cache_k = cache_sb_k.reshape(depth, dec_batch, past * SB_HEADS, SB_DIM)
    cache_v = cache_sb_v.reshape(depth, dec_batch, past * SB_HEADS, SB_DIM)
    gain = lambda g: g.reshape(depth, 1, -1)

    shared = (gain(mix_pre_g), *_pack_w_in(w_in), gain(mla_q_norm_g), gain(mla_kv_norm_g), _pack_w_uq(mla_w_uq))
    w_ukt = jnp.transpose(mla_w_uk, (0, 2, 3, 1)).astype(BF16)
    w_uv = jnp.transpose(mla_w_uv, (0, 2, 1, 3)).astype(BF16)
    w_uk_flat = mla_w_uk.reshape(depth, KV_LORA, MLA_HEADS * NOPE_DIM).astype(BF16)
    w_uv_flat = mla_w_uv.reshape(depth, KV_LORA, MLA_WIDTH).astype(BF16)
    merge_w = (conv_w, gain(out_norm_mla_g), gain(out_norm_conv_g), gain(out_norm_sb_g), w_o.astype(BF16),
               gain(mix_post_g))

    def new_state(rows):
        return (jnp.zeros((depth, rows, KV_LORA), F32), jnp.zeros((depth, rows, ROPE_DIM), F32),
                jnp.zeros((depth, rows * SB_HEADS, SB_DIM), F32), jnp.zeros((depth, rows * SB_HEADS, SB_DIM), F32))

    h_p, h_s = x_prompt.reshape(rp, d), x_sample.reshape(rs, d)
    state_p, state_s = new_state(rp), new_state(rs)
    conv_p, conv_s = [], []
    for l in range(depth):
        ffn1 = lambda x: _ffn(x, gain(ffn1_pre_g), ffn1_w_gate, ffn1_w_up, ffn1_w_down, gain(ffn1_post_g), l,
                              tm_ffn, tf_ffn)
        ffn2 = lambda x: _ffn(x, gain(ffn2_pre_g), ffn2_w_gate, ffn2_w_up, ffn2_w_down, gain(ffn2_post_g), l,
                              tm_ffn, tf_ffn)
        h_p, h_s = ffn1(h_p), ffn1(h_s)

        state_p, (gb, uc, qsb, ksbh, vsbh, q, k, v) = _proj_prompt(h_p, shared, w_uk_flat, w_uv_flat, cos_p, sin_p,
                                                                  state_p, l, tm)
        ya = _mla_prompt(q, k, v, batch, seq, t_attn, _pick_tile(2 * t_attn, seq))
        yc = _sb_prompt(qsb, ksbh, vsbh, batch, seq, t_attn)
        tails = uc.reshape(batch, seq // SEG, SEG, CONV_DIM)[:, :, SEG - (CONV_W - 1):]
        prev = jnp.concatenate([jnp.zeros_like(tails[:, :1]), tails[:, :-1]], axis=1)
        h_p = _merge(h_p, ya, gb, uc, prev.reshape(-1, CONV_W - 1, CONV_DIM), yc, *merge_w, l, tm_merge)
        conv_p.append(tails[:, -1])

        state_s, (gb, uc, qsb, ksbh, vsbh, q, kv) = _proj_sample(h_s, shared, w_ukt, cos_s, sin_s, state_s, l, tm)
        ya = _mla_sample(q, kv, cache_mla_ckv, cache_kr_t, w_uv, l, dec_batch, sq)
        yc = _sb_sample(qsb, ksbh, vsbh, cache_k, cache_v, l, dec_batch, sq)
        h_s = _merge(h_s, ya, gb, uc, state_conv[l], yc, *merge_w, l, tm_merge)
        conv_s.append(uc.reshape(dec_batch, sq, CONV_DIM)[:, sq - (CONV_W - 1):])

        h_p, h_s = ffn2(h_p), ffn2(h_s)

    def state_out(state, lead):
        ckv, kr, sbk, sbv = state
        return (ckv.reshape(depth, *lead, KV_LORA), kr.reshape(depth, *lead, ROPE_DIM),
                sbk.reshape(depth, *lead, SB_HEADS, SB_DIM), sbv.reshape(depth, *lead, SB_HEADS, SB_DIM))

    return (h_p.reshape(batch, seq, d), h_s.reshape(dec_batch, sq, d),
            *state_out(state_p, (batch, seq)), jnp.stack(conv_p, axis=0),
            *state_out(state_s, (dec_batch, sq)), jnp.stack(conv_s, axis=0))
```

```python
import functools

import jax
import jax.numpy as jnp
from jax import lax
from jax.experimental import pallas as pl
from jax.experimental.pallas import tpu as pltpu

F32 = jnp.float32
BF16 = jnp.bfloat16

D_MODEL = 2048
CHUNK = 64
EPS = 1e-6
MLA_HEADS = 8
Q_LORA = 512
KV_LORA = 512
NOPE_DIM = 128
ROPE_DIM = 64
V_DIM = 128
ROPE_THETA = 10000.0
MLA_WIDTH = MLA_HEADS * V_DIM
CONV_DIM = 512
CONV_W = 3
SB_HEADS = 4
SB_DIM = 128
SB_WIDTH = SB_HEADS * SB_DIM
_SB_HEAD_COLS = [slice(h * SB_DIM, (h + 1) * SB_DIM) for h in range(SB_HEADS)]
MLA_SCALE = (NOPE_DIM + ROPE_DIM) ** -0.5
SB_SCALE = SB_DIM ** -0.5
NEG_INF = -1e30
_LOG2E = 1.4426950408889634
_CHUNK_SHIFT = CHUNK.bit_length() - 1
assert 1 << _CHUNK_SHIFT == CHUNK

LANES = 128
ROPE_SLOT = LANES
KV_W = KV_LORA + ROPE_SLOT
QK_W = NOPE_DIM + ROPE_SLOT
SEG = 64
VMEM_LIMIT = 56 * 1024 * 1024
FFN_VMEM_LIMIT = 60 * 1024 * 1024
_MLA_HEAD_COLS = [slice(h * LANES, (h + 1) * LANES) for h in range(MLA_HEADS)]
assert NOPE_DIM == LANES and V_DIM == LANES

_OFF_CQ, _OFF_CKV, _OFF_KROPE = 0, Q_LORA, Q_LORA + KV_LORA
_OFF_GB, _OFF_GC, _OFF_XC, _OFF_QSB, _OFF_KSB, _OFF_VSB = (_OFF_KROPE + ROPE_DIM + 512 * i for i in range(6))
assert all(off % 16 == 0 for off in (_OFF_GB, _OFF_GC, _OFF_XC, _OFF_QSB, _OFF_KSB, _OFF_VSB))
_UQ_ROPE = MLA_HEADS * NOPE_DIM
_UQ_ROT = _UQ_ROPE + MLA_HEADS * ROPE_SLOT
W_UQ_PACKED = _UQ_ROT + MLA_HEADS * ROPE_SLOT


def _params(*sem, vmem_limit=VMEM_LIMIT):
    return pltpu.CompilerParams(dimension_semantics=sem, vmem_limit_bytes=vmem_limit)


def _layer_slab(a, layer, n_grid, single_buffer=True):
    zeros = (0,) * (a.ndim - 1)
    index_map = (lambda i: (layer,) + zeros) if n_grid == 1 else (lambda i, j: (layer,) + zeros)
    return pl.BlockSpec((None,) + a.shape[1:], index_map, pipeline_mode=pl.Buffered(1) if single_buffer else None)


def _rms(x, g):
    return x * lax.rsqrt(jnp.mean(x * x, axis=-1, keepdims=True) + EPS) * g


def _dot(a, b):
    return jnp.dot(a, b, preferred_element_type=F32)


def _dot_nt(a, b):
    return lax.dot_general(a, b, (((1,), (1,)), ((), ())), preferred_element_type=F32)


def _lane_tile(x, n):
    return jnp.concatenate([x] * n, axis=1)


def _pick_tile(cap, *sizes):
    t = cap
    while any(s % t for s in sizes):
        t //= 2
    return t


def _ffn_body(x_ref, pre_ref, wg_ref, wu_ref, wd_ref, post_ref, o_ref, xn_ref):
    j = pl.program_id(1)
    last = pl.num_programs(1) - 1

    def chunk(xn):
        a = jax.nn.silu(_dot(xn, wg_ref[...].astype(BF16))) * _dot(xn, wu_ref[...].astype(BF16))
        return _dot(a.astype(BF16), wd_ref[...].astype(BF16))

    @pl.when(j == 0)
    def _():
        xn = _rms(x_ref[...], pre_ref[...]).astype(BF16)
        xn_ref[...] = xn
        o_ref[...] = chunk(xn)

    @pl.when(jnp.logical_and(j > 0, j < last))
    def _():
        o_ref[...] += chunk(xn_ref[...])

    @pl.when(j == last)
    def _():
        acc = o_ref[...] + chunk(xn_ref[...])
        o_ref[...] = x_ref[...] + 0.5 * _rms(acc, post_ref[...])


def _ffn(x, pre_g, w_gate, w_up, w_down, post_g, layer, tm, tf):
    r, d = x.shape
    dff = w_gate.shape[2]
    return pl.pallas_call(
        _ffn_body,
        grid=(r // tm, dff // tf),
        in_specs=[
            pl.BlockSpec((tm, d), lambda i, j: (i, 0)),
            _layer_slab(pre_g, layer, 2),
            pl.BlockSpec((None, d, tf), lambda i, j: (layer, 0, j)),
            pl.BlockSpec((None, d, tf), lambda i, j: (layer, 0, j)),
            pl.BlockSpec((None, tf, d), lambda i, j: (layer, j, 0)),
            _layer_slab(post_g, layer, 2),
        ],
        out_specs=pl.BlockSpec((tm, d), lambda i, j: (i, 0)),
        out_shape=jax.ShapeDtypeStruct((r, d), F32),
        scratch_shapes=[pltpu.VMEM((tm, d), BF16)],
        compiler_params=_params("parallel", "arbitrary", vmem_limit=FFN_VMEM_LIMIT),
        name="ffn",
    )(x, pre_g, w_gate, w_up, w_down, post_g)


def _proj_shared(h_ref, pre_ref, win_ref, wkr_ref, qg_ref, kvg_ref, wuq_ref, cos_ref, sin_ref,
                 ckv_ref, kr_ref, ksb_ref, vsb_ref, gb_ref, uc_ref, qsb_ref, ksbh_ref, vsbh_ref):
    xn = _rms(h_ref[...], pre_ref[...]).astype(BF16)
    cos = cos_ref[...]
    sin = sin_ref[...]

    def col(off, width):
        return _dot_nt(xn, win_ref[off:off + width, :])

    c_kv = _rms(col(_OFF_CKV, KV_LORA), kvg_ref[...])
    ckv_ref[...] = c_kv
    kr_both = _dot_nt(xn, wkr_ref[...])
    k_rope = kr_both[:, :ROPE_SLOT] * cos + kr_both[:, ROPE_SLOT:] * sin
    kr_ref[...] = k_rope[:, :ROPE_DIM]

    gb_ref[...] = col(_OFF_GB, CONV_DIM)
    uc_ref[...] = col(_OFF_GC, CONV_DIM) * col(_OFF_XC, CONV_DIM)

    qsb_ref[...] = col(_OFF_QSB, SB_WIDTH).astype(BF16)
    k_sb = col(_OFF_KSB, SB_WIDTH)
    v_sb = col(_OFF_VSB, SB_WIDTH)
    tm = k_sb.shape[0]
    for h, hs in enumerate(_SB_HEAD_COLS):
        ksb_ref[pl.ds(h, tm, stride=SB_HEADS), :] = k_sb[:, hs]
        vsb_ref[pl.ds(h, tm, stride=SB_HEADS), :] = v_sb[:, hs]
    ksbh_ref[...] = k_sb.astype(BF16)
    vsbh_ref[...] = v_sb.astype(BF16)

    c_q = _rms(col(_OFF_CQ, Q_LORA), qg_ref[...]).astype(BF16)
    q_nope = _dot(c_q, wuq_ref[:, :_UQ_ROPE]).astype(BF16)
    q_rope = (_dot(c_q, wuq_ref[:, _UQ_ROPE:_UQ_ROT]) * _lane_tile(cos, MLA_HEADS)
              + _dot(c_q, wuq_ref[:, _UQ_ROT:]) * _lane_tile(sin, MLA_HEADS)).astype(BF16)
    return c_kv.astype(BF16), k_rope.astype(BF16), q_nope, q_rope


def _proj_prompt_body(h_ref, pre_ref, win_ref, wkr_ref, qg_ref, kvg_ref, wuq_ref, cos_ref, sin_ref, wuk_ref, wuv_ref,
                      ckv_in, kr_in, ksb_in, vsb_in,
                      ckv_ref, kr_ref, ksb_ref, vsb_ref, gb_ref, uc_ref, qsb_ref, ksbh_ref, vsbh_ref,
                      q_ref, k_ref, v_ref):
    del ckv_in, kr_in, ksb_in, vsb_in
    c_kv, k_rope, q_nope, q_rope = _proj_shared(
        h_ref, pre_ref, win_ref, wkr_ref, qg_ref, kvg_ref, wuq_ref, cos_ref, sin_ref,
        ckv_ref, kr_ref, ksb_ref, vsb_ref, gb_ref, uc_ref, qsb_ref, ksbh_ref, vsbh_ref)
    k_nope = _dot(c_kv, wuk_ref[...]).astype(BF16)
    v = _dot(c_kv, wuv_ref[...]).astype(BF16)
    for h, hs in enumerate(_MLA_HEAD_COLS):
        q_ref[h, :, :NOPE_DIM] = q_nope[:, hs]
        q_ref[h, :, NOPE_DIM:] = q_rope[:, hs]
        k_ref[h, :, :NOPE_DIM] = k_nope[:, hs]
        k_ref[h, :, NOPE_DIM:] = k_rope
        v_ref[h] = v[:, hs]


def _proj_sample_body(h_ref, pre_ref, win_ref, wkr_ref, qg_ref, kvg_ref, wuq_ref, cos_ref, sin_ref, wukt_ref,
                      ckv_in, kr_in, ksb_in, vsb_in,
                      ckv_ref, kr_ref, ksb_ref, vsb_ref, gb_ref, uc_ref, qsb_ref, ksbh_ref, vsbh_ref,
                      q_ref, kv_ref):
    del ckv_in, kr_in, ksb_in, vsb_in
    c_kv, k_rope, q_nope, q_rope = _proj_shared(
        h_ref, pre_ref, win_ref, wkr_ref, qg_ref, kvg_ref, wuq_ref, cos_ref, sin_ref,
        ckv_ref, kr_ref, ksb_ref, vsb_ref, gb_ref, uc_ref, qsb_ref, ksbh_ref, vsbh_ref)
    kv_ref[:, :KV_LORA] = c_kv
    kv_ref[:, KV_LORA:] = k_rope
    for h, hs in enumerate(_MLA_HEAD_COLS):
        q_ref[h, :, :KV_LORA] = _dot(q_nope[:, hs], wukt_ref[h]).astype(BF16)
        q_ref[h, :, KV_LORA:] = q_rope[:, hs]


def _proj(body, h, shared, extra, cos, sin, state, layer, tm, attn_shapes, attn_specs):
    r, d = h.shape
    n_tab = cos.shape[0] // tm
    row = lambda w: pl.BlockSpec((tm, w), lambda i: (i, 0))
    table = pl.BlockSpec((tm, ROPE_SLOT), lambda i: (i % n_tab, 0))
    slab = lambda a: _layer_slab(a, layer, 1)
    state_spec = lambda a: pl.BlockSpec((None, a.shape[1] // (r // tm), a.shape[2]), lambda i: (layer, i, 0))
    sds = jax.ShapeDtypeStruct
    n_in = 1 + len(shared) + 2 + len(extra)
    outs = pl.pallas_call(
        body,
        grid=(r // tm,),
        in_specs=([row(d)] + [slab(a) for a in shared] + [table, table] + [slab(a) for a in extra]
                  + [pl.BlockSpec(memory_space=pl.ANY)] * len(state)),
        out_specs=([state_spec(a) for a in state]
                   + [row(CONV_DIM), row(CONV_DIM), row(SB_WIDTH), row(SB_WIDTH), row(SB_WIDTH)] + attn_specs),
        out_shape=([sds(a.shape, a.dtype) for a in state]
                   + [sds((r, CONV_DIM), F32), sds((r, CONV_DIM), F32), sds((r, SB_WIDTH), BF16),
                      sds((r, SB_WIDTH), BF16), sds((r, SB_WIDTH), BF16)] + attn_shapes),
        input_output_aliases={n_in + k: k for k in range(len(state))},
        compiler_params=_params("parallel"),
        name=body.__name__.strip("_").replace("_body", ""),
    )(h, *shared, cos, sin, *extra, *state)
    return outs[:len(state)], outs[len(state):]


def _proj_prompt(h, shared, w_uk, w_uv, cos, sin, state, layer, tm):
    r = h.shape[0]
    heads3 = lambda w: pl.BlockSpec((MLA_HEADS, tm, w), lambda i: (0, i, 0))
    sds = jax.ShapeDtypeStruct
    return _proj(_proj_prompt_body, h, shared, (w_uk, w_uv), cos, sin, state, layer, tm,
                 [sds((MLA_HEADS, r, QK_W), BF16), sds((MLA_HEADS, r, QK_W), BF16), sds((MLA_HEADS, r, V_DIM), BF16)],
                 [heads3(QK_W), heads3(QK_W), heads3(V_DIM)])


def _proj_sample(h, shared, w_ukt, cos, sin, state, layer, tm):
    r = h.shape[0]
    sds = jax.ShapeDtypeStruct
    return _proj(_proj_sample_body, h, shared, (w_ukt,), cos, sin, state, layer, tm,
                 [sds((MLA_HEADS, r, KV_W), BF16), sds((r, KV_W), BF16)],
                 [pl.BlockSpec((MLA_HEADS, tm, KV_W), lambda i: (0, i, 0)), pl.BlockSpec((tm, KV_W), lambda i: (i, 0))])


def _mla_prompt_body(q_ref, k_ref, v_ref, o_ref, m_ref, l_ref, acc_ref, *, tq, tk):
    qi = pl.program_id(1)
    m_ref[...] = jnp.full_like(m_ref, NEG_INF)
    l_ref[...] = jnp.zeros_like(l_ref)
    acc_ref[...] = jnp.zeros_like(acc_ref)

    def step(kts, masked):
        keys = [pl.ds(pl.multiple_of(kt * tk, tk), tk) for kt in kts]
        if masked:
            qpos = qi * tq + lax.broadcasted_iota(jnp.int32, (tq, 1), 0)
            kpos = [kt * tk + lax.broadcasted_iota(jnp.int32, (1, tk), 1) for kt in kts]
            visible = [(kp >> _CHUNK_SHIFT) <= (qpos >> _CHUNK_SHIFT) for kp in kpos]

        def scores(h):
            s = [_dot_nt(q_ref[h], k_ref[h, kk, :]) * (MLA_SCALE * _LOG2E) for kk in keys]
            return [jnp.where(vis, x, NEG_INF) for vis, x in zip(visible, s)] if masked else s

        s_next = scores(0)
        for h in range(MLA_HEADS):
            s = s_next
            if h + 1 < MLA_HEADS:
                s_next = scores(h + 1)
            rs = slice(h * tq, (h + 1) * tq)
            m_old = m_ref[rs]
            m_new = functools.reduce(jnp.maximum, [m_old] + [jnp.max(x, axis=-1, keepdims=True) for x in s])
            alpha = jnp.exp2(m_old - m_new)
            p = [jnp.exp2(x - _lane_tile(m_new, tk // LANES)) for x in s]
            l_ref[rs] = alpha * l_ref[rs] + sum(x[:, c * LANES:(c + 1) * LANES] for x in p for c in range(tk // LANES))
            acc_ref[rs] = alpha * acc_ref[rs] + sum(_dot(x.astype(BF16), v_ref[h, kk, :]) for x, kk in zip(p, keys))
            m_ref[rs] = m_new

    n_full = (qi * tq) // tk
    lax.fori_loop(0, n_full // 2, lambda i, c: (step([2 * i, 2 * i + 1], False), c)[1], 0)

    @pl.when(n_full % 2 == 1)
    def _():
        step([n_full - 1], False)

    for d in range(-(-tq // tk)):
        step([n_full + d], True)

    for h in range(MLA_HEADS):
        rs = slice(h * tq, (h + 1) * tq)
        o_ref[:, h * V_DIM:(h + 1) * V_DIM] = acc_ref[rs] / jnp.sum(l_ref[rs], axis=-1, keepdims=True)


def _mla_prompt(q, k, v, batch, seq, tq, tk):
    nq = seq // tq
    rows = MLA_HEADS * tq
    whole_batch = lambda w: pl.BlockSpec((MLA_HEADS, seq, w), lambda b, i: (0, b, 0), pipeline_mode=pl.Buffered(1))
    return pl.pallas_call(
        functools.partial(_mla_prompt_body, tq=tq, tk=tk),
        grid=(batch, nq),
        in_specs=[
            pl.BlockSpec((MLA_HEADS, tq, QK_W), lambda b, i: (0, b * nq + i, 0)),
            whole_batch(QK_W), whole_batch(V_DIM),
        ],
        out_specs=pl.BlockSpec((tq, MLA_WIDTH), lambda b, i: (b * nq + i, 0)),
        out_shape=jax.ShapeDtypeStruct((batch * seq, MLA_WIDTH), F32),
        scratch_shapes=[pltpu.VMEM((rows, LANES), F32), pltpu.VMEM((rows, LANES), F32),
                        pltpu.VMEM((rows, V_DIM), F32)],
        compiler_params=_params("parallel", "arbitrary"),
        name="mla_prompt",
    )(q, k, v)


def _mla_up(o, wuv_ref, o_ref, tq):
    for h in range(MLA_HEADS):
        oh = o[h * tq:(h + 1) * tq].astype(BF16)
        o_ref[:, h * V_DIM:(h + 1) * V_DIM] = _dot(oh, wuv_ref[h])


def _mla_sample_body(q_ref, kvn_ref, cc_ref, ckr_ref, wuv_ref, o_ref, *, sq, tk):
    rows = MLA_HEADS * sq
    q = q_ref[...].reshape(rows, KV_W)
    q_lat, q_rope = q[:, :KV_LORA], q[:, KV_LORA:KV_LORA + ROPE_DIM]
    scale = MLA_SCALE * _LOG2E

    def cache_tile(t):
        keys = slice(t * tk, (t + 1) * tk)
        cc = cc_ref[keys, :].astype(BF16)
        return (_dot_nt(q_lat, cc) + _dot(q_rope, ckr_ref[:, keys].astype(BF16))) * scale, cc

    def new_tile():
        kvn = kvn_ref[...]
        return _dot_nt(q, kvn) * scale, kvn[:, :KV_LORA]

    tiles = [functools.partial(cache_tile, t) for t in range(cc_ref.shape[0] // tk)] + [new_tile]
    m = jnp.full((rows, 1), NEG_INF, F32)
    l = jnp.zeros((rows, 1), F32)
    acc = jnp.zeros((rows, KV_LORA), F32)
    nxt = tiles[0]()
    for i in range(len(tiles)):
        s, values = nxt
        if i + 1 < len(tiles):
            nxt = tiles[i + 1]()
        m_new = jnp.maximum(m, jnp.max(s, axis=-1, keepdims=True))
        alpha = jnp.exp2(m - m_new)
        p = jnp.exp2(s - m_new)
        l = alpha * l + jnp.sum(p, axis=-1, keepdims=True)
        acc = alpha * acc + _dot(p.astype(BF16), values)
        m = m_new
    _mla_up(acc / l, wuv_ref, o_ref, sq)


def _mla_sample(q, kv, cache_ckv, cache_kr_t, w_uv, layer, dec_batch, sq):
    past = cache_ckv.shape[2]
    assert past % CHUNK == 0 and sq <= CHUNK
    return pl.pallas_call(
        functools.partial(_mla_sample_body, sq=sq, tk=_pick_tile(512, past)),
        grid=(dec_batch,),
        in_specs=[
            pl.BlockSpec((MLA_HEADS, sq, KV_W), lambda b: (0, b, 0)),
            pl.BlockSpec((sq, KV_W), lambda b: (b, 0)),
            pl.BlockSpec((None, None, past, KV_LORA), lambda b: (layer, b, 0, 0)),
            pl.BlockSpec((None, None, ROPE_DIM, past), lambda b: (layer, b, 0, 0)),
            _layer_slab(w_uv, layer, 1),
        ],
        out_specs=pl.BlockSpec((sq, MLA_WIDTH), lambda b: (b, 0)),
        out_shape=jax.ShapeDtypeStruct((dec_batch * sq, MLA_WIDTH), F32),
        compiler_params=_params("parallel"),
        name="mla_sample",
    )(q, kv, cache_ckv, cache_kr_t, w_uv)


def _suffix_ones(n):
    j = lax.broadcasted_iota(jnp.int32, (n, n), 0)
    k = lax.broadcasted_iota(jnp.int32, (n, n), 1)
    return jnp.where(j > k, -1.0, 0.0).astype(BF16)


def _sb_tiles(qs, ks, vs, carries, ones, causal, right_of=None):
    log_beta, sums, totals = _sb_keep(_sb_logits(qs, ks), ones, causal)
    return _sb_apply(log_beta, sums, totals, carries, vs, causal, right_of), totals


def _sb_logits(qs, ks):
    return [_dot_nt(q, k) * (SB_SCALE * _LOG2E) for q, k in zip(qs, ks)]


def _sb_keep(logits, ones, causal):
    log_beta, drop, split = [], [], []
    for x in logits:
        sp = jnp.maximum(x, 0.0) + jnp.log(1.0 + jnp.exp2(-jnp.abs(x))) * _LOG2E
        log_beta.append(x - sp)
        dr = sp if causal is None else jnp.where(causal, sp, 0.0)
        hi = dr.astype(BF16)
        lo = (dr - hi.astype(F32)).astype(BF16)
        drop.append(dr)
        split.append(jnp.concatenate([hi, lo], axis=0))
    sums = [_dot(x, ones) for x in split]
    return log_beta, sums, [-jnp.sum(dr, axis=-1, keepdims=True) for dr in drop]


def _sb_apply(log_beta, sums, totals, carries, vs, causal, right_of):
    rows = log_beta[0].shape[0]
    weights = []
    for i, (lb, sm, carry) in enumerate(zip(log_beta, sums, carries)):
        if right_of is not None and right_of[i] is not None:
            carry = carry + totals[right_of[i]]
        between = sm[:rows] + sm[rows:] + carry
        w = jnp.exp2(lb + between)
        weights.append(w if causal is None else jnp.where(causal, w, 0.0))
    return [_dot(w.astype(BF16), v) for w, v in zip(weights, vs)]


def _sb_prompt_body(q_ref, k_ref, v_ref, o_ref, carry_ref, *, t):
    qi = pl.program_id(1)
    ones = _suffix_ones(t)
    causal = lax.broadcasted_iota(jnp.int32, (t, t), 1) < lax.broadcasted_iota(jnp.int32, (t, t), 0)
    o_ref[...] = jnp.zeros_like(o_ref)
    carry_ref[...] = jnp.zeros_like(carry_ref)

    def tiles(kts, mask):
        rows = [pl.ds(pl.multiple_of(kt * t, t), t) for kt in kts]
        chains = [(r, hs) for r in rows for hs in _SB_HEAD_COLS]
        carries = [_lane_tile(carry_ref[:, hs], t // LANES) for _, hs in chains]
        right_of = [None] * SB_HEADS + list(range(len(chains) - SB_HEADS))
        outs, tots = _sb_tiles([q_ref[:, hs] for _, hs in chains], [k_ref[r, hs] for r, hs in chains],
                               [v_ref[r, hs] for r, hs in chains], carries, ones, mask, right_of)
        for (_, hs), out, tot in zip(chains, outs, tots):
            o_ref[:, hs] += out
            carry_ref[:, hs] += tot

    tiles([qi], causal)
    lax.fori_loop(0, qi // 2, lambda i, c: (tiles([qi - 1 - 2 * i, qi - 2 - 2 * i], None), c)[1], 0)

    @pl.when(qi % 2 == 1)
    def _():
        tiles([0], None)


def _sb_prompt(q, k, v, batch, seq, t):
    nq = seq // t
    return pl.pallas_call(
        functools.partial(_sb_prompt_body, t=t),
        grid=(batch, nq),
        in_specs=[
            pl.BlockSpec((t, SB_WIDTH), lambda b, i: (b * nq + i, 0)),
            pl.BlockSpec((seq, SB_WIDTH), lambda b, i: (b, 0)),
            pl.BlockSpec((seq, SB_WIDTH), lambda b, i: (b, 0)),
        ],
        out_specs=pl.BlockSpec((t, SB_WIDTH), lambda b, i: (b * nq + i, 0)),
        out_shape=jax.ShapeDtypeStruct((batch * seq, SB_WIDTH), F32),
        scratch_shapes=[pltpu.VMEM((t, SB_HEADS * LANES), F32)],
        compiler_params=_params("parallel", "arbitrary"),
        name="sb_prompt",
    )(q, k, v)


def _sb_sample_body(q_ref, kn_ref, vn_ref, ck_ref, cv_ref, o_ref, *, sq, past, tk):
    ones_new = _suffix_ones(sq)
    ones = _suffix_ones(tk)
    causal = lax.broadcasted_iota(jnp.int32, (sq, sq), 1) < lax.broadcasted_iota(jnp.int32, (sq, sq), 0)
    qs = [q_ref[:, hs] for hs in _SB_HEAD_COLS]
    accs, carries = _sb_tiles(qs, [kn_ref[:, hs] for hs in _SB_HEAD_COLS], [vn_ref[:, hs] for hs in _SB_HEAD_COLS],
                              [0.0] * SB_HEADS, ones_new, causal)
    n_tiles = past // tk
    group = 2 if n_tiles % 2 == 0 else 1
    for g in reversed(range(n_tiles // group)):
        kts = [g * group + group - 1 - i for i in range(group)]
        rows = [pl.ds(kt * tk * SB_HEADS + h, tk, stride=SB_HEADS) for kt in kts for h in range(SB_HEADS)]
        right_of = [None] * SB_HEADS + list(range(len(rows) - SB_HEADS))
        outs, tots = _sb_tiles(qs * group, [ck_ref[r, :].astype(BF16) for r in rows],
                               [cv_ref[r, :].astype(BF16) for r in rows], carries * group, ones, None, right_of)
        for i in range(len(rows)):
            accs[i % SB_HEADS] = accs[i % SB_HEADS] + outs[i]
            carries[i % SB_HEADS] = carries[i % SB_HEADS] + tots[i]
    for hs, acc in zip(_SB_HEAD_COLS, accs):
        o_ref[:, hs] = acc


def _sb_sample(q, k, v, cache_k, cache_v, layer, dec_batch, sq):
    past = cache_k.shape[2] // SB_HEADS
    new = lambda: pl.BlockSpec((sq, SB_WIDTH), lambda b: (b, 0))
    old = lambda: pl.BlockSpec((None, None, past * SB_HEADS, SB_DIM), lambda b: (layer, b, 0, 0))
    return pl.pallas_call(
        functools.partial(_sb_sample_body, sq=sq, past=past, tk=_pick_tile(256, past)),
        grid=(dec_batch,),
        in_specs=[new(), new(), new(), old(), old()],
        out_specs=pl.BlockSpec((sq, SB_WIDTH), lambda b: (b, 0)),
        out_shape=jax.ShapeDtypeStruct((dec_batch * sq, SB_WIDTH), F32),
        compiler_params=_params("parallel"),
        name="sb_sample",
    )(q, k, v, cache_k, cache_v)


def _merge_body(h_ref, ya_ref, gb_ref, uc_ref, prev_ref, yc_ref, cw_ref, ga_ref, gconv_ref, gc_ref,
                wo_ref, post_ref, o_ref, yb_ref, *, tm):
    ya = _rms(ya_ref[...], ga_ref[...]).astype(BF16)
    m = _dot(ya, wo_ref[:MLA_WIDTH, :])
    yc = _rms(yc_ref[...], gc_ref[...]).astype(BF16)
    m += _dot(yc, wo_ref[MLA_WIDTH + CONV_DIM:, :])
    cw = cw_ref[...]
    row = lax.broadcasted_iota(jnp.int32, (SEG, 1), 0)
    for s in range(tm // SEG):
        rows = slice(s * SEG, (s + 1) * SEG)
        u = uc_ref[rows, :]
        prev = prev_ref[s]
        u1 = jnp.where(row == 0, prev[1:2], pltpu.roll(u, 1, 0))
        u2 = jnp.where(row == 0, prev[0:1], jnp.where(row == 1, prev[1:2], pltpu.roll(u, 2, 0)))
        conv = cw[0:1] * u2 + cw[1:2] * u1 + cw[2:3] * u
        yb_ref[rows, :] = _rms(gb_ref[rows, :] * conv, gconv_ref[...]).astype(BF16)
    m += _dot(yb_ref[...], wo_ref[MLA_WIDTH:MLA_WIDTH + CONV_DIM, :])
    o_ref[...] = h_ref[...] + _rms(m, post_ref[...])


def _merge(h, ya, gb, uc, prev, yc, conv_w, ga, gconv, gc, w_o, post_g, layer, tm):
    r, d = h.shape
    row = lambda w: pl.BlockSpec((tm, w), lambda i: (i, 0))
    slab = lambda a: _layer_slab(a, layer, 1)
    return pl.pallas_call(
        functools.partial(_merge_body, tm=tm),
        grid=(r // tm,),
        in_specs=[
            row(d), row(MLA_WIDTH), row(CONV_DIM), row(CONV_DIM),
            pl.BlockSpec((tm // SEG, CONV_W - 1, CONV_DIM), lambda i: (i, 0, 0)),
            row(SB_WIDTH), slab(conv_w), slab(ga), slab(gconv), slab(gc), slab(w_o), slab(post_g),
        ],
        out_specs=row(d),
        out_shape=jax.ShapeDtypeStruct((r, d), F32),
        scratch_shapes=[pltpu.VMEM((tm, CONV_DIM), BF16)],
        compiler_params=_params("parallel"),
        name="merge",
    )(h, ya, gb, uc, prev, yc, conv_w, ga, gconv, gc, w_o, post_g)


def _rot_half(w):
    half = ROPE_DIM // 2
    return jnp.concatenate([-w[..., half:], w[..., :half]], axis=-1)


def _pad_slot(w):
    return jnp.pad(w, [(0, 0)] * (w.ndim - 1) + [(0, ROPE_SLOT - w.shape[-1])])


def _pack_w_in(w):
    wt = jnp.swapaxes(w, 1, 2)
    half = ROPE_DIM // 2
    k_rope = wt[:, _OFF_KROPE:_OFF_KROPE + ROPE_DIM]
    rot = jnp.concatenate([-k_rope[:, half:], k_rope[:, :half]], axis=1)
    pad = lambda a: jnp.pad(a, ((0, 0), (0, ROPE_SLOT - ROPE_DIM), (0, 0)))
    return wt.astype(BF16), jnp.concatenate([pad(k_rope), pad(rot)], axis=1).astype(BF16)


def _pack_w_uq(w):
    depth = w.shape[0]
    w = w.reshape(depth, Q_LORA, MLA_HEADS, NOPE_DIM + ROPE_DIM)
    flat = lambda a: a.reshape(depth, Q_LORA, -1)
    rope = w[..., NOPE_DIM:]
    return jnp.concatenate([flat(w[..., :NOPE_DIM]), flat(_pad_slot(rope)), flat(_pad_slot(_rot_half(rope)))],
                           axis=-1).astype(BF16)


def _rope_tables(pos):
    half = ROPE_DIM // 2
    inv_freq = ROPE_THETA ** (-jnp.arange(half, dtype=F32) / half)
    ang = pos.astype(F32)[:, None] * inv_freq[None, :]
    pad = lambda t: jnp.pad(jnp.concatenate([t, t], axis=1), ((0, 0), (0, ROPE_SLOT - ROPE_DIM)))
    return pad(jnp.cos(ang)), pad(jnp.sin(ang))


def kernel(x_prompt, x_sample, cache_mla_ckv, cache_mla_krope, cache_sb_k, cache_sb_v, state_conv, ffn1_pre_g, ffn1_w_gate, ffn1_w_up, ffn1_w_down, ffn1_post_g, mix_pre_g, w_in, mla_q_norm_g, mla_kv_norm_g, mla_w_uq, mla_w_uk, mla_w_uv, conv_w, out_norm_mla_g, out_norm_conv_g, out_norm_sb_g, w_o, mix_post_g, ffn2_pre_g, ffn2_w_gate, ffn2_w_up, ffn2_w_down, ffn2_post_g):
    batch, seq, d = x_prompt.shape
    dec_batch, sq, _ = x_sample.shape
    depth, _, past, _ = cache_mla_ckv.shape
    rp, rs = batch * seq, dec_batch * sq
    assert d == D_MODEL and sq == SEG and seq % SEG == 0

    tm_ffn = _pick_tile(1024, rp, rs)
    tf_ffn = _pick_tile(256, ffn1_w_gate.shape[2])
    tm = _pick_tile(256, seq, rs)
    tm_merge = _pick_tile(512, seq, rs)
    t_attn = _pick_tile(256, seq)

    cos_p, sin_p = _rope_tables(jnp.arange(seq, dtype=jnp.int32))
    cos_s, sin_s = _rope_tables(jnp.tile(past + jnp.arange(sq, dtype=jnp.int32), tm // sq))
    cache_kr_t = jnp.swapaxes(cache_mla_krope, 2, 3)
    cache_k = cache_sb_k.reshape(depth, dec_batch, past * SB_HEADS, SB_DIM)
    cache_v = cache_sb_v.reshape(depth, dec_batch, past * SB_HEADS, SB_DIM)
    gain = lambda g: g.reshape(depth, 1, -1)

    shared = (gain(mix_pre_g), *_pack_w_in(w_in), gain(mla_q_norm_g), gain(mla_kv_norm_g), _pack_w_uq(mla_w_uq))
    w_ukt = jnp.transpose(mla_w_uk, (0, 2, 3, 1)).astype(BF16)
    w_uv = jnp.transpose(mla_w_uv, (0, 2, 1, 3)).astype(BF16)
    w_uk_flat = mla_w_uk.reshape(depth, KV_LORA, MLA_HEADS * NOPE_DIM).astype(BF16)
    w_uv_flat = mla_w_uv.reshape(depth, KV_LORA, MLA_WIDTH).astype(BF16)
    merge_w = (conv_w, gain(out_norm_mla_g), gain(out_norm_conv_g), gain(out_norm_sb_g), w_o.astype(BF16),
               gain(mix_post_g))

    def new_state(rows):
        return (jnp.zeros((depth, rows, KV_LORA), F32), jnp.zeros((depth, rows, ROPE_DIM), F32),
                jnp.zeros((depth, rows * SB_HEADS, SB_DIM), F32), jnp.zeros((depth, rows * SB_HEADS, SB_DIM), F32))

    h_p, h_s = x_prompt.reshape(rp, d), x_sample.reshape(rs, d)
    state_p, state_s = new_state(rp), new_state(rs)
    conv_p, conv_s = [], []
    for l in range(depth):
        ffn1 = lambda x: _ffn(x, gain(ffn1_pre_g), ffn1_w_gate, ffn1_w_up, ffn1_w_down, gain(ffn1_post_g), l,
                              tm_ffn, tf_ffn)
        ffn2 = lambda x: _ffn(x, gain(ffn2_pre_g), ffn2_w_gate, ffn2_w_up, ffn2_w_down, gain(ffn2_post_g), l,
                              tm_ffn, tf_ffn)
        h_p, h_s = ffn1(h_p), ffn1(h_s)

        state_p, (gb, uc, qsb, ksbh, vsbh, q, k, v) = _proj_prompt(h_p, shared, w_uk_flat, w_uv_flat, cos_p, sin_p,
                                                                  state_p, l, tm)
        ya = _mla_prompt(q, k, v, batch, seq, t_attn, _pick_tile(2 * t_attn, seq))
        yc = _sb_prompt(qsb, ksbh, vsbh, batch, seq, t_attn)
        tails = uc.reshape(batch, seq // SEG, SEG, CONV_DIM)[:, :, SEG - (CONV_W - 1):]
        prev = jnp.concatenate([jnp.zeros_like(tails[:, :1]), tails[:, :-1]], axis=1)
        h_p = _merge(h_p, ya, gb, uc, prev.reshape(-1, CONV_W - 1, CONV_DIM), yc, *merge_w, l, tm_merge)
        conv_p.append(tails[:, -1])

        state_s, (gb, uc, qsb, ksbh, vsbh, q, kv) = _proj_sample(h_s, shared, w_ukt, cos_s, sin_s, state_s, l, tm)
        ya = _mla_sample(q, kv, cache_mla_ckv, cache_kr_t, w_uv, l, dec_batch, sq)
        yc = _sb_sample(qsb, ksbh, vsbh, cache_k, cache_v, l, dec_batch, sq)
        h_s = _merge(h_s, ya, gb, uc, state_conv[l], yc, *merge_w, l, tm_merge)
        conv_s.append(uc.reshape(dec_batch, sq, CONV_DIM)[:, sq - (CONV_W - 1):])

        h_p, h_s = ffn2(h_p), ffn2(h_s)

    def state_out(state, lead):
        ckv, kr, sbk, sbv = state
        return (ckv.reshape(depth, *lead, KV_LORA), kr.reshape(depth, *lead, ROPE_DIM),
                sbk.reshape(depth, *lead, SB_HEADS, SB_DIM), sbv.reshape(depth, *lead, SB_HEADS, SB_DIM))

    return (h_p.reshape(batch, seq, d), h_s.reshape(dec_batch, sq, d),
            *state_out(state_p, (batch, seq)), jnp.stack(conv_p, axis=0),
            *state_out(state_s, (dec_batch, sq)), jnp.stack(conv_s, axis=0))
```

```python
import functools

import jax
import jax.numpy as jnp
from jax import lax
from jax.experimental import pallas as pl
from jax.experimental.pallas import tpu as pltpu

F32 = jnp.float32
BF16 = jnp.bfloat16

D_MODEL = 2048
CHUNK = 64
EPS = 1e-6
MLA_HEADS = 8
Q_LORA = 512
KV_LORA = 512
NOPE_DIM = 128
ROPE_DIM = 64
V_DIM = 128
ROPE_THETA = 10000.0
MLA_WIDTH = MLA_HEADS * V_DIM
CONV_DIM = 512
CONV_W = 3
SB_HEADS = 4
SB_DIM = 128
SB_WIDTH = SB_HEADS * SB_DIM
_SB_HEAD_COLS = [slice(h * SB_DIM, (h + 1) * SB_DIM) for h in range(SB_HEADS)]
MLA_SCALE = (NOPE_DIM + ROPE_DIM) ** -0.5
SB_SCALE = SB_DIM ** -0.5
NEG_INF = -1e30
_LOG2E = 1.4426950408889634
_CHUNK_SHIFT = CHUNK.bit_length() - 1
assert 1 << _CHUNK_SHIFT == CHUNK

LANES = 128
ROPE_SLOT = LANES
KV_W = KV_LORA + ROPE_SLOT
QK_W = NOPE_DIM + ROPE_SLOT
SEG = 64
VMEM_LIMIT = 56 * 1024 * 1024
FFN_VMEM_LIMIT = 60 * 1024 * 1024
_MLA_HEAD_COLS = [slice(h * LANES, (h + 1) * LANES) for h in range(MLA_HEADS)]
assert NOPE_DIM == LANES and V_DIM == LANES

_OFF_CQ, _OFF_CKV, _OFF_KROPE = 0, Q_LORA, Q_LORA + KV_LORA
_OFF_GB, _OFF_GC, _OFF_XC, _OFF_QSB, _OFF_KSB, _OFF_VSB = (_OFF_KROPE + ROPE_DIM + 512 * i for i in range(6))
assert all(off % 16 == 0 for off in (_OFF_GB, _OFF_GC, _OFF_XC, _OFF_QSB, _OFF_KSB, _OFF_VSB))
_UQ_ROPE = MLA_HEADS * NOPE_DIM
_UQ_ROT = _UQ_ROPE + MLA_HEADS * ROPE_SLOT
W_UQ_PACKED = _UQ_ROT + MLA_HEADS * ROPE_SLOT


def _params(*sem, vmem_limit=VMEM_LIMIT):
    return pltpu.CompilerParams(dimension_semantics=sem, vmem_limit_bytes=vmem_limit)


def _layer_slab(a, layer, n_grid, single_buffer=True):
    zeros = (0,) * (a.ndim - 1)
    index_map = (lambda i: (layer,) + zeros) if n_grid == 1 else (lambda i, j: (layer,) + zeros)
    return pl.BlockSpec((None,) + a.shape[1:], index_map, pipeline_mode=pl.Buffered(1) if single_buffer else None)


def _rms(x, g):
    return x * lax.rsqrt(jnp.mean(x * x, axis=-1, keepdims=True) + EPS) * g


def _dot(a, b):
    return jnp.dot(a, b, preferred_element_type=F32)


def _dot_nt(a, b):
    return lax.dot_general(a, b, (((1,), (1,)), ((), ())), preferred_element_type=F32)


def _lane_tile(x, n):
    return jnp.concatenate([x] * n, axis=1)


def _pick_tile(cap, *sizes):
    t = cap
    while any(s % t for s in sizes):
        t //= 2
    return t


def _ffn_body(x_ref, pre_ref, wg_ref, wu_ref, wd_ref, post_ref, o_ref, xn_ref):
    j = pl.program_id(1)
    last = pl.num_programs(1) - 1

    def chunk(xn):
        a = jax.nn.silu(_dot(xn, wg_ref[...].astype(BF16))) * _dot(xn, wu_ref[...].astype(BF16))
        return _dot(a.astype(BF16), wd_ref[...].astype(BF16))

    @pl.when(j == 0)
    def _():
        xn = _rms(x_ref[...], pre_ref[...]).astype(BF16)
        xn_ref[...] = xn
        o_ref[...] = chunk(xn)

    @pl.when(jnp.logical_and(j > 0, j < last))
    def _():
        o_ref[...] += chunk(xn_ref[...])

    @pl.when(j == last)
    def _():
        acc = o_ref[...] + chunk(xn_ref[...])
        o_ref[...] = x_ref[...] + 0.5 * _rms(acc, post_ref[...])


def _ffn(x, pre_g, w_gate, w_up, w_down, post_g, layer, tm, tf):
    r, d = x.shape
    dff = w_gate.shape[2]
    return pl.pallas_call(
        _ffn_body,
        grid=(r // tm, dff // tf),
        in_specs=[
            pl.BlockSpec((tm, d), lambda i, j: (i, 0)),
            _layer_slab(pre_g, layer, 2),
            pl.BlockSpec((None, d, tf), lambda i, j: (layer, 0, j)),
            pl.BlockSpec((None, d, tf), lambda i, j: (layer, 0, j)),
            pl.BlockSpec((None, tf, d), lambda i, j: (layer, j, 0)),
            _layer_slab(post_g, layer, 2),
        ],
        out_specs=pl.BlockSpec((tm, d), lambda i, j: (i, 0)),
        out_shape=jax.ShapeDtypeStruct((r, d), F32),
        scratch_shapes=[pltpu.VMEM((tm, d), BF16)],
        compiler_params=_params("parallel", "arbitrary", vmem_limit=FFN_VMEM_LIMIT),
        name="ffn",
    )(x, pre_g, w_gate, w_up, w_down, post_g)


def _proj_shared(h_ref, pre_ref, win_ref, wkr_ref, qg_ref, kvg_ref, wuq_ref, cos_ref, sin_ref,
                 ckv_ref, kr_ref, ksb_ref, vsb_ref, gb_ref, uc_ref, qsb_ref, ksbh_ref, vsbh_ref, fill_depth):
    def put(ref, rows, value):
        if fill_depth is None:
            ref[rows, :] = value
        else:
            for slab in range(fill_depth):
                ref[slab, rows, :] = value

    xn = _rms(h_ref[...], pre_ref[...]).astype(BF16)
    cos = cos_ref[...]
    sin = sin_ref[...]

    def col(off, width):
        return _dot_nt(xn, win_ref[off:off + width, :])

    c_kv = _rms(col(_OFF_CKV, KV_LORA), kvg_ref[...])
    put(ckv_ref, slice(None), c_kv)
    kr_both = _dot_nt(xn, wkr_ref[...])
    k_rope = kr_both[:, :ROPE_SLOT] * cos + kr_both[:, ROPE_SLOT:] * sin
    put(kr_ref, slice(None), k_rope[:, :ROPE_DIM])

    gb_ref[...] = col(_OFF_GB, CONV_DIM)
    uc_ref[...] = col(_OFF_GC, CONV_DIM) * col(_OFF_XC, CONV_DIM)

    qsb_ref[...] = col(_OFF_QSB, SB_WIDTH).astype(BF16)
    k_sb = col(_OFF_KSB, SB_WIDTH)
    v_sb = col(_OFF_VSB, SB_WIDTH)
    tm = k_sb.shape[0]
    for h, hs in enumerate(_SB_HEAD_COLS):
        put(ksb_ref, pl.ds(h, tm, stride=SB_HEADS), k_sb[:, hs])
        put(vsb_ref, pl.ds(h, tm, stride=SB_HEADS), v_sb[:, hs])
    ksbh_ref[...] = k_sb.astype(BF16)
    vsbh_ref[...] = v_sb.astype(BF16)

    c_q = _rms(col(_OFF_CQ, Q_LORA), qg_ref[...]).astype(BF16)
    q_nope = _dot(c_q, wuq_ref[:, :_UQ_ROPE]).astype(BF16)
    q_rope = (_dot(c_q, wuq_ref[:, _UQ_ROPE:_UQ_ROT]) * _lane_tile(cos, MLA_HEADS)
              + _dot(c_q, wuq_ref[:, _UQ_ROT:]) * _lane_tile(sin, MLA_HEADS)).astype(BF16)
    return c_kv.astype(BF16), k_rope.astype(BF16), q_nope, q_rope


def _proj_prompt_body(h_ref, pre_ref, win_ref, wkr_ref, qg_ref, kvg_ref, wuq_ref, cos_ref, sin_ref, wuk_ref, wuv_ref,
                      *refs, fill_depth):
    ckv_ref, kr_ref, ksb_ref, vsb_ref, gb_ref, uc_ref, qsb_ref, ksbh_ref, vsbh_ref, q_ref, k_ref, v_ref = refs[-12:]
    c_kv, k_rope, q_nope, q_rope = _proj_shared(
        h_ref, pre_ref, win_ref, wkr_ref, qg_ref, kvg_ref, wuq_ref, cos_ref, sin_ref,
        ckv_ref, kr_ref, ksb_ref, vsb_ref, gb_ref, uc_ref, qsb_ref, ksbh_ref, vsbh_ref, fill_depth)
    k_nope = _dot(c_kv, wuk_ref[...]).astype(BF16)
    v = _dot(c_kv, wuv_ref[...]).astype(BF16)
    for h, hs in enumerate(_MLA_HEAD_COLS):
        q_ref[h, :, :NOPE_DIM] = q_nope[:, hs]
        q_ref[h, :, NOPE_DIM:] = q_rope[:, hs]
        k_ref[h, :, :NOPE_DIM] = k_nope[:, hs]
        k_ref[h, :, NOPE_DIM:] = k_rope
        v_ref[h] = v[:, hs]


def _proj_sample_body(h_ref, pre_ref, win_ref, wkr_ref, qg_ref, kvg_ref, wuq_ref, cos_ref, sin_ref, wukt_ref, *refs,
                      fill_depth):
    ckv_ref, kr_ref, ksb_ref, vsb_ref, gb_ref, uc_ref, qsb_ref, ksbh_ref, vsbh_ref, q_ref, kv_ref = refs[-11:]
    c_kv, k_rope, q_nope, q_rope = _proj_shared(
        h_ref, pre_ref, win_ref, wkr_ref, qg_ref, kvg_ref, wuq_ref, cos_ref, sin_ref,
        ckv_ref, kr_ref, ksb_ref, vsb_ref, gb_ref, uc_ref, qsb_ref, ksbh_ref, vsbh_ref, fill_depth)
    kv_ref[:, :KV_LORA] = c_kv
    kv_ref[:, KV_LORA:] = k_rope
    for h, hs in enumerate(_MLA_HEAD_COLS):
        q_ref[h, :, :KV_LORA] = _dot(q_nope[:, hs], wukt_ref[h]).astype(BF16)
        q_ref[h, :, KV_LORA:] = q_rope[:, hs]


def _proj(body, h, shared, extra, cos, sin, state, layer, tm, attn_shapes, attn_specs):
    r, d = h.shape
    creates = isinstance(state[0], jax.ShapeDtypeStruct)
    aliased = () if creates else tuple(state)
    n_tab = cos.shape[0] // tm
    row = lambda w: pl.BlockSpec((tm, w), lambda i: (i, 0))
    table = pl.BlockSpec((tm, ROPE_SLOT), lambda i: (i % n_tab, 0))
    slab = lambda a: _layer_slab(a, layer, 1)
    if creates:
        state_spec = lambda a: pl.BlockSpec((a.shape[0], a.shape[1] // (r // tm), a.shape[2]), lambda i: (0, i, 0))
    else:
        state_spec = lambda a: pl.BlockSpec((None, a.shape[1] // (r // tm), a.shape[2]), lambda i: (layer, i, 0))
    sds = jax.ShapeDtypeStruct
    n_in = 1 + len(shared) + 2 + len(extra)
    outs = pl.pallas_call(
        functools.partial(body, fill_depth=state[0].shape[0] if creates else None),
        grid=(r // tm,),
        in_specs=([row(d)] + [slab(a) for a in shared] + [table, table] + [slab(a) for a in extra]
                  + [pl.BlockSpec(memory_space=pl.ANY)] * len(aliased)),
        out_specs=([state_spec(a) for a in state]
                   + [row(CONV_DIM), row(CONV_DIM), row(SB_WIDTH), row(SB_WIDTH), row(SB_WIDTH)] + attn_specs),
        out_shape=([sds(a.shape, a.dtype) for a in state]
                   + [sds((r, CONV_DIM), F32), sds((r, CONV_DIM), F32), sds((r, SB_WIDTH), BF16),
                      sds((r, SB_WIDTH), BF16), sds((r, SB_WIDTH), BF16)] + attn_shapes),
        input_output_aliases={n_in + k: k for k in range(len(aliased))},
        compiler_params=_params("parallel"),
        name=body.__name__.strip("_").replace("_body", ""),
    )(h, *shared, cos, sin, *extra, *aliased)
    return outs[:len(state)], outs[len(state):]


def _proj_prompt(h, shared, w_uk, w_uv, cos, sin, state, layer, tm):
    r = h.shape[0]
    heads3 = lambda w: pl.BlockSpec((MLA_HEADS, tm, w), lambda i: (0, i, 0))
    sds = jax.ShapeDtypeStruct
    return _proj(_proj_prompt_body, h, shared, (w_uk, w_uv), cos, sin, state, layer, tm,
                 [sds((MLA_HEADS, r, QK_W), BF16), sds((MLA_HEADS, r, QK_W), BF16), sds((MLA_HEADS, r, V_DIM), BF16)],
                 [heads3(QK_W), heads3(QK_W), heads3(V_DIM)])


def _proj_sample(h, shared, w_ukt, cos, sin, state, layer, tm):
    r = h.shape[0]
    sds = jax.ShapeDtypeStruct
    return _proj(_proj_sample_body, h, shared, (w_ukt,), cos, sin, state, layer, tm,
                 [sds((MLA_HEADS, r, KV_W), BF16), sds((r, KV_W), BF16)],
                 [pl.BlockSpec((MLA_HEADS, tm, KV_W), lambda i: (0, i, 0)), pl.BlockSpec((tm, KV_W), lambda i: (i, 0))])


def _mla_prompt_body(q_ref, k_ref, v_ref, o_ref, m_ref, l_ref, acc_ref, *, tq, tk):
    qi = pl.program_id(1)
    m_ref[...] = jnp.full_like(m_ref, NEG_INF)
    l_ref[...] = jnp.zeros_like(l_ref)
    acc_ref[...] = jnp.zeros_like(acc_ref)

    def step(kts, masked):
        keys = [pl.ds(pl.multiple_of(kt * tk, tk), tk) for kt in kts]
        if masked:
            qpos = qi * tq + lax.broadcasted_iota(jnp.int32, (tq, 1), 0)
            kpos = [kt * tk + lax.broadcasted_iota(jnp.int32, (1, tk), 1) for kt in kts]
            visible = [(kp >> _CHUNK_SHIFT) <= (qpos >> _CHUNK_SHIFT) for kp in kpos]

        def scores(h):
            s = [_dot_nt(q_ref[h], k_ref[h, kk, :]) * (MLA_SCALE * _LOG2E) for kk in keys]
            return [jnp.where(vis, x, NEG_INF) for vis, x in zip(visible, s)] if masked else s

        s_next = scores(0)
        for h in range(MLA_HEADS):
            s = s_next
            if h + 1 < MLA_HEADS:
                s_next = scores(h + 1)
            rs = slice(h * tq, (h + 1) * tq)
            m_old = m_ref[rs]
            m_new = functools.reduce(jnp.maximum, [m_old] + [jnp.max(x, axis=-1, keepdims=True) for x in s])
            alpha = jnp.exp2(m_old - m_new)
            p = [jnp.exp2(x - _lane_tile(m_new, tk // LANES)) for x in s]
            l_ref[rs] = alpha * l_ref[rs] + sum(x[:, c * LANES:(c + 1) * LANES] for x in p for c in range(tk // LANES))
            acc_ref[rs] = alpha * acc_ref[rs] + sum(_dot(x.astype(BF16), v_ref[h, kk, :]) for x, kk in zip(p, keys))
            m_ref[rs] = m_new

    n_full = (qi * tq) // tk
    lax.fori_loop(0, n_full // 2, lambda i, c: (step([2 * i, 2 * i + 1], False), c)[1], 0)

    @pl.when(n_full % 2 == 1)
    def _():
        step([n_full - 1], False)

    for d in range(-(-tq // tk)):
        step([n_full + d], True)

    for h in range(MLA_HEADS):
        rs = slice(h * tq, (h + 1) * tq)
        o_ref[:, h * V_DIM:(h + 1) * V_DIM] = acc_ref[rs] / jnp.sum(l_ref[rs], axis=-1, keepdims=True)


def _mla_prompt(q, k, v, batch, seq, tq, tk):
    nq = seq // tq
    rows = MLA_HEADS * tq
    whole_batch = lambda w: pl.BlockSpec((MLA_HEADS, seq, w), lambda b, i: (0, b, 0), pipeline_mode=pl.Buffered(1))
    return pl.pallas_call(
        functools.partial(_mla_prompt_body, tq=tq, tk=tk),
        grid=(batch, nq),
        in_specs=[
            pl.BlockSpec((MLA_HEADS, tq, QK_W), lambda b, i: (0, b * nq + i, 0)),
            whole_batch(QK_W), whole_batch(V_DIM),
        ],
        out_specs=pl.BlockSpec((tq, MLA_WIDTH), lambda b, i: (b * nq + i, 0)),
        out_shape=jax.ShapeDtypeStruct((batch * seq, MLA_WIDTH), F32),
        scratch_shapes=[pltpu.VMEM((rows, LANES), F32), pltpu.VMEM((rows, LANES), F32),
                        pltpu.VMEM((rows, V_DIM), F32)],
        compiler_params=_params("parallel", "arbitrary"),
        name="mla_prompt",
    )(q, k, v)


def _mla_up(o, wuv_ref, o_ref, tq):
    for h in range(MLA_HEADS):
        oh = o[h * tq:(h + 1) * tq].astype(BF16)
        o_ref[:, h * V_DIM:(h + 1) * V_DIM] = _dot(oh, wuv_ref[h])


def _mla_sample_body(q_ref, kvn_ref, cc_ref, ckr_ref, wuv_ref, o_ref, *, sq, tk):
    rows = MLA_HEADS * sq
    q = q_ref[...].reshape(rows, KV_W)
    q_lat, q_rope = q[:, :KV_LORA], q[:, KV_LORA:KV_LORA + ROPE_DIM]
    scale = MLA_SCALE * _LOG2E

    def cache_tile(t):
        keys = slice(t * tk, (t + 1) * tk)
        cc = cc_ref[keys, :].astype(BF16)
        return (_dot_nt(q_lat, cc) + _dot(q_rope, ckr_ref[:, keys].astype(BF16))) * scale, cc

    def new_tile():
        kvn = kvn_ref[...]
        return _dot_nt(q, kvn) * scale, kvn[:, :KV_LORA]

    tiles = [functools.partial(cache_tile, t) for t in range(cc_ref.shape[0] // tk)] + [new_tile]
    m = jnp.full((rows, 1), NEG_INF, F32)
    l = jnp.zeros((rows, 1), F32)
    acc = jnp.zeros((rows, KV_LORA), F32)
    nxt = tiles[0]()
    for i in range(len(tiles)):
        s, values = nxt
        if i + 1 < len(tiles):
            nxt = tiles[i + 1]()
        m_new = jnp.maximum(m, jnp.max(s, axis=-1, keepdims=True))
        alpha = jnp.exp2(m - m_new)
        p = jnp.exp2(s - m_new)
        l = alpha * l + jnp.sum(p, axis=-1, keepdims=True)
        acc = alpha * acc + _dot(p.astype(BF16), values)
        m = m_new
    _mla_up(acc / l, wuv_ref, o_ref, sq)


def _mla_sample(q, kv, cache_ckv, cache_kr_t, w_uv, layer, dec_batch, sq):
    past = cache_ckv.shape[2]
    assert past % CHUNK == 0 and sq <= CHUNK
    return pl.pallas_call(
        functools.partial(_mla_sample_body, sq=sq, tk=_pick_tile(512, past)),
        grid=(dec_batch,),
        in_specs=[
            pl.BlockSpec((MLA_HEADS, sq, KV_W), lambda b: (0, b, 0)),
            pl.BlockSpec((sq, KV_W), lambda b: (b, 0)),
            pl.BlockSpec((None, None, past, KV_LORA), lambda b: (layer, b, 0, 0)),
            pl.BlockSpec((None, None, ROPE_DIM, past), lambda b: (layer, b, 0, 0)),
            _layer_slab(w_uv, layer, 1),
        ],
        out_specs=pl.BlockSpec((sq, MLA_WIDTH), lambda b: (b, 0)),
        out_shape=jax.ShapeDtypeStruct((dec_batch * sq, MLA_WIDTH), F32),
        compiler_params=_params("parallel"),
        name="mla_sample",
    )(q, kv, cache_ckv, cache_kr_t, w_uv)


def _suffix_ones(n):
    j = lax.broadcasted_iota(jnp.int32, (n, n), 0)
    k = lax.broadcasted_iota(jnp.int32, (n, n), 1)
    return jnp.where(j > k, -1.0, 0.0).astype(BF16)


def _sb_tiles(qs, ks, vs, carries, ones, causal, right_of=None):
    log_beta, sums, totals = _sb_keep(_sb_logits(qs, ks), ones, causal)
    return _sb_apply(log_beta, sums, totals, carries, vs, causal, right_of), totals


def _sb_logits(qs, ks):
    return [_dot_nt(q, k) * (SB_SCALE * _LOG2E) for q, k in zip(qs, ks)]


def _sb_keep(logits, ones, causal):
    log_beta, drop, split = [], [], []
    for x in logits:
        sp = jnp.maximum(x, 0.0) + jnp.log(1.0 + jnp.exp2(-jnp.abs(x))) * _LOG2E
        log_beta.append(x - sp)
        dr = sp if causal is None else jnp.where(causal, sp, 0.0)
        hi = dr.astype(BF16)
        lo = (dr - hi.astype(F32)).astype(BF16)
        drop.append(dr)
        split.append(jnp.concatenate([hi, lo], axis=0))
    sums = [_dot(x, ones) for x in split]
    return log_beta, sums, [-jnp.sum(dr, axis=-1, keepdims=True) for dr in drop]


def _sb_apply(log_beta, sums, totals, carries, vs, causal, right_of):
    rows = log_beta[0].shape[0]
    weights = []
    for i, (lb, sm, carry) in enumerate(zip(log_beta, sums, carries)):
        if right_of is not None and right_of[i] is not None:
            carry = carry + totals[right_of[i]]
        between = sm[:rows] + sm[rows:] + carry
        w = jnp.exp2(lb + between)
        weights.append(w if causal is None else jnp.where(causal, w, 0.0))
    return [_dot(w.astype(BF16), v) for w, v in zip(weights, vs)]


def _sb_prompt_body(q_ref, k_ref, v_ref, o_ref, carry_ref, *, t):
    qi = pl.program_id(1)
    ones = _suffix_ones(t)
    causal = lax.broadcasted_iota(jnp.int32, (t, t), 1) < lax.broadcasted_iota(jnp.int32, (t, t), 0)
    o_ref[...] = jnp.zeros_like(o_ref)
    carry_ref[...] = jnp.zeros_like(carry_ref)

    def tiles(kts, mask):
        rows = [pl.ds(pl.multiple_of(kt * t, t), t) for kt in kts]
        chains = [(r, hs) for r in rows for hs in _SB_HEAD_COLS]
        carries = [_lane_tile(carry_ref[:, hs], t // LANES) for _, hs in chains]
        right_of = [None] * SB_HEADS + list(range(len(chains) - SB_HEADS))
        outs, tots = _sb_tiles([q_ref[:, hs] for _, hs in chains], [k_ref[r, hs] for r, hs in chains],
                               [v_ref[r, hs] for r, hs in chains], carries, ones, mask, right_of)
        for (_, hs), out, tot in zip(chains, outs, tots):
            o_ref[:, hs] += out
            carry_ref[:, hs] += tot

    tiles([qi], causal)
    lax.fori_loop(0, qi // 2, lambda i, c: (tiles([qi - 1 - 2 * i, qi - 2 - 2 * i], None), c)[1], 0)

    @pl.when(qi % 2 == 1)
    def _():
        tiles([0], None)


def _sb_prompt(q, k, v, batch, seq, t):
    nq = seq // t
    return pl.pallas_call(
        functools.partial(_sb_prompt_body, t=t),
        grid=(batch, nq),
        in_specs=[
            pl.BlockSpec((t, SB_WIDTH), lambda b, i: (b * nq + i, 0)),
            pl.BlockSpec((seq, SB_WIDTH), lambda b, i: (b, 0)),
            pl.BlockSpec((seq, SB_WIDTH), lambda b, i: (b, 0)),
        ],
        out_specs=pl.BlockSpec((t, SB_WIDTH), lambda b, i: (b * nq + i, 0)),
        out_shape=jax.ShapeDtypeStruct((batch * seq, SB_WIDTH), F32),
        scratch_shapes=[pltpu.VMEM((t, SB_HEADS * LANES), F32)],
        compiler_params=_params("parallel", "arbitrary"),
        name="sb_prompt",
    )(q, k, v)


def _sb_sample_body(q_ref, kn_ref, vn_ref, ck_ref, cv_ref, o_ref, *, sq, past, tk):
    ones_new = _suffix_ones(sq)
    ones = _suffix_ones(tk)
    causal = lax.broadcasted_iota(jnp.int32, (sq, sq), 1) < lax.broadcasted_iota(jnp.int32, (sq, sq), 0)
    qs = [q_ref[:, hs] for hs in _SB_HEAD_COLS]
    accs, carries = _sb_tiles(qs, [kn_ref[:, hs] for hs in _SB_HEAD_COLS], [vn_ref[:, hs] for hs in _SB_HEAD_COLS],
                              [0.0] * SB_HEADS, ones_new, causal)
    n_tiles = past // tk
    group = 2 if n_tiles % 2 == 0 else 1
    for g in reversed(range(n_tiles // group)):
        kts = [g * group + group - 1 - i for i in range(group)]
        rows = [pl.ds(kt * tk * SB_HEADS + h, tk, stride=SB_HEADS) for kt in kts for h in range(SB_HEADS)]
        right_of = [None] * SB_HEADS + list(range(len(rows) - SB_HEADS))
        outs, tots = _sb_tiles(qs * group, [ck_ref[r, :].astype(BF16) for r in rows],
                               [cv_ref[r, :].astype(BF16) for r in rows], carries * group, ones, None, right_of)
        for i in range(len(rows)):
            accs[i % SB_HEADS] = accs[i % SB_HEADS] + outs[i]
            carries[i % SB_HEADS] = carries[i % SB_HEADS] + tots[i]
    for hs, acc in zip(_SB_HEAD_COLS, accs):
        o_ref[:, hs] = acc


def _sb_sample(q, k, v, cache_k, cache_v, layer, dec_batch, sq):
    past = cache_k.shape[2] // SB_HEADS
    new = lambda: pl.BlockSpec((sq, SB_WIDTH), lambda b: (b, 0))
    old = lambda: pl.BlockSpec((None, None, past * SB_HEADS, SB_DIM), lambda b: (layer, b, 0, 0))
    return pl.pallas_call(
        functools.partial(_sb_sample_body, sq=sq, past=past, tk=_pick_tile(256, past)),
        grid=(dec_batch,),
        in_specs=[new(), new(), new(), old(), old()],
        out_specs=pl.BlockSpec((sq, SB_WIDTH), lambda b: (b, 0)),
        out_shape=jax.ShapeDtypeStruct((dec_batch * sq, SB_WIDTH), F32),
        compiler_params=_params("parallel"),
        name="sb_sample",
    )(q, k, v, cache_k, cache_v)


def _merge_body(h_ref, ya_ref, gb_ref, uc_ref, prev_ref, yc_ref, cw_ref, ga_ref, gconv_ref, gc_ref,
                wo_ref, post_ref, o_ref, yb_ref, *, tm):
    ya = _rms(ya_ref[...], ga_ref[...]).astype(BF16)
    m = _dot(ya, wo_ref[:MLA_WIDTH, :])
    yc = _rms(yc_ref[...], gc_ref[...]).astype(BF16)
    m += _dot(yc, wo_ref[MLA_WIDTH + CONV_DIM:, :])
    cw = cw_ref[...]
    row = lax.broadcasted_iota(jnp.int32, (SEG, 1), 0)
    for s in range(tm // SEG):
        rows = slice(s * SEG, (s + 1) * SEG)
        u = uc_ref[rows, :]
        prev = prev_ref[s]
        u1 = jnp.where(row == 0, prev[1:2], pltpu.roll(u, 1, 0))
        u2 = jnp.where(row == 0, prev[0:1], jnp.where(row == 1, prev[1:2], pltpu.roll(u, 2, 0)))
        conv = cw[0:1] * u2 + cw[1:2] * u1 + cw[2:3] * u
        yb_ref[rows, :] = _rms(gb_ref[rows, :] * conv, gconv_ref[...]).astype(BF16)
    m += _dot(yb_ref[...], wo_ref[MLA_WIDTH:MLA_WIDTH + CONV_DIM, :])
    o_ref[...] = h_ref[...] + _rms(m, post_ref[...])


def _merge(h, ya, gb, uc, prev, yc, conv_w, ga, gconv, gc, w_o, post_g, layer, tm):
    r, d = h.shape
    row = lambda w: pl.BlockSpec((tm, w), lambda i: (i, 0))
    slab = lambda a: _layer_slab(a, layer, 1)
    return pl.pallas_call(
        functools.partial(_merge_body, tm=tm),
        grid=(r // tm,),
        in_specs=[
            row(d), row(MLA_WIDTH), row(CONV_DIM), row(CONV_DIM),
            pl.BlockSpec((tm // SEG, CONV_W - 1, CONV_DIM), lambda i: (i, 0, 0)),
            row(SB_WIDTH), slab(conv_w), slab(ga), slab(gconv), slab(gc), slab(w_o), slab(post_g),
        ],
        out_specs=row(d),
        out_shape=jax.ShapeDtypeStruct((r, d), F32),
        scratch_shapes=[pltpu.VMEM((tm, CONV_DIM), BF16)],
        compiler_params=_params("parallel"),
        name="merge",
    )(h, ya, gb, uc, prev, yc, conv_w, ga, gconv, gc, w_o, post_g)


def _rot_half(w):
    half = ROPE_DIM // 2
    return jnp.concatenate([-w[..., half:], w[..., :half]], axis=-1)


def _pad_slot(w):
    return jnp.pad(w, [(0, 0)] * (w.ndim - 1) + [(0, ROPE_SLOT - w.shape[-1])])


def _pack_w_in(w):
    wt = jnp.swapaxes(w, 1, 2)
    half = ROPE_DIM // 2
    k_rope = wt[:, _OFF_KROPE:_OFF_KROPE + ROPE_DIM]
    rot = jnp.concatenate([-k_rope[:, half:], k_rope[:, :half]], axis=1)
    pad = lambda a: jnp.pad(a, ((0, 0), (0, ROPE_SLOT - ROPE_DIM), (0, 0)))
    return wt.astype(BF16), jnp.concatenate([pad(k_rope), pad(rot)], axis=1).astype(BF16)


def _pack_w_uq(w):
    depth = w.shape[0]
    w = w.reshape(depth, Q_LORA, MLA_HEADS, NOPE_DIM + ROPE_DIM)
    flat = lambda a: a.reshape(depth, Q_LORA, -1)
    rope = w[..., NOPE_DIM:]
    return jnp.concatenate([flat(w[..., :NOPE_DIM]), flat(_pad_slot(rope)), flat(_pad_slot(_rot_half(rope)))],
                           axis=-1).astype(BF16)


def _rope_tables(pos):
    half = ROPE_DIM // 2
    inv_freq = ROPE_THETA ** (-jnp.arange(half, dtype=F32) / half)
    ang = pos.astype(F32)[:, None] * inv_freq[None, :]
    pad = lambda t: jnp.pad(jnp.concatenate([t, t], axis=1), ((0, 0), (0, ROPE_SLOT - ROPE_DIM)))
    return pad(jnp.cos(ang)), pad(jnp.sin(ang))


def kernel(x_prompt, x_sample, cache_mla_ckv, cache_mla_krope, cache_sb_k, cache_sb_v, state_conv, ffn1_pre_g, ffn1_w_gate, ffn1_w_up, ffn1_w_down, ffn1_post_g, mix_pre_g, w_in, mla_q_norm_g, mla_kv_norm_g, mla_w_uq, mla_w_uk, mla_w_uv, conv_w, out_norm_mla_g, out_norm_conv_g, out_norm_sb_g, w_o, mix_post_g, ffn2_pre_g, ffn2_w_gate, ffn2_w_up, ffn2_w_down, ffn2_post_g):
    batch, seq, d = x_prompt.shape
    dec_batch, sq, _ = x_sample.shape
    depth, _, past, _ = cache_mla_ckv.shape
    rp, rs = batch * seq, dec_batch * sq
    assert d == D_MODEL and sq == SEG and seq % SEG == 0

    tm_ffn = _pick_tile(1024, rp, rs)
    tf_ffn = _pick_tile(256, ffn1_w_gate.shape[2])
    tm = _pick_tile(256, seq, rs)
    tm_merge = _pick_tile(512, seq, rs)
    t_attn = _pick_tile(256, seq)

    cos_p, sin_p = _rope_tables(jnp.arange(seq, dtype=jnp.int32))
    cos_s, sin_s = _rope_tables(jnp.tile(past + jnp.arange(sq, dtype=jnp.int32), tm // sq))
    cache_kr_t = jnp.swapaxes(cache_mla_krope, 2, 3)
    cache_k = cache_sb_k.reshape(depth, dec_batch, past * SB_HEADS, SB_DIM)
    cache_v = cache_sb_v.reshape(depth, dec_batch, past * SB_HEADS, SB_DIM)
    gain = lambda g: g.reshape(depth, 1, -1)

    shared = (gain(mix_pre_g), *_pack_w_in(w_in), gain(mla_q_norm_g), gain(mla_kv_norm_g), _pack_w_uq(mla_w_uq))
    w_ukt = jnp.transpose(mla_w_uk, (0, 2, 3, 1)).astype(BF16)
    w_uv = jnp.transpose(mla_w_uv, (0, 2, 1, 3)).astype(BF16)
    w_uk_flat = mla_w_uk.reshape(depth, KV_LORA, MLA_HEADS * NOPE_DIM).astype(BF16)
    w_uv_flat = mla_w_uv.reshape(depth, KV_LORA, MLA_WIDTH).astype(BF16)
    merge_w = (conv_w, gain(out_norm_mla_g), gain(out_norm_conv_g), gain(out_norm_sb_g), w_o.astype(BF16),
               gain(mix_post_g))

    def new_state(rows):
        sds = jax.ShapeDtypeStruct
        return (sds((depth, rows, KV_LORA), F32), sds((depth, rows, ROPE_DIM), F32),
                sds((depth, rows * SB_HEADS, SB_DIM), F32), sds((depth, rows * SB_HEADS, SB_DIM), F32))

    h_p, h_s = x_prompt.reshape(rp, d), x_sample.reshape(rs, d)
    state_p, state_s = new_state(rp), new_state(rs)
    conv_p, conv_s = [], []
    for l in range(depth):
        ffn1 = lambda x: _ffn(x, gain(ffn1_pre_g), ffn1_w_gate, ffn1_w_up, ffn1_w_down, gain(ffn1_post_g), l,
                              tm_ffn, tf_ffn)
        ffn2 = lambda x: _ffn(x, gain(ffn2_pre_g), ffn2_w_gate, ffn2_w_up, ffn2_w_down, gain(ffn2_post_g), l,
                              tm_ffn, tf_ffn)
        h_p, h_s = ffn1(h_p), ffn1(h_s)

        state_p, (gb, uc, qsb, ksbh, vsbh, q, k, v) = _proj_prompt(h_p, shared, w_uk_flat, w_uv_flat, cos_p, sin_p,
                                                                  state_p, l, tm)
        ya = _mla_prompt(q, k, v, batch, seq, t_attn, _pick_tile(2 * t_attn, seq))
        yc = _sb_prompt(qsb, ksbh, vsbh, batch, seq, t_attn)
        tails = uc.reshape(batch, seq // SEG, SEG, CONV_DIM)[:, :, SEG - (CONV_W - 1):]
        prev = jnp.concatenate([jnp.zeros_like(tails[:, :1]), tails[:, :-1]], axis=1)
        h_p = _merge(h_p, ya, gb, uc, prev.reshape(-1, CONV_W - 1, CONV_DIM), yc, *merge_w, l, tm_merge)
        conv_p.append(tails[:, -1])

        state_s, (gb, uc, qsb, ksbh, vsbh, q, kv) = _proj_sample(h_s, shared, w_ukt, cos_s, sin_s, state_s, l, tm)
        ya = _mla_sample(q, kv, cache_mla_ckv, cache_kr_t, w_uv, l, dec_batch, sq)
        yc = _sb_sample(qsb, ksbh, vsbh, cache_k, cache_v, l, dec_batch, sq)
        h_s = _merge(h_s, ya, gb, uc, state_conv[l], yc, *merge_w, l, tm_merge)
        conv_s.append(uc.reshape(dec_batch, sq, CONV_DIM)[:, sq - (CONV_W - 1):])

        h_p, h_s = ffn2(h_p), ffn2(h_s)

    def state_out(state, lead):
        ckv, kr, sbk, sbv = state
        return (ckv.reshape(depth, *lead, KV_LORA), kr.reshape(depth, *lead, ROPE_DIM),
                sbk.reshape(depth, *lead, SB_HEADS, SB_DIM), sbv.reshape(depth, *lead, SB_HEADS, SB_DIM))

    return (h_p.reshape(batch, seq, d), h_s.reshape(dec_batch, sq, d),
            *state_out(state_p, (batch, seq)), jnp.stack(conv_p, axis=0),
            *state_out(state_s, (dec_batch, sq)), jnp.stack(conv_s, axis=0))
```

```python
import functools

import jax
import jax.numpy as jnp
from jax import lax
from jax.experimental import pallas as pl
from jax.experimental.pallas import tpu as pltpu

F32 = jnp.float32
BF16 = jnp.bfloat16

D_MODEL = 2048
CHUNK = 64
EPS = 1e-6
MLA_HEADS = 8
Q_LORA = 512
KV_LORA = 512
NOPE_DIM = 128
ROPE_DIM = 64
V_DIM = 128
ROPE_THETA = 10000.0
MLA_WIDTH = MLA_HEADS * V_DIM
CONV_DIM = 512
CONV_W = 3
SB_HEADS = 4
SB_DIM = 128
SB_WIDTH = SB_HEADS * SB_DIM
_SB_HEAD_COLS = [slice(h * SB_DIM, (h + 1) * SB_DIM) for h in range(SB_HEADS)]
MLA_SCALE = (NOPE_DIM + ROPE_DIM) ** -0.5
SB_SCALE = SB_DIM ** -0.5
NEG_INF = -1e30
_LOG2E = 1.4426950408889634
_CHUNK_SHIFT = CHUNK.bit_length() - 1
assert 1 << _CHUNK_SHIFT == CHUNK

LANES = 128
ROPE_SLOT = LANES
KV_W = KV_LORA + ROPE_SLOT
QK_W = NOPE_DIM + ROPE_SLOT
SEG = 64
VMEM_LIMIT = 56 * 1024 * 1024
FFN_VMEM_LIMIT = 60 * 1024 * 1024
_MLA_HEAD_COLS = [slice(h * LANES, (h + 1) * LANES) for h in range(MLA_HEADS)]
assert NOPE_DIM == LANES and V_DIM == LANES

_OFF_CQ, _OFF_CKV, _OFF_KROPE = 0, Q_LORA, Q_LORA + KV_LORA
_OFF_GB, _OFF_GC, _OFF_XC, _OFF_QSB, _OFF_KSB, _OFF_VSB = (_OFF_KROPE + ROPE_DIM + 512 * i for i in range(6))
assert all(off % 16 == 0 for off in (_OFF_GB, _OFF_GC, _OFF_XC, _OFF_QSB, _OFF_KSB, _OFF_VSB))
_UQ_ROPE = MLA_HEADS * NOPE_DIM
_UQ_ROT = _UQ_ROPE + MLA_HEADS * ROPE_SLOT
W_UQ_PACKED = _UQ_ROT + MLA_HEADS * ROPE_SLOT


def _params(*sem, vmem_limit=VMEM_LIMIT):
    return pltpu.CompilerParams(dimension_semantics=sem, vmem_limit_bytes=vmem_limit)


def _layer_slab(a, layer, n_grid, single_buffer=True):
    zeros = (0,) * (a.ndim - 1)
    index_map = (lambda i: (layer,) + zeros) if n_grid == 1 else (lambda i, j: (layer,) + zeros)
    return pl.BlockSpec((None,) + a.shape[1:], index_map, pipeline_mode=pl.Buffered(1) if single_buffer else None)


def _rms(x, g):
    return x * lax.rsqrt(jnp.mean(x * x, axis=-1, keepdims=True) + EPS) * g


def _dot(a, b):
    return jnp.dot(a, b, preferred_element_type=F32)


def _dot_nt(a, b):
    return lax.dot_general(a, b, (((1,), (1,)), ((), ())), preferred_element_type=F32)


def _lane_tile(x, n):
    return jnp.concatenate([x] * n, axis=1)


def _pick_tile(cap, *sizes):
    t = cap
    while any(s % t for s in sizes):
        t //= 2
    return t


def _ffn_body(x_ref, pre_ref, wg_ref, wu_ref, wd_ref, post_ref, o_ref, xn_ref):
    j = pl.program_id(1)
    last = pl.num_programs(1) - 1

    def chunk(xn):
        a = jax.nn.silu(_dot(xn, wg_ref[...].astype(BF16))) * _dot(xn, wu_ref[...].astype(BF16))
        return _dot(a.astype(BF16), wd_ref[...].astype(BF16))

    @pl.when(j == 0)
    def _():
        xn = _rms(x_ref[...], pre_ref[...]).astype(BF16)
        xn_ref[...] = xn
        o_ref[...] = chunk(xn)

    @pl.when(jnp.logical_and(j > 0, j < last))
    def _():
        o_ref[...] += chunk(xn_ref[...])

    @pl.when(j == last)
    def _():
        acc = o_ref[...] + chunk(xn_ref[...])
        o_ref[...] = x_ref[...] + 0.5 * _rms(acc, post_ref[...])


def _ffn(x, pre_g, w_gate, w_up, w_down, post_g, layer, tm, tf):
    r, d = x.shape
    dff = w_gate.shape[2]
    return pl.pallas_call(
        _ffn_body,
        grid=(r // tm, dff // tf),
        in_specs=[
            pl.BlockSpec((tm, d), lambda i, j: (i, 0)),
            _layer_slab(pre_g, layer, 2),
            pl.BlockSpec((None, d, tf), lambda i, j: (layer, 0, j)),
            pl.BlockSpec((None, d, tf), lambda i, j: (layer, 0, j)),
            pl.BlockSpec((None, tf, d), lambda i, j: (layer, j, 0)),
            _layer_slab(post_g, layer, 2),
        ],
        out_specs=pl.BlockSpec((tm, d), lambda i, j: (i, 0)),
        out_shape=jax.ShapeDtypeStruct((r, d), F32),
        scratch_shapes=[pltpu.VMEM((tm, d), BF16)],
        compiler_params=_params("parallel", "arbitrary", vmem_limit=FFN_VMEM_LIMIT),
        name="ffn",
    )(x, pre_g, w_gate, w_up, w_down, post_g)


def _proj_shared(h_ref, pre_ref, win_ref, wkr_ref, qg_ref, kvg_ref, wuq_ref, cos_ref, sin_ref,
                 ckv_ref, kr_ref, ksb_ref, vsb_ref, gb_ref, uc_ref, qsb_ref, ksbh_ref, vsbh_ref, fill_depth):
    def put(ref, rows, value):
        if fill_depth is None:
            ref[rows, :] = value
        else:
            for slab in range(fill_depth):
                ref[slab, rows, :] = value

    xn = _rms(h_ref[...], pre_ref[...]).astype(BF16)
    cos = cos_ref[...]
    sin = sin_ref[...]

    def col(off, width):
        return _dot_nt(xn, win_ref[off:off + width, :])

    c_kv = _rms(col(_OFF_CKV, KV_LORA), kvg_ref[...])
    put(ckv_ref, slice(None), c_kv)
    kr_both = _dot_nt(xn, wkr_ref[...])
    k_rope = kr_both[:, :ROPE_SLOT] * cos + kr_both[:, ROPE_SLOT:] * sin
    put(kr_ref, slice(None), k_rope[:, :ROPE_DIM])

    gb_ref[...] = col(_OFF_GB, CONV_DIM)
    uc_ref[...] = col(_OFF_GC, CONV_DIM) * col(_OFF_XC, CONV_DIM)

    qsb_ref[...] = col(_OFF_QSB, SB_WIDTH).astype(BF16)
    k_sb = col(_OFF_KSB, SB_WIDTH)
    v_sb = col(_OFF_VSB, SB_WIDTH)
    tm = k_sb.shape[0]
    for h, hs in enumerate(_SB_HEAD_COLS):
        put(ksb_ref, pl.ds(h, tm, stride=SB_HEADS), k_sb[:, hs])
        put(vsb_ref, pl.ds(h, tm, stride=SB_HEADS), v_sb[:, hs])
    ksbh_ref[...] = k_sb.astype(BF16)
    vsbh_ref[...] = v_sb.astype(BF16)

    c_q = _rms(col(_OFF_CQ, Q_LORA), qg_ref[...]).astype(BF16)
    q_nope = _dot(c_q, wuq_ref[:, :_UQ_ROPE]).astype(BF16)
    q_rope = (_dot(c_q, wuq_ref[:, _UQ_ROPE:_UQ_ROT]) * _lane_tile(cos, MLA_HEADS)
              + _dot(c_q, wuq_ref[:, _UQ_ROT:]) * _lane_tile(sin, MLA_HEADS)).astype(BF16)
    return c_kv.astype(BF16), k_rope.astype(BF16), q_nope, q_rope


def _proj_prompt_body(h_ref, pre_ref, win_ref, wkr_ref, qg_ref, kvg_ref, wuq_ref, cos_ref, sin_ref, wuk_ref, wuv_ref,
                      *refs, fill_depth):
    ckv_ref, kr_ref, ksb_ref, vsb_ref, gb_ref, uc_ref, qsb_ref, ksbh_ref, vsbh_ref, q_ref, k_ref, v_ref = refs[-12:]
    c_kv, k_rope, q_nope, q_rope = _proj_shared(
        h_ref, pre_ref, win_ref, wkr_ref, qg_ref, kvg_ref, wuq_ref, cos_ref, sin_ref,
        ckv_ref, kr_ref, ksb_ref, vsb_ref, gb_ref, uc_ref, qsb_ref, ksbh_ref, vsbh_ref, fill_depth)
    k_nope = _dot(c_kv, wuk_ref[...]).astype(BF16)
    v = _dot(c_kv, wuv_ref[...]).astype(BF16)
    for h, hs in enumerate(_MLA_HEAD_COLS):
        q_ref[h, :, :NOPE_DIM] = q_nope[:, hs]
        q_ref[h, :, NOPE_DIM:] = q_rope[:, hs]
        k_ref[h, :, :NOPE_DIM] = k_nope[:, hs]
        k_ref[h, :, NOPE_DIM:] = k_rope
        v_ref[h] = v[:, hs]


def _proj_sample_body(h_ref, pre_ref, win_ref, wkr_ref, qg_ref, kvg_ref, wuq_ref, cos_ref, sin_ref, wukt_ref, *refs,
                      fill_depth):
    ckv_ref, kr_ref, ksb_ref, vsb_ref, gb_ref, uc_ref, qsb_ref, ksbh_ref, vsbh_ref, q_ref, kv_ref = refs[-11:]
    c_kv, k_rope, q_nope, q_rope = _proj_shared(
        h_ref, pre_ref, win_ref, wkr_ref, qg_ref, kvg_ref, wuq_ref, cos_ref, sin_ref,
        ckv_ref, kr_ref, ksb_ref, vsb_ref, gb_ref, uc_ref, qsb_ref, ksbh_ref, vsbh_ref, fill_depth)
    kv_ref[:, :KV_LORA] = c_kv
    kv_ref[:, KV_LORA:] = k_rope
    for h, hs in enumerate(_MLA_HEAD_COLS):
        q_ref[h, :, :KV_LORA] = _dot(q_nope[:, hs], wukt_ref[h]).astype(BF16)
        q_ref[h, :, KV_LORA:] = q_rope[:, hs]


def _proj(body, h, shared, extra, cos, sin, state, layer, tm, attn_shapes, attn_specs):
    r, d = h.shape
    creates = isinstance(state[0], jax.ShapeDtypeStruct)
    aliased = () if creates else tuple(state)
    n_tab = cos.shape[0] // tm
    row = lambda w: pl.BlockSpec((tm, w), lambda i: (i, 0))
    table = pl.BlockSpec((tm, ROPE_SLOT), lambda i: (i % n_tab, 0))
    slab = lambda a: _layer_slab(a, layer, 1)
    if creates:
        state_spec = lambda a: pl.BlockSpec((a.shape[0], a.shape[1] // (r // tm), a.shape[2]), lambda i: (0, i, 0))
    else:
        state_spec = lambda a: pl.BlockSpec((None, a.shape[1] // (r // tm), a.shape[2]), lambda i: (layer, i, 0))
    sds = jax.ShapeDtypeStruct
    n_in = 1 + len(shared) + 2 + len(extra)
    outs = pl.pallas_call(
        functools.partial(body, fill_depth=state[0].shape[0] if creates else None),
        grid=(r // tm,),
        in_specs=([row(d)] + [slab(a) for a in shared] + [table, table] + [slab(a) for a in extra]
                  + [pl.BlockSpec(memory_space=pl.ANY)] * len(aliased)),
        out_specs=([state_spec(a) for a in state]
                   + [row(CONV_DIM), row(CONV_DIM), row(SB_WIDTH), row(SB_WIDTH), row(SB_WIDTH)] + attn_specs),
        out_shape=([sds(a.shape, a.dtype) for a in state]
                   + [sds((r, CONV_DIM), F32), sds((r, CONV_DIM), F32), sds((r, SB_WIDTH), BF16),
                      sds((r, SB_WIDTH), BF16), sds((r, SB_WIDTH), BF16)] + attn_shapes),
        input_output_aliases={n_in + k: k for k in range(len(aliased))},
        compiler_params=_params("parallel"),
        name=body.__name__.strip("_").replace("_body", ""),
    )(h, *shared, cos, sin, *extra, *aliased)
    return outs[:len(state)], outs[len(state):]


def _proj_prompt(h, shared, w_uk, w_uv, cos, sin, state, layer, tm):
    r = h.shape[0]
    heads3 = lambda w: pl.BlockSpec((MLA_HEADS, tm, w), lambda i: (0, i, 0))
    sds = jax.ShapeDtypeStruct
    return _proj(_proj_prompt_body, h, shared, (w_uk, w_uv), cos, sin, state, layer, tm,
                 [sds((MLA_HEADS, r, QK_W), BF16), sds((MLA_HEADS, r, QK_W), BF16), sds((MLA_HEADS, r, V_DIM), BF16)],
                 [heads3(QK_W), heads3(QK_W), heads3(V_DIM)])


def _proj_sample(h, shared, w_ukt, cos, sin, state, layer, tm):
    r = h.shape[0]
    sds = jax.ShapeDtypeStruct
    return _proj(_proj_sample_body, h, shared, (w_ukt,), cos, sin, state, layer, tm,
                 [sds((MLA_HEADS, r, KV_W), BF16), sds((r, KV_W), BF16)],
                 [pl.BlockSpec((MLA_HEADS, tm, KV_W), lambda i: (0, i, 0)), pl.BlockSpec((tm, KV_W), lambda i: (i, 0))])


def _mla_prompt_body(q_ref, k_ref, v_ref, o_ref, m_ref, l_ref, acc_ref, *, tq, tk):
    qi = pl.program_id(1)
    m_ref[...] = jnp.full_like(m_ref, NEG_INF)
    l_ref[...] = jnp.zeros_like(l_ref)
    acc_ref[...] = jnp.zeros_like(acc_ref)

    def step(starts, width, masked):
        keys = [pl.ds(pl.multiple_of(start, width), width) for start in starts]
        if masked:
            qpos = qi * tq + lax.broadcasted_iota(jnp.int32, (tq, 1), 0)
            kpos = [start + lax.broadcasted_iota(jnp.int32, (1, width), 1) for start in starts]
            visible = [(kp >> _CHUNK_SHIFT) <= (qpos >> _CHUNK_SHIFT) for kp in kpos]

        def scores(h):
            s = [_dot_nt(q_ref[h], k_ref[h, kk, :]) * (MLA_SCALE * _LOG2E) for kk in keys]
            return [jnp.where(vis, x, NEG_INF) for vis, x in zip(visible, s)] if masked else s

        s_next = scores(0)
        for h in range(MLA_HEADS):
            s = s_next
            if h + 1 < MLA_HEADS:
                s_next = scores(h + 1)
            rs = slice(h * tq, (h + 1) * tq)
            m_old = m_ref[rs]
            m_new = functools.reduce(jnp.maximum, [m_old] + [jnp.max(x, axis=-1, keepdims=True) for x in s])
            alpha = jnp.exp2(m_old - m_new)
            p = [jnp.exp2(x - _lane_tile(m_new, width // LANES)) for x in s]
            l_ref[rs] = alpha * l_ref[rs] + sum(x[:, c * LANES:(c + 1) * LANES] for x in p
                                                for c in range(width // LANES))
            acc_ref[rs] = alpha * acc_ref[rs] + sum(_dot(x.astype(BF16), v_ref[h, kk, :]) for x, kk in zip(p, keys))
            m_ref[rs] = m_new

    n_full = (qi * tq) // tk
    lax.fori_loop(0, n_full // 2, lambda i, c: (step([2 * i * tk, (2 * i + 1) * tk], tk, False), c)[1], 0)

    @pl.when(n_full % 2 == 1)
    def _():
        step([(n_full - 1) * tk], tk, False)

    @pl.when(qi * tq > n_full * tk)
    def _():
        step([n_full * tk], tq, False)

    step([qi * tq], tq, True)

    for h in range(MLA_HEADS):
        rs = slice(h * tq, (h + 1) * tq)
        o_ref[:, h * V_DIM:(h + 1) * V_DIM] = acc_ref[rs] / jnp.sum(l_ref[rs], axis=-1, keepdims=True)


def _mla_prompt(q, k, v, batch, seq, tq, tk):
    assert tk in (tq, 2 * tq)
    nq = seq // tq
    rows = MLA_HEADS * tq
    whole_batch = lambda w: pl.BlockSpec((MLA_HEADS, seq, w), lambda b, i: (0, b, 0), pipeline_mode=pl.Buffered(1))
    return pl.pallas_call(
        functools.partial(_mla_prompt_body, tq=tq, tk=tk),
        grid=(batch, nq),
        in_specs=[
            pl.BlockSpec((MLA_HEADS, tq, QK_W), lambda b, i: (0, b * nq + i, 0)),
            whole_batch(QK_W), whole_batch(V_DIM),
        ],
        out_specs=pl.BlockSpec((tq, MLA_WIDTH), lambda b, i: (b * nq + i, 0)),
        out_shape=jax.ShapeDtypeStruct((batch * seq, MLA_WIDTH), F32),
        scratch_shapes=[pltpu.VMEM((rows, LANES), F32), pltpu.VMEM((rows, LANES), F32),
                        pltpu.VMEM((rows, V_DIM), F32)],
        compiler_params=_params("parallel", "arbitrary"),
        name="mla_prompt",
    )(q, k, v)


def _mla_up(o, wuv_ref, o_ref, tq):
    for h in range(MLA_HEADS):
        oh = o[h * tq:(h + 1) * tq].astype(BF16)
        o_ref[:, h * V_DIM:(h + 1) * V_DIM] = _dot(oh, wuv_ref[h])


def _mla_sample_body(q_ref, kvn_ref, cc_ref, ckr_ref, wuv_ref, o_ref, *, sq, tk):
    rows = MLA_HEADS * sq
    q = q_ref[...].reshape(rows, KV_W)
    q_lat, q_rope = q[:, :KV_LORA], q[:, KV_LORA:KV_LORA + ROPE_DIM]
    scale = MLA_SCALE * _LOG2E

    def cache_tile(t):
        keys = slice(t * tk, (t + 1) * tk)
        cc = cc_ref[keys, :].astype(BF16)
        return (_dot_nt(q_lat, cc) + _dot(q_rope, ckr_ref[:, keys].astype(BF16))) * scale, cc

    def new_tile():
        kvn = kvn_ref[...]
        return _dot_nt(q, kvn) * scale, kvn[:, :KV_LORA]

    tiles = [functools.partial(cache_tile, t) for t in range(cc_ref.shape[0] // tk)] + [new_tile]
    m = jnp.full((rows, 1), NEG_INF, F32)
    l = jnp.zeros((rows, 1), F32)
    acc = jnp.zeros((rows, KV_LORA), F32)
    nxt = tiles[0]()
    for i in range(len(tiles)):
        s, values = nxt
        if i + 1 < len(tiles):
            nxt = tiles[i + 1]()
        m_new = jnp.maximum(m, jnp.max(s, axis=-1, keepdims=True))
        alpha = jnp.exp2(m - m_new)
        p = jnp.exp2(s - m_new)
        l = alpha * l + jnp.sum(p, axis=-1, keepdims=True)
        acc = alpha * acc + _dot(p.astype(BF16), values)
        m = m_new
    _mla_up(acc / l, wuv_ref, o_ref, sq)


def _mla_sample(q, kv, cache_ckv, cache_kr_t, w_uv, layer, dec_batch, sq):
    past = cache_ckv.shape[2]
    assert past % CHUNK == 0 and sq <= CHUNK
    return pl.pallas_call(
        functools.partial(_mla_sample_body, sq=sq, tk=_pick_tile(512, past)),
        grid=(dec_batch,),
        in_specs=[
            pl.BlockSpec((MLA_HEADS, sq, KV_W), lambda b: (0, b, 0)),
            pl.BlockSpec((sq, KV_W), lambda b: (b, 0)),
            pl.BlockSpec((None, None, past, KV_LORA), lambda b: (layer, b, 0, 0)),
            pl.BlockSpec((None, None, ROPE_DIM, past), lambda b: (layer, b, 0, 0)),
            _layer_slab(w_uv, layer, 1),
        ],
        out_specs=pl.BlockSpec((sq, MLA_WIDTH), lambda b: (b, 0)),
        out_shape=jax.ShapeDtypeStruct((dec_batch * sq, MLA_WIDTH), F32),
        compiler_params=_params("parallel"),
        name="mla_sample",
    )(q, kv, cache_ckv, cache_kr_t, w_uv)


def _suffix_ones(n):
    j = lax.broadcasted_iota(jnp.int32, (n, n), 0)
    k = lax.broadcasted_iota(jnp.int32, (n, n), 1)
    return jnp.where(j > k, -1.0, 0.0).astype(BF16)


def _sb_tiles(qs, ks, vs, carries, ones, causal, right_of=None):
    log_beta, sums, totals = _sb_keep(_sb_logits(qs, ks), ones, causal)
    return _sb_apply(log_beta, sums, totals, carries, vs, causal, right_of), totals


def _sb_logits(qs, ks):
    return [_dot_nt(q, k) * (SB_SCALE * _LOG2E) for q, k in zip(qs, ks)]


def _sb_keep(logits, ones, causal):
    log_beta, drop, split = [], [], []
    for x in logits:
        sp = jnp.maximum(x, 0.0) + jnp.log(1.0 + jnp.exp2(-jnp.abs(x))) * _LOG2E
        log_beta.append(x - sp)
        dr = sp if causal is None else jnp.where(causal, sp, 0.0)
        hi = dr.astype(BF16)
        lo = (dr - hi.astype(F32)).astype(BF16)
        drop.append(dr)
        split.append(jnp.concatenate([hi, lo], axis=0))
    sums = [_dot(x, ones) for x in split]
    return log_beta, sums, [-jnp.sum(dr, axis=-1, keepdims=True) for dr in drop]


def _sb_apply(log_beta, sums, totals, carries, vs, causal, right_of):
    rows = log_beta[0].shape[0]
    weights = []
    for i, (lb, sm, carry) in enumerate(zip(log_beta, sums, carries)):
        if right_of is not None and right_of[i] is not None:
            carry = carry + totals[right_of[i]]
        between = sm[:rows] + sm[rows:] + carry
        w = jnp.exp2(lb + between)
        weights.append(w if causal is None else jnp.where(causal, w, 0.0))
    return [_dot(w.astype(BF16), v) for w, v in zip(weights, vs)]


def _sb_prompt_body(q_ref, k_ref, v_ref, o_ref, carry_ref, *, t):
    qi = pl.program_id(1)
    ones = _suffix_ones(t)
    causal = lax.broadcasted_iota(jnp.int32, (t, t), 1) < lax.broadcasted_iota(jnp.int32, (t, t), 0)
    o_ref[...] = jnp.zeros_like(o_ref)
    carry_ref[...] = jnp.zeros_like(carry_ref)

    def tiles(kts, mask):
        rows = [pl.ds(pl.multiple_of(kt * t, t), t) for kt in kts]
        chains = [(r, hs) for r in rows for hs in _SB_HEAD_COLS]
        carries = [_lane_tile(carry_ref[:, hs], t // LANES) for _, hs in chains]
        right_of = [None] * SB_HEADS + list(range(len(chains) - SB_HEADS))
        outs, tots = _sb_tiles([q_ref[:, hs] for _, hs in chains], [k_ref[r, hs] for r, hs in chains],
                               [v_ref[r, hs] for r, hs in chains], carries, ones, mask, right_of)
        for (_, hs), out, tot in zip(chains, outs, tots):
            o_ref[:, hs] += out
            carry_ref[:, hs] += tot

    tiles([qi], causal)
    lax.fori_loop(0, qi // 2, lambda i, c: (tiles([qi - 1 - 2 * i, qi - 2 - 2 * i], None), c)[1], 0)

    @pl.when(qi % 2 == 1)
    def _():
        tiles([0], None)


def _sb_prompt(q, k, v, batch, seq, t):
    nq = seq // t
    return pl.pallas_call(
        functools.partial(_sb_prompt_body, t=t),
        grid=(batch, nq),
        in_specs=[
            pl.BlockSpec((t, SB_WIDTH), lambda b, i: (b * nq + i, 0)),
            pl.BlockSpec((seq, SB_WIDTH), lambda b, i: (b, 0)),
            pl.BlockSpec((seq, SB_WIDTH), lambda b, i: (b, 0)),
        ],
        out_specs=pl.BlockSpec((t, SB_WIDTH), lambda b, i: (b * nq + i, 0)),
        out_shape=jax.ShapeDtypeStruct((batch * seq, SB_WIDTH), F32),
        scratch_shapes=[pltpu.VMEM((t, SB_HEADS * LANES), F32)],
        compiler_params=_params("parallel", "arbitrary"),
        name="sb_prompt",
    )(q, k, v)


def _sb_sample_body(q_ref, kn_ref, vn_ref, ck_ref, cv_ref, o_ref, *, sq, past, tk):
    ones_new = _suffix_ones(sq)
    ones = _suffix_ones(tk)
    causal = lax.broadcasted_iota(jnp.int32, (sq, sq), 1) < lax.broadcasted_iota(jnp.int32, (sq, sq), 0)
    qs = [q_ref[:, hs] for hs in _SB_HEAD_COLS]
    accs, carries = _sb_tiles(qs, [kn_ref[:, hs] for hs in _SB_HEAD_COLS], [vn_ref[:, hs] for hs in _SB_HEAD_COLS],
                              [0.0] * SB_HEADS, ones_new, causal)
    n_tiles = past // tk
    group = 2 if n_tiles % 2 == 0 else 1
    for g in reversed(range(n_tiles // group)):
        kts = [g * group + group - 1 - i for i in range(group)]
        rows = [pl.ds(kt * tk * SB_HEADS + h, tk, stride=SB_HEADS) for kt in kts for h in range(SB_HEADS)]
        right_of = [None] * SB_HEADS + list(range(len(rows) - SB_HEADS))
        outs, tots = _sb_tiles(qs * group, [ck_ref[r, :].astype(BF16) for r in rows],
                               [cv_ref[r, :].astype(BF16) for r in rows], carries * group, ones, None, right_of)
        for i in range(len(rows)):
            accs[i % SB_HEADS] = accs[i % SB_HEADS] + outs[i]
            carries[i % SB_HEADS] = carries[i % SB_HEADS] + tots[i]
    for hs, acc in zip(_SB_HEAD_COLS, accs):
        o_ref[:, hs] = acc


def _sb_sample(q, k, v, cache_k, cache_v, layer, dec_batch, sq):
    past = cache_k.shape[2] // SB_HEADS
    new = lambda: pl.BlockSpec((sq, SB_WIDTH), lambda b: (b, 0))
    old = lambda: pl.BlockSpec((None, None, past * SB_HEADS, SB_DIM), lambda b: (layer, b, 0, 0))
    return pl.pallas_call(
        functools.partial(_sb_sample_body, sq=sq, past=past, tk=_pick_tile(256, past)),
        grid=(dec_batch,),
        in_specs=[new(), new(), new(), old(), old()],
        out_specs=pl.BlockSpec((sq, SB_WIDTH), lambda b: (b, 0)),
        out_shape=jax.ShapeDtypeStruct((dec_batch * sq, SB_WIDTH), F32),
        compiler_params=_params("parallel"),
        name="sb_sample",
    )(q, k, v, cache_k, cache_v)


def _merge_body(h_ref, ya_ref, gb_ref, uc_ref, prev_ref, yc_ref, cw_ref, ga_ref, gconv_ref, gc_ref,
                wo_ref, post_ref, o_ref, yb_ref, *, tm):
    ya = _rms(ya_ref[...], ga_ref[...]).astype(BF16)
    m = _dot(ya, wo_ref[:MLA_WIDTH, :])
    yc = _rms(yc_ref[...], gc_ref[...]).astype(BF16)
    m += _dot(yc, wo_ref[MLA_WIDTH + CONV_DIM:, :])
    cw = cw_ref[...]
    row = lax.broadcasted_iota(jnp.int32, (SEG, 1), 0)
    for s in range(tm // SEG):
        rows = slice(s * SEG, (s + 1) * SEG)
        u = uc_ref[rows, :]
        prev = prev_ref[s]
        u1 = jnp.where(row == 0, prev[1:2], pltpu.roll(u, 1, 0))
        u2 = jnp.where(row == 0, prev[0:1], jnp.where(row == 1, prev[1:2], pltpu.roll(u, 2, 0)))
        conv = cw[0:1] * u2 + cw[1:2] * u1 + cw[2:3] * u
        yb_ref[rows, :] = _rms(gb_ref[rows, :] * conv, gconv_ref[...]).astype(BF16)
    m += _dot(yb_ref[...], wo_ref[MLA_WIDTH:MLA_WIDTH + CONV_DIM, :])
    o_ref[...] = h_ref[...] + _rms(m, post_ref[...])


def _merge(h, ya, gb, uc, prev, yc, conv_w, ga, gconv, gc, w_o, post_g, layer, tm):
    r, d = h.shape
    row = lambda w: pl.BlockSpec((tm, w), lambda i: (i, 0))
    slab = lambda a: _layer_slab(a, layer, 1)
    return pl.pallas_call(
        functools.partial(_merge_body, tm=tm),
        grid=(r // tm,),
        in_specs=[
            row(d), row(MLA_WIDTH), row(CONV_DIM), row(CONV_DIM),
            pl.BlockSpec((tm // SEG, CONV_W - 1, CONV_DIM), lambda i: (i, 0, 0)),
            row(SB_WIDTH), slab(conv_w), slab(ga), slab(gconv), slab(gc), slab(w_o), slab(post_g),
        ],
        out_specs=row(d),
        out_shape=jax.ShapeDtypeStruct((r, d), F32),
        scratch_shapes=[pltpu.VMEM((tm, CONV_DIM), BF16)],
        compiler_params=_params("parallel"),
        name="merge",
    )(h, ya, gb, uc, prev, yc, conv_w, ga, gconv, gc, w_o, post_g)


def _rot_half(w):
    half = ROPE_DIM // 2
    return jnp.concatenate([-w[..., half:], w[..., :half]], axis=-1)


def _pad_slot(w):
    return jnp.pad(w, [(0, 0)] * (w.ndim - 1) + [(0, ROPE_SLOT - w.shape[-1])])


def _pack_w_in(w):
    wt = jnp.swapaxes(w, 1, 2)
    half = ROPE_DIM // 2
    k_rope = wt[:, _OFF_KROPE:_OFF_KROPE + ROPE_DIM]
    rot = jnp.concatenate([-k_rope[:, half:], k_rope[:, :half]], axis=1)
    pad = lambda a: jnp.pad(a, ((0, 0), (0, ROPE_SLOT - ROPE_DIM), (0, 0)))
    return wt.astype(BF16), jnp.concatenate([pad(k_rope), pad(rot)], axis=1).astype(BF16)


def _pack_w_uq(w):
    depth = w.shape[0]
    w = w.reshape(depth, Q_LORA, MLA_HEADS, NOPE_DIM + ROPE_DIM)
    flat = lambda a: a.reshape(depth, Q_LORA, -1)
    rope = w[..., NOPE_DIM:]
    return jnp.concatenate([flat(w[..., :NOPE_DIM]), flat(_pad_slot(rope)), flat(_pad_slot(_rot_half(rope)))],
                           axis=-1).astype(BF16)


def _rope_tables(pos):
    half = ROPE_DIM // 2
    inv_freq = ROPE_THETA ** (-jnp.arange(half, dtype=F32) / half)
    ang = pos.astype(F32)[:, None] * inv_freq[None, :]
    pad = lambda t: jnp.pad(jnp.concatenate([t, t], axis=1), ((0, 0), (0, ROPE_SLOT - ROPE_DIM)))
    return pad(jnp.cos(ang)), pad(jnp.sin(ang))


def kernel(x_prompt, x_sample, cache_mla_ckv, cache_mla_krope, cache_sb_k, cache_sb_v, state_conv, ffn1_pre_g, ffn1_w_gate, ffn1_w_up, ffn1_w_down, ffn1_post_g, mix_pre_g, w_in, mla_q_norm_g, mla_kv_norm_g, mla_w_uq, mla_w_uk, mla_w_uv, conv_w, out_norm_mla_g, out_norm_conv_g, out_norm_sb_g, w_o, mix_post_g, ffn2_pre_g, ffn2_w_gate, ffn2_w_up, ffn2_w_down, ffn2_post_g):
    batch, seq, d = x_prompt.shape
    dec_batch, sq, _ = x_sample.shape
    depth, _, past, _ = cache_mla_ckv.shape
    rp, rs = batch * seq, dec_batch * sq
    assert d == D_MODEL and sq == SEG and seq % SEG == 0

    tm_ffn = _pick_tile(1024, rp, rs)
    tf_ffn = _pick_tile(256, ffn1_w_gate.shape[2])
    tm = _pick_tile(256, seq, rs)
    tm_merge = _pick_tile(512, seq, rs)
    t_attn = _pick_tile(256, seq)

    cos_p, sin_p = _rope_tables(jnp.arange(seq, dtype=jnp.int32))
    cos_s, sin_s = _rope_tables(jnp.tile(past + jnp.arange(sq, dtype=jnp.int32), tm // sq))
    cache_kr_t = jnp.swapaxes(cache_mla_krope, 2, 3)
    cache_k = cache_sb_k.reshape(depth, dec_batch, past * SB_HEADS, SB_DIM)
    cache_v = cache_sb_v.reshape(depth, dec_batch, past * SB_HEADS, SB_DIM)
    gain = lambda g: g.reshape(depth, 1, -1)

    shared = (gain(mix_pre_g), *_pack_w_in(w_in), gain(mla_q_norm_g), gain(mla_kv_norm_g), _pack_w_uq(mla_w_uq))
    w_ukt = jnp.transpose(mla_w_uk, (0, 2, 3, 1)).astype(BF16)
    w_uv = jnp.transpose(mla_w_uv, (0, 2, 1, 3)).astype(BF16)
    w_uk_flat = mla_w_uk.reshape(depth, KV_LORA, MLA_HEADS * NOPE_DIM).astype(BF16)
    w_uv_flat = mla_w_uv.reshape(depth, KV_LORA, MLA_WIDTH).astype(BF16)
    merge_w = (conv_w, gain(out_norm_mla_g), gain(out_norm_conv_g), gain(out_norm_sb_g), w_o.astype(BF16),
               gain(mix_post_g))

    def new_state(rows):
        sds = jax.ShapeDtypeStruct
        return (sds((depth, rows, KV_LORA), F32), sds((depth, rows, ROPE_DIM), F32),
                sds((depth, rows * SB_HEADS, SB_DIM), F32), sds((depth, rows * SB_HEADS, SB_DIM), F32))

    h_p, h_s = x_prompt.reshape(rp, d), x_sample.reshape(rs, d)
    state_p, state_s = new_state(rp), new_state(rs)
    conv_p, conv_s = [], []
    for l in range(depth):
        ffn1 = lambda x: _ffn(x, gain(ffn1_pre_g), ffn1_w_gate, ffn1_w_up, ffn1_w_down, gain(ffn1_post_g), l,
                              tm_ffn, tf_ffn)
        ffn2 = lambda x: _ffn(x, gain(ffn2_pre_g), ffn2_w_gate, ffn2_w_up, ffn2_w_down, gain(ffn2_post_g), l,
                              tm_ffn, tf_ffn)
        h_p, h_s = ffn1(h_p), ffn1(h_s)

        state_p, (gb, uc, qsb, ksbh, vsbh, q, k, v) = _proj_prompt(h_p, shared, w_uk_flat, w_uv_flat, cos_p, sin_p,
                                                                  state_p, l, tm)
        ya = _mla_prompt(q, k, v, batch, seq, t_attn, _pick_tile(2 * t_attn, seq))
        yc = _sb_prompt(qsb, ksbh, vsbh, batch, seq, t_attn)
        tails = uc.reshape(batch, seq // SEG, SEG, CONV_DIM)[:, :, SEG - (CONV_W - 1):]
        prev = jnp.concatenate([jnp.zeros_like(tails[:, :1]), tails[:, :-1]], axis=1)
        h_p = _merge(h_p, ya, gb, uc, prev.reshape(-1, CONV_W - 1, CONV_DIM), yc, *merge_w, l, tm_merge)
        conv_p.append(tails[:, -1])

        state_s, (gb, uc, qsb, ksbh, vsbh, q, kv) = _proj_sample(h_s, shared, w_ukt, cos_s, sin_s, state_s, l, tm)
        ya = _mla_sample(q, kv, cache_mla_ckv, cache_kr_t, w_uv, l, dec_batch, sq)
        yc = _sb_sample(qsb, ksbh, vsbh, cache_k, cache_v, l, dec_batch, sq)
        h_s = _merge(h_s, ya, gb, uc, state_conv[l], yc, *merge_w, l, tm_merge)
        conv_s.append(uc.reshape(dec_batch, sq, CONV_DIM)[:, sq - (CONV_W - 1):])

        h_p, h_s = ffn2(h_p), ffn2(h_s)

    def state_out(state, lead):
        ckv, kr, sbk, sbv = state
        return (ckv.reshape(depth, *lead, KV_LORA), kr.reshape(depth, *lead, ROPE_DIM),
                sbk.reshape(depth, *lead, SB_HEADS, SB_DIM), sbv.reshape(depth, *lead, SB_HEADS, SB_DIM))

    return (h_p.reshape(batch, seq, d), h_s.reshape(dec_batch, sq, d),
            *state_out(state_p, (batch, seq)), jnp.stack(conv_p, axis=0),
            *state_out(state_s, (dec_batch, sq)), jnp.stack(conv_s, axis=0))
```
